```python
import math
import jax, jax.numpy as jnp
from jax import lax
import numpy as np

D_MODEL = 1024
BATCH = 8
SEQ = 4096
DEPTH = 2
DEC_BATCH = 32
DEC_SEQ = 8
PAST_LEN = 16384
PAGE_SIZE = 128

HEAD_DIM = 64
N_HEAD_SLOTS = 8
DIL_GROUPS = ((128, 1), (512, 4), (2048, 16))
N_DIL = len(DIL_GROUPS)
ATTN_WIDTH = N_HEAD_SLOTS * HEAD_DIM
QKV_WIDTH = 3 * N_DIL * ATTN_WIDTH
ROPE_THETA = 10000.0
BLOCK_Q = 128
SSM_WIDTH = D_MODEL
SSM_GROUP = 16
SSM_GROUPS = SSM_WIDTH // SSM_GROUP
SSM_STATE = 64
SCAN_CHUNK = 128
D_FF = -(-8 * D_MODEL // (3 * 256)) * 256
RMS_EPS = 1e-6
N_ATTN_LAYERS = (DEPTH + 1) // 2
N_SSM_LAYERS = DEPTH // 2

kernel_name = 'dilated_attn_s5_hybrid_step'


def rmsnorm(x, g):
    xf = x.astype(jnp.float32)
    y = xf * lax.rsqrt(jnp.mean(xf * xf, axis=-1, keepdims=True) + RMS_EPS)
    return (y * g.astype(jnp.float32)).astype(x.dtype)


def rope(x, pos):
    half = HEAD_DIM // 2
    inv = ROPE_THETA ** (-jnp.arange(half, dtype=jnp.float32) / half)
    ang = pos.astype(jnp.float32)[:, None] * inv[None, :]
    cos = jnp.cos(ang)[None, :, None, None, :]
    sin = jnp.sin(ang)[None, :, None, None, :]
    xf = x.astype(jnp.float32)
    x1, x2 = xf[..., :half], xf[..., half:]
    return jnp.concatenate([x1 * cos - x2 * sin, x2 * cos + x1 * sin], axis=-1).astype(x.dtype)


def dilated_block(q_blk, kv_all, qi, dilation, n_keys):
    idx = qi[:, None] - dilation * jnp.arange(n_keys)[None, :]
    valid = idx >= 0
    kvg = jnp.take(kv_all, jnp.maximum(idx, 0), axis=1)
    kf = kvg[:, :, :, 0].astype(jnp.float32)
    vf = kvg[:, :, :, 1].astype(jnp.float32)
    s = jnp.einsum('bqhd,bqkhd->bqhk', q_blk.astype(jnp.float32), kf) * (HEAD_DIM ** -0.5)
    s = jnp.where(valid[None, :, None, :], s, -jnp.inf)
    lse = jax.nn.logsumexp(s, axis=-1)
    p = jnp.exp(s - lse[..., None])
    o = jnp.einsum('bqhk,bqkhd->bqhd', p, vf)
    return o, lse


def attention_layer(h, pos, bufs, w_qkv, q_gain, k_gain, w_o):
    B, T, _ = h.shape
    qkv = (h @ w_qkv).reshape(B, T, 3, N_DIL, N_HEAD_SLOTS, HEAD_DIM)
    q = rope(rmsnorm(qkv[:, :, 0], q_gain), pos)
    k = rope(rmsnorm(qkv[:, :, 1], k_gain), pos)
    v = qkv[:, :, 2]
    kv_alls, new_bufs, offs = [], [], []
    for g, (win, dil) in enumerate(DIL_GROUPS):
        kv_new = jnp.stack([k[:, :, g], v[:, :, g]], axis=2)
        kv_all = jnp.concatenate([bufs[g].astype(kv_new.dtype), kv_new], axis=1)
        kv_alls.append(kv_all)
        offs.append(bufs[g].shape[1])
        keep = min(win, kv_all.shape[1])
        new_bufs.append(kv_all[:, kv_all.shape[1] - keep:])
    qb = BLOCK_Q if T % BLOCK_Q == 0 else T
    n_blocks = T // qb

    def block(i):
        outs, lses = [], []
        for g, (win, dil) in enumerate(DIL_GROUPS):
            q_blk = lax.dynamic_slice_in_dim(q[:, :, g], i * qb, qb, axis=1)
            qi = offs[g] + i * qb + jnp.arange(qb)
            o, l = dilated_block(q_blk, kv_alls[g], qi, dil, win // dil + 1)
            outs.append(o)
            lses.append(l)
        wts = jax.nn.softmax(jnp.stack(lses, axis=0), axis=0)
        return jnp.einsum('gbqh,gbqhd->bqhd', wts, jnp.stack(outs, axis=0))

    o = lax.map(block, jnp.arange(n_blocks))
    o = jnp.moveaxis(o, 0, 1).reshape(B, T, ATTN_WIDTH).astype(h.dtype)
    return o @ w_o, new_bufs


def ssm_layer(h, state0, w_in, a_re, a_im, log_dt, b_re, b_im, c_re, c_im, d_skip, w_glu):
    B, T, _ = h.shape
    f32 = jnp.float32
    u = (h @ w_in).astype(f32)
    lam = lax.complex(a_re.astype(f32), a_im.astype(f32))
    dt = jnp.exp(log_dt.astype(f32))[:, None]
    abar = jnp.exp(lam * dt)
    bmat = lax.complex(b_re.astype(f32), b_im.astype(f32))
    bbar = ((abar - 1.0) / lam)[..., None] * bmat
    cmat = lax.complex(c_re.astype(f32), c_im.astype(f32))
    ch = SCAN_CHUNK if T % SCAN_CHUNK == 0 else T
    nc = T // ch
    uc = jnp.moveaxis(u.reshape(B, nc, ch, SSM_GROUPS, SSM_GROUP), 1, 0)

    def combine(e1, e2):
        a1, b1 = e1
        a2, b2 = e2
        return a1 * a2, a2 * b1 + b2

    def step(hc, uk):
        bu = jnp.einsum('gpc,btgc->btgp', bbar, uk.astype(jnp.complex64))
        a_el = jnp.broadcast_to(abar, bu.shape)
        a_cum, x0 = lax.associative_scan(combine, (a_el, bu), axis=1)
        xs = x0 + a_cum * hc[:, None]
        y = jnp.einsum('gcp,btgp->btgc', cmat, xs).real
        return xs[:, -1], y

    h_last, ys = lax.scan(step, state0, uc)
    y = jnp.moveaxis(ys, 0, 1).reshape(B, T, SSM_WIDTH) + d_skip.astype(f32) * u
    g = jax.nn.gelu(y).astype(h.dtype)
    z = g @ w_glu
    val, gate = jnp.split(z, 2, axis=-1)
    return val * jax.nn.sigmoid(gate), h_last


def swiglu(h, w_gate, w_up, w_down):
    return (jax.nn.silu(h @ w_gate) * (h @ w_up)) @ w_down


def trunk(x, pos, attn_bufs, ssm_states, p):
    new_attn, new_ssm = [], []
    for i in range(DEPTH):
        h = rmsnorm(x, p['norm_mix'][i])
        if i % 2 == 0:
            la = i // 2
            out, nb = attention_layer(h, pos, attn_bufs[la], p['w_qkv'][la], p['q_norm'][la],
                                      p['k_norm'][la], p['w_o'][la])
            new_attn.append(nb)
        else:
            lb = i // 2
            out, st = ssm_layer(h, ssm_states[lb], p['ssm_w_in'][lb], p['ssm_a_re'][lb],
                                p['ssm_a_im'][lb], p['ssm_log_dt'][lb], p['ssm_b_re'][lb],
                                p['ssm_b_im'][lb], p['ssm_c_re'][lb], p['ssm_c_im'][lb],
                                p['ssm_d'][lb], p['ssm_w_glu'][lb])
            new_ssm.append(st)
        x = x + out.astype(x.dtype)
        h2 = rmsnorm(x, p['norm_ffn'][i])
        x = x + swiglu(h2, p['ffn_w_gate'][i], p['ffn_w_up'][i], p['ffn_w_down'][i]).astype(x.dtype)
    return x, new_attn, new_ssm


def _stack_kv(kv_layers, g):
    return jnp.stack([layer[g] for layer in kv_layers], axis=0)


def _stack_state(states, dtype):
    return jnp.stack([jnp.stack([s.real, s.imag], axis=-1) for s in states], axis=0).astype(dtype)


def setup_inputs(seed: int = 0) -> dict:
    key = jax.random.key(seed)
    ks = jax.random.split(key, 26)
    f32 = jnp.float32

    def nrm(k, shape, scale):
        return jax.random.normal(k, shape, f32) * scale

    na, nb = N_ATTN_LAYERS, N_SSM_LAYERS

    def cache_shape(win):
        return (na, DEC_BATCH, min(win, PAST_LEN), 2, N_HEAD_SLOTS, HEAD_DIM)

    n_idx = jnp.arange(SSM_STATE, dtype=f32)
    G, P, C = SSM_GROUPS, SSM_STATE, SSM_GROUP
    return {
        'x_prompt': nrm(ks[0], (BATCH, SEQ, D_MODEL), 1.0),
        'x_sample': nrm(ks[1], (DEC_BATCH, DEC_SEQ, D_MODEL), 1.0),
        'cache_kv_w128': nrm(ks[2], cache_shape(DIL_GROUPS[0][0]), 1.0),
        'cache_kv_w512': nrm(ks[3], cache_shape(DIL_GROUPS[1][0]), 1.0),
        'cache_kv_w2048': nrm(ks[4], cache_shape(DIL_GROUPS[2][0]), 1.0),
        'state_ssm': nrm(ks[5], (nb, DEC_BATCH, G, P, 2), 1.0),
        'norm_mix': 1.0 + nrm(ks[6], (DEPTH, D_MODEL), 0.02),
        'norm_ffn': 1.0 + nrm(ks[7], (DEPTH, D_MODEL), 0.02),
        'w_qkv': nrm(ks[8], (na, D_MODEL, QKV_WIDTH), D_MODEL ** -0.5),
        'q_norm': 1.0 + nrm(ks[9], (na, HEAD_DIM), 0.02),
        'k_norm': 1.0 + nrm(ks[10], (na, HEAD_DIM), 0.02),
        'w_o': nrm(ks[11], (na, ATTN_WIDTH, D_MODEL), ATTN_WIDTH ** -0.5),
        'ssm_w_in': nrm(ks[12], (nb, D_MODEL, SSM_WIDTH), D_MODEL ** -0.5),
        'ssm_a_re': -0.5 + nrm(ks[13], (nb, G, P), 0.01),
        'ssm_a_im': math.pi * n_idx + nrm(ks[14], (nb, G, P), 0.01),
        'ssm_log_dt': jax.random.uniform(ks[15], (nb, G), f32, math.log(1e-3), math.log(1e-1)),
        'ssm_b_re': nrm(ks[16], (nb, G, P, C), (2 * C) ** -0.5),
        'ssm_b_im': nrm(ks[17], (nb, G, P, C), (2 * C) ** -0.5),
        'ssm_c_re': nrm(ks[18], (nb, G, C, P), (2 * P) ** -0.5),
        'ssm_c_im': nrm(ks[19], (nb, G, C, P), (2 * P) ** -0.5),
        'ssm_d': nrm(ks[20], (nb, SSM_WIDTH), 1.0),
        'ssm_w_glu': nrm(ks[21], (nb, SSM_WIDTH, 2 * D_MODEL), SSM_WIDTH ** -0.5),
        'ffn_w_gate': nrm(ks[22], (DEPTH, D_MODEL, D_FF), D_MODEL ** -0.5),
        'ffn_w_up': nrm(ks[23], (DEPTH, D_MODEL, D_FF), D_MODEL ** -0.5),
        'ffn_w_down': nrm(ks[24], (DEPTH, D_FF, D_MODEL), D_FF ** -0.5),
    }


def reference(x_prompt, x_sample, cache_kv_w128, cache_kv_w512, cache_kv_w2048, state_ssm,
              norm_mix, norm_ffn, w_qkv, q_norm, k_norm, w_o,
              ssm_w_in, ssm_a_re, ssm_a_im, ssm_log_dt, ssm_b_re, ssm_b_im, ssm_c_re, ssm_c_im,
              ssm_d, ssm_w_glu, ffn_w_gate, ffn_w_up, ffn_w_down):
    p = dict(norm_mix=norm_mix, norm_ffn=norm_ffn, w_qkv=w_qkv, q_norm=q_norm, k_norm=k_norm,
             w_o=w_o, ssm_w_in=ssm_w_in, ssm_a_re=ssm_a_re, ssm_a_im=ssm_a_im,
             ssm_log_dt=ssm_log_dt, ssm_b_re=ssm_b_re, ssm_b_im=ssm_b_im, ssm_c_re=ssm_c_re,
             ssm_c_im=ssm_c_im, ssm_d=ssm_d, ssm_w_glu=ssm_w_glu, ffn_w_gate=ffn_w_gate,
             ffn_w_up=ffn_w_up, ffn_w_down=ffn_w_down)
    bp = x_prompt.shape[0]
    empty = jnp.zeros((bp, 0, 2, N_HEAD_SLOTS, HEAD_DIM), x_prompt.dtype)
    prompt_bufs = [[empty for _ in range(N_DIL)] for _ in range(N_ATTN_LAYERS)]
    prompt_states = [jnp.zeros((bp, SSM_GROUPS, SSM_STATE), jnp.complex64) for _ in range(N_SSM_LAYERS)]
    pos_p = jnp.arange(x_prompt.shape[1])
    y_prompt, kv_p, st_p = trunk(x_prompt, pos_p, prompt_bufs, prompt_states, p)
    caches = (cache_kv_w128, cache_kv_w512, cache_kv_w2048)
    sample_bufs = [[c[l] for c in caches] for l in range(N_ATTN_LAYERS)]
    sample_states = [lax.complex(state_ssm[l, ..., 0].astype(jnp.float32),
                                 state_ssm[l, ..., 1].astype(jnp.float32)) for l in range(N_SSM_LAYERS)]
    pos_s = PAST_LEN + jnp.arange(x_sample.shape[1])
    y_sample, kv_s, st_s = trunk(x_sample, pos_s, sample_bufs, sample_states, p)
    return (y_prompt, y_sample,
            _stack_kv(kv_p, 0), _stack_kv(kv_p, 1), _stack_kv(kv_p, 2),
            _stack_state(st_p, x_prompt.dtype),
            _stack_kv(kv_s, 0), _stack_kv(kv_s, 1), _stack_kv(kv_s, 2),
            _stack_state(st_s, x_sample.dtype))
```

```python
import functools
import math

import numpy as np
import jax
import jax.numpy as jnp
from jax import lax
from jax.experimental import pallas as pl
from jax.experimental.pallas import tpu as pltpu

F32 = jnp.float32
BF16 = jnp.bfloat16

D_MODEL = 1024
HEAD_DIM = 64
N_HEADS = 8
ATTN_W = N_HEADS * HEAD_DIM
DILATIONS = (1, 4, 16)
WINDOWS = (128, 512, 2048)
KEYS_BACK = 128
ROPE_THETA = 10000.0
RMS_EPS = 1e-6
SSM_GROUP = 16
SSM_GROUPS = D_MODEL // SSM_GROUP
SSM_STATE = 64
SSM_CHUNK = 16
ROW_TILE = 256
P16 = 16
NEG_BIG = -1e30
VMEM_LIMIT = 56 * 1024 * 1024


def _cparams(*sem):
    return pltpu.CompilerParams(dimension_semantics=tuple(sem), vmem_limit_bytes=VMEM_LIMIT)


def _const_spec(shape):
    nd = len(shape)
    return pl.BlockSpec(shape, lambda *_: (0,) * nd)


def _rms(x, gain):
    ms = jnp.mean(x * x, axis=-1, keepdims=True)
    return x * lax.rsqrt(ms + RMS_EPS) * gain


def _qkv_slabs(x_ref, g_ref, w_ref, perm_ref, ones_ref, qg_ref, kg_ref, cos_ref, sin_ref, permute):
    h = _rms(x_ref[...], g_ref[...]).astype(BF16)
    if permute:
        h = jnp.dot(perm_ref[...], h, preferred_element_type=F32).astype(BF16)
    cos = jnp.concatenate([cos_ref[...]] * 4, axis=1)
    sin = jnp.concatenate([sin_ref[...]] * 4, axis=1)
    lane = lax.broadcasted_iota(jnp.int32, (ROW_TILE, ATTN_W), 1)
    first_half = (lane & (HEAD_DIM - 1)) < (HEAD_DIM // 2)
    ones = ones_ref[...]
    out = {}
    for s in range(3):
        for g in range(3):
            col = (s * 3 + g) * ATTN_W
            y = jnp.dot(h, w_ref[:, col:col + ATTN_W], preferred_element_type=F32)
            if s < 2:
                gain = (qg_ref if s == 0 else kg_ref)[...]
                yy = (y * y).astype(BF16)
                ss = jnp.concatenate(
                    [jnp.dot(yy[:, :256], ones, preferred_element_type=F32),
                     jnp.dot(yy[:, 256:], ones, preferred_element_type=F32)], axis=1)
                yn = y * lax.rsqrt(ss * (1.0 / HEAD_DIM) + RMS_EPS) * gain
                swapped = jnp.where(first_half,
                                    pltpu.roll(yn, ATTN_W - HEAD_DIM // 2, 1),
                                    pltpu.roll(yn, HEAD_DIM // 2, 1))
                y = yn * cos + swapped * sin
            out[(s, g)] = y
    return out


def _qkv_prompt_kernel(x_ref, g_ref, w_ref, perm_ref, ones_ref, qg_ref, kg_ref, cos_ref, sin_ref,
                       q_ref, k_ref, v_ref, t0_ref, t1_ref, t2_ref):
    slabs = _qkv_slabs(x_ref, g_ref, w_ref, perm_ref, ones_ref, qg_ref, kg_ref, cos_ref, sin_ref, True)
    tails = (t0_ref, t1_ref, t2_ref)
    for g in range(3):
        lo, hi = g * ATTN_W, (g + 1) * ATTN_W
        for s, ref in enumerate((q_ref, k_ref, v_ref)):
            ref[:, :, lo:hi] = slabs[(s, g)].reshape(P16, P16, ATTN_W).astype(BF16)
        k3 = slabs[(1, g)].reshape(P16, P16, ATTN_W)
        v3 = slabs[(2, g)].reshape(P16, P16, ATTN_W)
        if g == 0:
            k3, v3 = k3[:, 8:, :], v3[:, 8:, :]
        tails[g][:, :, :ATTN_W] = k3
        tails[g][:, :, ATTN_W:] = v3


def _qkv_sample_kernel(x_ref, g_ref, w_ref, perm_ref, ones_ref, qg_ref, kg_ref, cos_ref, sin_ref,
                       q_ref, k_ref, v_ref):
    slabs = _qkv_slabs(x_ref, g_ref, w_ref, perm_ref, ones_ref, qg_ref, kg_ref, cos_ref, sin_ref, False)
    for g in range(3):
        lo, hi = g * ATTN_W, (g + 1) * ATTN_W
        for s, ref in enumerate((q_ref, k_ref, v_ref)):
            ref[:, lo:hi] = slabs[(s, g)]


def _qkv_common_inputs(norm_g, w_qkv_bf, q_gain, k_gain):
    perm = np.zeros((ROW_TILE, ROW_TILE), np.float32)
    m = np.arange(ROW_TILE)
    perm[m, P16 * (m % P16) + m // P16] = 1.0
    ones = np.kron(np.eye(4, dtype=np.float32), np.ones((HEAD_DIM, HEAD_DIM), np.float32))
    return (norm_g.reshape(1, D_MODEL), w_qkv_bf, jnp.asarray(perm, BF16), jnp.asarray(ones, BF16),
            jnp.tile(q_gain, N_HEADS).reshape(1, ATTN_W), jnp.tile(k_gain, N_HEADS).reshape(1, ATTN_W))


def _rope_tables(pos):
    half = HEAD_DIM // 2
    inv = ROPE_THETA ** (-jnp.arange(half, dtype=F32) / half)
    ang = pos.astype(F32)[:, None] * inv[None, :]
    cos, sin = jnp.cos(ang), jnp.sin(ang)
    cos = jnp.concatenate([cos, cos, cos, cos], axis=1)
    sin = jnp.concatenate([-sin, sin, -sin, sin], axis=1)
    return cos, sin


def _p16_positions(seq):
    t = np.arange(seq).reshape(seq // ROW_TILE, P16, P16)
    return jnp.asarray(t.transpose(0, 2, 1).reshape(seq))


def _qkv_prompt(x2d, batch, seq, common):
    n_tb = seq // ROW_TILE
    n_a = seq // P16
    cos, sin = _rope_tables(_p16_positions(seq))
    tail_a = tuple(w // P16 for w in WINDOWS)
    qkv_shape = jax.ShapeDtypeStruct((batch, P16, n_a, 3 * ATTN_W), BF16)
    out_shape = (qkv_shape,) * 3 + tuple(
        jax.ShapeDtypeStruct((batch, P16, ta, 2 * ATTN_W), F32) for ta in tail_a)
    qkv_spec = pl.BlockSpec((None, P16, P16, 3 * ATTN_W), lambda b, t: (b, 0, t, 0))

    def tail_spec(ta):
        blk = min(ta, P16)
        first = n_tb - ta // blk
        return pl.BlockSpec((None, P16, blk, 2 * ATTN_W), lambda b, t: (b, 0, jnp.maximum(t - first, 0), 0))

    return pl.pallas_call(
        _qkv_prompt_kernel,
        grid=(batch, n_tb),
        in_specs=[pl.BlockSpec((ROW_TILE, D_MODEL), lambda b, t: (b * n_tb + t, 0)),
                  _const_spec((1, D_MODEL)), _const_spec((D_MODEL, 9 * ATTN_W)),
                  _const_spec((ROW_TILE, ROW_TILE)), _const_spec((256, 256)),
                  _const_spec((1, ATTN_W)), _const_spec((1, ATTN_W)),
                  pl.BlockSpec((ROW_TILE, 128), lambda b, t: (t, 0)),
                  pl.BlockSpec((ROW_TILE, 128), lambda b, t: (t, 0))],
        out_specs=(qkv_spec,) * 3 + tuple(tail_spec(ta) for ta in tail_a),
        out_shape=out_shape,
        compiler_params=_cparams("arbitrary", "arbitrary"),
        name="qkv_prompt",
    )(x2d, *common, cos, sin)


def _qkv_sample(x2d, pos, common):
    rows = x2d.shape[0]
    cos, sin = _rope_tables(pos)
    spec = pl.BlockSpec((ROW_TILE, 3 * ATTN_W), lambda i: (i, 0))
    return pl.pallas_call(
        _qkv_sample_kernel,
        grid=(rows // ROW_TILE,),
        in_specs=[pl.BlockSpec((ROW_TILE, D_MODEL), lambda i: (i, 0)),
                  _const_spec((1, D_MODEL)), _const_spec((D_MODEL, 9 * ATTN_W)),
                  _const_spec((ROW_TILE, ROW_TILE)), _const_spec((256, 256)),
                  _const_spec((1, ATTN_W)), _const_spec((1, ATTN_W)),
                  pl.BlockSpec((ROW_TILE, 128), lambda i: (i, 0)),
                  pl.BlockSpec((ROW_TILE, 128), lambda i: (i, 0))],
        out_specs=(spec,) * 3,
        out_shape=(jax.ShapeDtypeStruct((rows, 3 * ATTN_W), F32),) * 3,
        compiler_params=_cparams("arbitrary"),
        name="qkv_sample",
    )(x2d, *common, cos, sin)


def _attn_geometry(dil, n_a):
    nres = P16 // dil
    qa = ROW_TILE // nres
    halo = 0 if qa >= n_a else max(P16, -(-KEYS_BACK // nres))
    klen = min(qa + halo, n_a)
    return nres, qa, halo, klen


def _attn_bias_tables(dil, n_a):
    nres, qa, halo, klen = _attn_geometry(dil, n_a)
    m = np.arange(nres * qa)
    n = np.arange(nres * klen)
    qoff = nres * (m % qa) + m // qa
    koff = nres * (n % klen) + n // klen
    tables = []
    for delta in (0, halo):
        dist = qoff[:, None] - koff[None, :] + nres * delta
        tables.append(np.where((dist >= 0) & (dist <= KEYS_BACK), 0.0, NEG_BIG).astype(np.float32))
    return jnp.asarray(np.stack(tables))


def _attn_prompt_kernel(bias_ref, q_ref, k_ref, v_ref, o_ref, l_ref, *, dil, n_a):
    nres, qa, halo, klen = _attn_geometry(dil, n_a)
    n_atiles = n_a // qa
    lane = lax.broadcasted_iota(jnp.int32, (nres * qa, 128), 1)
    head1 = lane >= HEAD_DIM

    def tile(t, carry):
        c = t // n_atiles
        at = t % n_atiles
        a0 = pl.multiple_of(at * qa, P16)
        k0 = pl.multiple_of(jnp.maximum(at * qa - halo, 0), P16)
        rows = [c + dil * b for b in range(nres)]
        q = jnp.concatenate([q_ref[r, pl.ds(a0, qa), :] for r in rows], axis=0)
        k = jnp.concatenate([k_ref[r, pl.ds(k0, klen), :] for r in rows], axis=0)
        v = jnp.concatenate([v_ref[r, pl.ds(k0, klen), :] for r in rows], axis=0)
        bias = bias_ref[jnp.minimum(at, 1)]
        q = q * jnp.asarray(HEAD_DIM ** -0.5, BF16)
        outs, lses = [], []
        for hh in range(2):
            qm = jnp.where(head1 == bool(hh), q, jnp.zeros_like(q))
            s = lax.dot_general(qm, k, (((1,), (1,)), ((), ())), preferred_element_type=F32) + bias
            mx = jnp.max(s, axis=-1, keepdims=True)
            p = jnp.exp(s - mx)
            den = jnp.sum(p, axis=-1, keepdims=True)
            o = jnp.dot(p.astype(BF16), v, preferred_element_type=F32)
            outs.append(o / den)
            lses.append(mx + jnp.log(den))
        o = jnp.where(head1, outs[1], outs[0]).astype(BF16)
        lse = jnp.where(head1, lses[1], lses[0])
        for b, r in enumerate(rows):
            o_ref[r, pl.ds(a0, qa), :] = o[b * qa:(b + 1) * qa]
            l_ref[r, pl.ds(a0, qa), :] = lse[b * qa:(b + 1) * qa]
        return carry

    lax.fori_loop(0, dil * n_atiles, tile, 0)


def _attn_prompt(q, k, v, g):
    batch, _, n_a, _ = q.shape
    dil = DILATIONS[g]
    bias = _attn_bias_tables(dil, n_a)
    in_spec = pl.BlockSpec((None, P16, n_a, 128), lambda b, h: (b, 0, 0, g * 4 + h))
    out_spec = pl.BlockSpec((None, P16, n_a, 128), lambda b, h: (b, 0, 0, h))
    return pl.pallas_call(
        functools.partial(_attn_prompt_kernel, dil=dil, n_a=n_a),
        grid=(batch, 4),
        in_specs=[_const_spec(bias.shape), in_spec, in_spec, in_spec],
        out_specs=(out_spec, out_spec),
        out_shape=(jax.ShapeDtypeStruct((batch, P16, n_a, ATTN_W), BF16),
                   jax.ShapeDtypeStruct((batch, P16, n_a, ATTN_W), F32)),
        compiler_params=_cparams("arbitrary", "arbitrary"),
        name=f"attn_prompt_d{dil}",
    )(bias, q, k, v)


def _attn_sample_kernel(q_ref, kn_ref, vn_ref, c_ref, o_ref, l_ref, nc_ref, *, dil, win):
    n_new = q_ref.shape[0]
    rows = n_new * N_HEADS
    row = lax.broadcasted_iota(jnp.int32, (rows, ATTN_W), 0)
    lane = lax.broadcasted_iota(jnp.int32, (rows, ATTN_W), 1)
    own_head = (lane // HEAD_DIM) == (row & (N_HEADS - 1))
    q_rep = jnp.broadcast_to(q_ref[...][:, None, :], (n_new, N_HEADS, ATTN_W)).reshape(rows, ATTN_W)
    q_exp = jnp.where(own_head, q_rep * (HEAD_DIM ** -0.5), 0.0)
    kc = c_ref[:, :ATTN_W].astype(BF16)
    vc = c_ref[:, ATTN_W:].astype(BF16)
    kn, vn = kn_ref[...], vn_ref[...]
    nt = (((1,), (1,)), ((), ()))
    s_c = lax.dot_general(q_exp.astype(BF16), kc, nt, preferred_element_type=F32)
    s_n = lax.dot_general(q_exp, kn, nt, preferred_element_type=F32)
    tok_c = lax.broadcasted_iota(jnp.int32, (rows, win), 0) // N_HEADS
    dist_c = win + tok_c - lax.broadcasted_iota(jnp.int32, (rows, win), 1)
    ok_c = (dist_c <= win) & ((dist_c & (dil - 1)) == 0)
    tok_n = lax.broadcasted_iota(jnp.int32, (rows, n_new), 0) // N_HEADS
    dist_n = tok_n - lax.broadcasted_iota(jnp.int32, (rows, n_new), 1)
    ok_n = (dist_n >= 0) & ((dist_n & (dil - 1)) == 0)
    s_c = jnp.where(ok_c, s_c, NEG_BIG)
    s_n = jnp.where(ok_n, s_n, NEG_BIG)
    mx = jnp.maximum(jnp.max(s_c, axis=-1, keepdims=True), jnp.max(s_n, axis=-1, keepdims=True))
    p_c = jnp.exp(s_c - mx)
    p_n = jnp.exp(s_n - mx)
    den = jnp.sum(p_c, axis=-1, keepdims=True) + jnp.sum(p_n, axis=-1, keepdims=True)
    o = (jnp.dot(p_c.astype(BF16), vc, preferred_element_type=F32)
         + jnp.dot(p_n, vn, preferred_element_type=F32)) / den
    lse = mx + jnp.log(den)
    o_ref[...] = jnp.sum(jnp.where(own_head, o, 0.0).reshape(n_new, N_HEADS, ATTN_W), axis=1)
    l_ref[...] = jnp.sum(jnp.where(own_head, lse, 0.0).reshape(n_new, N_HEADS, ATTN_W), axis=1)
    nc_ref[:win - n_new, :] = c_ref[n_new:, :]
    nc_ref[win - n_new:, :ATTN_W] = kn
    nc_ref[win - n_new:, ATTN_W:] = vn


def _attn_sample(q, kn, vn, cache, g, n_new):
    batch, win, _ = cache.shape
    dil = DILATIONS[g]
    tok_spec = pl.BlockSpec((n_new, ATTN_W), lambda b: (b, g))
    out_spec = pl.BlockSpec((n_new, ATTN_W), lambda b: (b, 0))
    c_spec = pl.BlockSpec((None, win, 2 * ATTN_W), lambda b: (b, 0, 0))
    return pl.pallas_call(
        functools.partial(_attn_sample_kernel, dil=dil, win=win),
        grid=(batch,),
        in_specs=[tok_spec, tok_spec, tok_spec, c_spec],
        out_specs=(out_spec, out_spec, c_spec),
        out_shape=(jax.ShapeDtypeStruct((batch * n_new, ATTN_W), F32),
                   jax.ShapeDtypeStruct((batch * n_new, ATTN_W), F32),
                   jax.ShapeDtypeStruct(cache.shape, F32)),
        compiler_params=_cparams("arbitrary"),
        name=f"attn_sample_d{dil}",
    )(q, kn, vn, cache)


def _merge_wo_kernel(o0, o1, o2, l0, l1, l2, x_ref, w_ref, perm_ref, y_ref, *, permute):
    os_ = [r[...].astype(F32).reshape(ROW_TILE, ATTN_W) for r in (o0, o1, o2)]
    ls = [r[...].reshape(ROW_TILE, ATTN_W) for r in (l0, l1, l2)]
    mx = jnp.maximum(jnp.maximum(ls[0], ls[1]), ls[2])
    es = [jnp.exp(l - mx) for l in ls]
    den = es[0] + es[1] + es[2]
    o = ((es[0] * os_[0] + es[1] * os_[1] + es[2] * os_[2]) / den).astype(BF16)
    if permute:
        o = jnp.dot(perm_ref[...], o, preferred_element_type=F32).astype(BF16)
    y_ref[...] = x_ref[...] + jnp.dot(o, w_ref[...], preferred_element_type=F32)


def _merge_wo(os_, ls, x2d, w_o_bf, permute, batch=None, seq=None):
    rows = x2d.shape[0]
    perm = np.zeros((ROW_TILE, ROW_TILE), np.float32)
    m = np.arange(ROW_TILE)
    perm[P16 * (m % P16) + m // P16, m] = 1.0
    perm = jnp.asarray(perm, BF16)
    x_spec = pl.BlockSpec((ROW_TILE, D_MODEL), lambda i: (i, 0))
    if permute:
        n_tb = seq // ROW_TILE
        a_spec = pl.BlockSpec((None, P16, P16, ATTN_W), lambda i: (i // n_tb, 0, i % n_tb, 0))
    else:
        a_spec = pl.BlockSpec((ROW_TILE, ATTN_W), lambda i: (i, 0))
    return pl.pallas_call(
        functools.partial(_merge_wo_kernel, permute=permute),
        grid=(rows // ROW_TILE,),
        in_specs=[a_spec] * 6 + [x_spec, _const_spec((ATTN_W, D_MODEL)), _const_spec((ROW_TILE, ROW_TILE))],
        out_specs=x_spec,
        out_shape=jax.ShapeDtypeStruct((rows, D_MODEL), F32),
        compiler_params=_cparams("arbitrary"),
        name="merge_wo",
    )(*os_, *ls, x2d, w_o_bf, perm)


def _ffn_kernel(x_ref, g_ref, wg_ref, wu_ref, wd_ref, y_ref, *, chunks):
    x = x_ref[...]
    h = _rms(x, g_ref[...]).astype(BF16)
    acc = x
    for lo, hi in chunks:
        gate = jnp.dot(h, wg_ref[:, lo:hi], preferred_element_type=F32)
        up = jnp.dot(h, wu_ref[:, lo:hi], preferred_element_type=F32)
        act = (jax.nn.silu(gate) * up).astype(BF16)
        acc = acc + jnp.dot(act, wd_ref[lo:hi, :], preferred_element_type=F32)
    y_ref[...] = acc


def _ffn(x2d, norm_g, wg_bf, wu_bf, wd_bf, tm):
    rows = x2d.shape[0]
    d_ff = wg_bf.shape[1]
    step = 1024
    chunks = tuple((lo, min(lo + step, d_ff)) for lo in range(0, d_ff, step))
    x_spec = pl.BlockSpec((tm, D_MODEL), lambda i: (i, 0))
    w_in = pl.BlockSpec((D_MODEL, d_ff), lambda i: (0, 0), pipeline_mode=pl.Buffered(1))
    w_out = pl.BlockSpec((d_ff, D_MODEL), lambda i: (0, 0), pipeline_mode=pl.Buffered(1))
    return pl.pallas_call(
        functools.partial(_ffn_kernel, chunks=chunks),
        grid=(rows // tm,),
        in_specs=[x_spec, _const_spec((1, D_MODEL)), w_in, w_in, w_out],
        out_specs=x_spec,
        out_shape=jax.ShapeDtypeStruct((rows, D_MODEL), F32),
        compiler_params=_cparams("arbitrary"),
        name="ffn",
    )(x2d, norm_g.reshape(1, D_MODEL), wg_bf, wu_bf, wd_bf)


def _norm_matmul_kernel(x_ref, g_ref, w_ref, y_ref):
    h = _rms(x_ref[...], g_ref[...]).astype(BF16)
    y_ref[...] = jnp.dot(h, w_ref[...], preferred_element_type=F32)


def _norm_matmul(x2d, norm_g, w_bf, tm):
    rows = x2d.shape[0]
    n = w_bf.shape[1]
    return pl.pallas_call(
        _norm_matmul_kernel,
        grid=(rows // tm,),
        in_specs=[pl.BlockSpec((tm, D_MODEL), lambda i: (i, 0)), _const_spec((1, D_MODEL)),
                  _const_spec((D_MODEL, n))],
        out_specs=pl.BlockSpec((tm, n), lambda i: (i, 0)),
        out_shape=jax.ShapeDtypeStruct((rows, n), F32),
        compiler_params=_cparams("arbitrary"),
        name="ssm_in_proj",
    )(x2d, norm_g.reshape(1, D_MODEL), w_bf)


def _ssm_core_kernel(u_ref, mt_ref, ft_ref, et_ref, a_ref, x0_ref, y_ref, xf_ref, *, n_chunks, has_init):
    u = u_ref[...]
    y = jnp.dot(mt_ref[...], u, preferred_element_type=F32)
    s = jnp.dot(ft_ref[...], u, preferred_element_type=F32)
    sre, sim = s[:SSM_STATE], s[SSM_STATE:]
    n_lanes = u.shape[1]
    are, aim = a_ref[0], a_ref[1]
    if n_chunks == 1:
        x0 = x0_ref[...]
        xre, xim = x0[:SSM_STATE], x0[SSM_STATE:]
        xf_ref[:SSM_STATE, :] = are * xre - aim * xim + sre
        xf_ref[SSM_STATE:, :] = are * xim + aim * xre + sim
        y = y + jnp.dot(et_ref[...], x0.astype(BF16), preferred_element_type=F32)
    else:
        assert not has_init and n_chunks % 128 == 0
        reps = n_lanes // 128
        pos = lax.broadcasted_iota(jnp.int32, (SSM_STATE, n_lanes), 1) & (n_chunks - 1)
        shift = 1
        while shift < n_chunks:
            keep = pos >= shift
            tre = jnp.where(keep, pltpu.roll(sre, shift, 1), 0.0)
            tim = jnp.where(keep, pltpu.roll(sim, shift, 1), 0.0)
            bre = jnp.concatenate([are] * reps, axis=1)
            bim = jnp.concatenate([aim] * reps, axis=1)
            sre, sim = sre + bre * tre - bim * tim, sim + bre * tim + bim * tre
            are, aim = are * are - aim * aim, 2.0 * are * aim
            shift *= 2
        for b in range(n_lanes // n_chunks):
            lo, hi = (b + 1) * n_chunks - 128, (b + 1) * n_chunks
            xf_ref[:SSM_STATE, b * 128:(b + 1) * 128] = sre[:, lo:hi]
            xf_ref[SSM_STATE:, b * 128:(b + 1) * 128] = sim[:, lo:hi]
        keep = pos >= 1
        xin = jnp.concatenate([jnp.where(keep, pltpu.roll(sre, 1, 1), 0.0),
                               jnp.where(keep, pltpu.roll(sim, 1, 1), 0.0)], axis=0)
        y = y + jnp.dot(et_ref[...], xin.astype(BF16), preferred_element_type=F32)
    y_ref[...] = y


def _ssm_core(ut, mats, x0t, n_chunks):
    mt, ft, et, a_pow = mats
    groups, rows, n_lanes = ut.shape
    has_init = x0t is not None
    a_lanes = n_lanes if n_chunks == 1 else 128
    a_b = jnp.broadcast_to(a_pow[:, :, :, None], a_pow.shape + (a_lanes,))
    if x0t is None:
        x0t = jnp.zeros((groups, 2 * SSM_STATE, 128), F32)
    xf_lanes = n_lanes if n_chunks == 1 else (n_lanes // n_chunks) * 128

    def gspec(shape):
        return pl.BlockSpec((None,) + tuple(shape[1:]), lambda g: (g,) + (0,) * (len(shape) - 1))

    return pl.pallas_call(
        functools.partial(_ssm_core_kernel, n_chunks=n_chunks, has_init=has_init),
        grid=(groups,),
        in_specs=[gspec(ut.shape), gspec(mt.shape), gspec(ft.shape), gspec(et.shape), gspec(a_b.shape),
                  gspec(x0t.shape)],
        out_specs=(gspec((groups, rows, n_lanes)), gspec((groups, 2 * SSM_STATE, xf_lanes))),
        out_shape=(jax.ShapeDtypeStruct((groups, rows, n_lanes), F32),
                   jax.ShapeDtypeStruct((groups, 2 * SSM_STATE, xf_lanes), F32)),
        compiler_params=_cparams("arbitrary"),
        name=f"ssm_core_{n_chunks}",
    )(ut, mt, ft, et, a_b, x0t)


def _ssm_matrices(a_re, a_im, log_dt, b_re, b_im, c_re, c_im, chunk):
    hp = lax.Precision.HIGHEST
    lam = lax.complex(a_re, a_im)
    dt = jnp.exp(log_dt)[:, None]
    abar = jnp.exp(lam * dt)
    bbar = ((abar - 1.0) / lam)[..., None] * lax.complex(b_re, b_im)
    steps = jnp.arange(chunk + 1, dtype=F32)[:, None, None]
    pw = jnp.exp(steps * (lam * dt)[None])
    w = bbar[None] * pw[..., None]
    kl = (jnp.einsum("ngpd,gcp->gcdn", w.real, c_re, precision=hp)
          - jnp.einsum("ngpd,gcp->gcdn", w.imag, c_im, precision=hp))
    i = np.arange(chunk)
    lag = i[:, None] - i[None, :]
    mt = jnp.where(jnp.asarray(lag >= 0), kl[:, :, :, np.maximum(lag, 0)], 0.0)
    mt = mt.transpose(0, 1, 3, 2, 4).reshape(SSM_GROUPS, SSM_GROUP * chunk, SSM_GROUP * chunk)
    wf = w[chunk - 1 - i]
    ft = jnp.stack([wf.real, wf.imag], axis=0).transpose(2, 0, 3, 4, 1)
    ft = ft.reshape(SSM_GROUPS, 2 * SSM_STATE, SSM_GROUP * chunk)
    ve = lax.complex(c_re, c_im)[None] * pw[1:, :, None, :]
    et = jnp.stack([ve.real, -ve.imag], axis=0).transpose(2, 3, 1, 0, 4)
    et = et.reshape(SSM_GROUPS, SSM_GROUP * chunk, 2 * SSM_STATE)
    a_pow = jnp.stack([pw[chunk].real, pw[chunk].imag], axis=1)
    return mt.astype(BF16), ft.astype(BF16), et.astype(BF16), a_pow


def _glu_kernel(y_ref, u_ref, d_ref, x_ref, w_ref, o_ref):
    yy = y_ref[...] + d_ref[...] * u_ref[...]
    z = jnp.dot(jax.nn.gelu(yy).astype(BF16), w_ref[...], preferred_element_type=F32)
    o_ref[...] = x_ref[...] + z[:, :D_MODEL] * jax.nn.sigmoid(z[:, D_MODEL:])


def _glu(y2d, u2d, d_skip, x2d, w_bf, tm):
    rows = x2d.shape[0]
    spec = pl.BlockSpec((tm, D_MODEL), lambda i: (i, 0))
    return pl.pallas_call(
        _glu_kernel,
        grid=(rows // tm,),
        in_specs=[spec, spec, _const_spec((1, D_MODEL)), spec, _const_spec((D_MODEL, 2 * D_MODEL))],
        out_specs=spec,
        out_shape=jax.ShapeDtypeStruct((rows, D_MODEL), F32),
        compiler_params=_cparams("arbitrary"),
        name="ssm_glu",
    )(y2d, u2d, d_skip.reshape(1, D_MODEL), x2d, w_bf)


def _ssm_mixer(x2d, batch, seq, chunk, norm_g, w_in_bf, mats, d_skip, w_glu_bf, state0, tm):
    n_chunks = seq // chunk
    u = _norm_matmul(x2d, norm_g, w_in_bf, tm)
    ut = u.reshape(batch, n_chunks, chunk, SSM_GROUPS, SSM_GROUP).transpose(3, 4, 2, 0, 1)
    ut = ut.reshape(SSM_GROUPS, SSM_GROUP * chunk, batch * n_chunks).astype(BF16)
    x0t = None
    if state0 is not None:
        x0t = state0.transpose(1, 3, 2, 0).reshape(SSM_GROUPS, 2 * SSM_STATE, batch)
    yt, xf = _ssm_core(ut, mats, x0t, n_chunks)
    y = yt.reshape(SSM_GROUPS, SSM_GROUP, chunk, batch, n_chunks).transpose(3, 4, 2, 0, 1)
    y = y.reshape(batch * seq, D_MODEL)
    if n_chunks > 1:
        xf = xf.reshape(SSM_GROUPS, 2, SSM_STATE, batch, 128)[..., 127]
    else:
        xf = xf.reshape(SSM_GROUPS, 2, SSM_STATE, batch)
    state = xf.transpose(3, 0, 2, 1)
    return _glu(y, u, d_skip, x2d, w_glu_bf, tm), state


def kernel(x_prompt, x_sample, cache_kv_w128, cache_kv_w512, cache_kv_w2048, state_ssm, norm_mix, norm_ffn,
           w_qkv, q_norm, k_norm, w_o, ssm_w_in, ssm_a_re, ssm_a_im, ssm_log_dt, ssm_b_re, ssm_b_im,
           ssm_c_re, ssm_c_im, ssm_d, ssm_w_glu, ffn_w_gate, ffn_w_up, ffn_w_down):
    batch, seq, _ = x_prompt.shape
    dec_batch, dec_seq, _ = x_sample.shape
    caches = (cache_kv_w128, cache_kv_w512, cache_kv_w2048)
    past_len = 16384
    assert seq % ROW_TILE == 0 and seq >= WINDOWS[-1] and (dec_batch * dec_seq) % ROW_TILE == 0
    assert all(c.shape[2] == w for c, w in zip(caches, WINDOWS))
    n_a = seq // P16
    tm_p = 512

    xp = x_prompt.reshape(batch * seq, D_MODEL)
    xs = x_sample.reshape(dec_batch * dec_seq, D_MODEL)

    common = _qkv_common_inputs(norm_mix[0], w_qkv[0].astype(BF16), q_norm[0], k_norm[0])
    w_o_bf = w_o[0].astype(BF16)
    qp, kp, vp, t0, t1, t2 = _qkv_prompt(xp, batch, seq, common)
    outs = [_attn_prompt(qp, kp, vp, g) for g in range(3)]
    xp = _merge_wo([o for o, _ in outs], [l for _, l in outs], xp, w_o_bf, True, batch, seq)
    kv_prompt = []
    for tail, win in zip((t0, t1, t2), WINDOWS):
        kv = tail.transpose(0, 2, 1, 3).reshape(batch, win, 2, N_HEADS, HEAD_DIM)
        kv_prompt.append(kv[None])

    pos_s = jnp.tile(past_len + jnp.arange(dec_seq), dec_batch)
    qs, ks, vs = _qkv_sample(xs, pos_s, common)
    s_outs = [_attn_sample(qs, ks, vs, caches[g][0].reshape(dec_batch, WINDOWS[g], 2 * ATTN_W), g, dec_seq)
              for g in range(3)]
    xs = _merge_wo([o for o, _, _ in s_outs], [l for _, l, _ in s_outs], xs, w_o_bf, False)
    kv_sample = [nc.reshape(1, dec_batch, win, 2, N_HEADS, HEAD_DIM)
                 for (_, _, nc), win in zip(s_outs, WINDOWS)]

    ffn_w = [(ffn_w_gate[i].astype(BF16), ffn_w_up[i].astype(BF16), ffn_w_down[i].astype(BF16))
             for i in range(2)]
    xp = _ffn(xp, norm_ffn[0], *ffn_w[0], tm_p)
    xs = _ffn(xs, norm_ffn[0], *ffn_w[0], ROW_TILE)

    ssm_p = (ssm_a_re[0], ssm_a_im[0], ssm_log_dt[0], ssm_b_re[0], ssm_b_im[0], ssm_c_re[0], ssm_c_im[0])
    w_in_bf = ssm_w_in[0].astype(BF16)
    w_glu_bf = ssm_w_glu[0].astype(BF16)
    xp, st_p = _ssm_mixer(xp, batch, seq, SSM_CHUNK, norm_mix[1], w_in_bf,
                          _ssm_matrices(*ssm_p, SSM_CHUNK), ssm_d[0], w_glu_bf, None, tm_p)
    xs, st_s = _ssm_mixer(xs, dec_batch, dec_seq, dec_seq, norm_mix[1], w_in_bf,
                          _ssm_matrices(*ssm_p, dec_seq), ssm_d[0], w_glu_bf, state_ssm[0], ROW_TILE)
    xp = _ffn(xp, norm_ffn[1], *ffn_w[1], tm_p)
    xs = _ffn(xs, norm_ffn[1], *ffn_w[1], ROW_TILE)

    return (xp.reshape(batch, seq, D_MODEL), xs.reshape(dec_batch, dec_seq, D_MODEL),
            kv_prompt[0], kv_prompt[1], kv_prompt[2], st_p[None],
            kv_sample[0], kv_sample[1], kv_sample[2], st_s[None])
```

```python
import functools
import math

import numpy as np
import jax
import jax.numpy as jnp
from jax import lax
from jax.experimental import pallas as pl
from jax.experimental.pallas import tpu as pltpu

F32 = jnp.float32
BF16 = jnp.bfloat16

D_MODEL = 1024
HEAD_DIM = 64
N_HEADS = 8
ATTN_W = N_HEADS * HEAD_DIM
DILATIONS = (1, 4, 16)
WINDOWS = (128, 512, 2048)
KEYS_BACK = 128
PAST_LEN = 16384
ROPE_THETA = 10000.0
RMS_EPS = 1e-6
SSM_GROUP = 16
SSM_GROUPS = D_MODEL // SSM_GROUP
SSM_STATE = 64
SSM_CHUNK = 16
ROW_TILE = 256
P16 = 16
NEG_BIG = -1e30
VMEM_LIMIT = 56 * 1024 * 1024


def _cparams(*sem):
    return pltpu.CompilerParams(dimension_semantics=tuple(sem), vmem_limit_bytes=VMEM_LIMIT)


def _const_spec(shape):
    nd = len(shape)
    return pl.BlockSpec(shape, lambda *_: (0,) * nd)


def _perm_matrices():
    m = np.arange(ROW_TILE)
    to_p16 = np.zeros((ROW_TILE, ROW_TILE), np.float32)
    to_p16[m, P16 * (m % P16) + m // P16] = 1.0
    return jnp.asarray(to_p16, BF16), jnp.asarray(to_p16.T, BF16)


def _rms(x, gain):
    ms = jnp.mean(x * x, axis=-1, keepdims=True)
    return x * lax.rsqrt(ms + RMS_EPS) * gain


def _qkv_slabs(x_ref, g_ref, w_ref, perm_ref, ones_ref, qg_ref, kg_ref, cos_ref, sin_ref, permute):
    h = _rms(x_ref[...], g_ref[...]).astype(BF16)
    if permute:
        h = jnp.dot(perm_ref[...], h, preferred_element_type=F32).astype(BF16)
    cos = jnp.concatenate([cos_ref[...]] * 4, axis=1)
    sin = jnp.concatenate([sin_ref[...]] * 4, axis=1)
    lane = lax.broadcasted_iota(jnp.int32, (ROW_TILE, ATTN_W), 1)
    first_half = (lane & (HEAD_DIM - 1)) < (HEAD_DIM // 2)
    ones = ones_ref[...]
    out = {}
    for s in range(3):
        for g in range(3):
            col = (s * 3 + g) * ATTN_W
            y = jnp.dot(h, w_ref[:, col:col + ATTN_W], preferred_element_type=F32)
            if s < 2:
                gain = (qg_ref if s == 0 else kg_ref)[...]
                yy = (y * y).astype(BF16)
                ss = jnp.concatenate(
                    [jnp.dot(yy[:, :256], ones, preferred_element_type=F32),
                     jnp.dot(yy[:, 256:], ones, preferred_element_type=F32)], axis=1)
                yn = y * lax.rsqrt(ss * (1.0 / HEAD_DIM) + RMS_EPS) * gain
                swapped = jnp.where(first_half,
                                    pltpu.roll(yn, ATTN_W - HEAD_DIM // 2, 1),
                                    pltpu.roll(yn, HEAD_DIM // 2, 1))
                y = yn * cos + swapped * sin
            out[(s, g)] = y
    return out


def _qkv_prompt_kernel(x_ref, g_ref, w_ref, perm_ref, ones_ref, qg_ref, kg_ref, cos_ref, sin_ref,
                       q_ref, k_ref, v_ref, t0_ref, t1_ref, t2_ref):
    slabs = _qkv_slabs(x_ref, g_ref, w_ref, perm_ref, ones_ref, qg_ref, kg_ref, cos_ref, sin_ref, True)
    tails = (t0_ref, t1_ref, t2_ref)
    for g in range(3):
        lo, hi = g * ATTN_W, (g + 1) * ATTN_W
        for s, ref in enumerate((q_ref, k_ref, v_ref)):
            ref[:, :, lo:hi] = slabs[(s, g)].reshape(P16, P16, ATTN_W).astype(BF16)
        k3 = slabs[(1, g)].reshape(P16, P16, ATTN_W)
        v3 = slabs[(2, g)].reshape(P16, P16, ATTN_W)
        if g == 0:
            k3, v3 = k3[:, 8:, :], v3[:, 8:, :]
        tails[g][:, :, :ATTN_W] = k3
        tails[g][:, :, ATTN_W:] = v3


def _qkv_sample_kernel(x_ref, g_ref, w_ref, perm_ref, ones_ref, qg_ref, kg_ref, cos_ref, sin_ref,
                       q_ref, k_ref, v_ref):
    slabs = _qkv_slabs(x_ref, g_ref, w_ref, perm_ref, ones_ref, qg_ref, kg_ref, cos_ref, sin_ref, False)
    for g in range(3):
        lo, hi = g * ATTN_W, (g + 1) * ATTN_W
        for s, ref in enumerate((q_ref, k_ref, v_ref)):
            ref[:, lo:hi] = slabs[(s, g)]


def _qkv_common_inputs(norm_g, w_qkv_bf, q_gain, k_gain):
    ones = np.kron(np.eye(4, dtype=np.float32), np.ones((HEAD_DIM, HEAD_DIM), np.float32))
    return (norm_g.reshape(1, D_MODEL), w_qkv_bf, _perm_matrices()[0], jnp.asarray(ones, BF16),
            jnp.tile(q_gain, N_HEADS).reshape(1, ATTN_W), jnp.tile(k_gain, N_HEADS).reshape(1, ATTN_W))


def _rope_tables(pos):
    half = HEAD_DIM // 2
    inv = ROPE_THETA ** (-jnp.arange(half, dtype=F32) / half)
    ang = pos.astype(F32)[:, None] * inv[None, :]
    cos, sin = jnp.cos(ang), jnp.sin(ang)
    cos = jnp.concatenate([cos, cos, cos, cos], axis=1)
    sin = jnp.concatenate([-sin, sin, -sin, sin], axis=1)
    return cos, sin


def _p16_positions(seq):
    t = np.arange(seq).reshape(seq // ROW_TILE, P16, P16)
    return jnp.asarray(t.transpose(0, 2, 1).reshape(seq))


def _qkv_prompt(x2d, batch, seq, common):
    n_tb = seq // ROW_TILE
    n_a = seq // P16
    cos, sin = _rope_tables(_p16_positions(seq))
    tail_a = tuple(w // P16 for w in WINDOWS)
    qkv_shape = jax.ShapeDtypeStruct((batch, P16, n_a, 3 * ATTN_W), BF16)
    out_shape = (qkv_shape,) * 3 + tuple(
        jax.ShapeDtypeStruct((batch, P16, ta, 2 * ATTN_W), F32) for ta in tail_a)
    qkv_spec = pl.BlockSpec((None, P16, P16, 3 * ATTN_W), lambda b, t: (b, 0, t, 0))

    def tail_spec(ta):
        blk = min(ta, P16)
        first = n_tb - ta // blk
        return pl.BlockSpec((None, P16, blk, 2 * ATTN_W), lambda b, t: (b, 0, jnp.maximum(t - first, 0), 0))

    return pl.pallas_call(
        _qkv_prompt_kernel,
        grid=(batch, n_tb),
        in_specs=[pl.BlockSpec((ROW_TILE, D_MODEL), lambda b, t: (b * n_tb + t, 0)),
                  _const_spec((1, D_MODEL)), _const_spec((D_MODEL, 9 * ATTN_W)),
                  _const_spec((ROW_TILE, ROW_TILE)), _const_spec((256, 256)),
                  _const_spec((1, ATTN_W)), _const_spec((1, ATTN_W)),
                  pl.BlockSpec((ROW_TILE, 128), lambda b, t: (t, 0)),
                  pl.BlockSpec((ROW_TILE, 128), lambda b, t: (t, 0))],
        out_specs=(qkv_spec,) * 3 + tuple(tail_spec(ta) for ta in tail_a),
        out_shape=out_shape,
        compiler_params=_cparams("arbitrary", "arbitrary"),
        name="qkv_prompt",
    )(x2d, *common, cos, sin)


def _qkv_sample(x2d, pos, common):
    rows = x2d.shape[0]
    cos, sin = _rope_tables(pos)
    spec = pl.BlockSpec((ROW_TILE, 3 * ATTN_W), lambda i: (i, 0))
    return pl.pallas_call(
        _qkv_sample_kernel,
        grid=(rows // ROW_TILE,),
        in_specs=[pl.BlockSpec((ROW_TILE, D_MODEL), lambda i: (i, 0)),
                  _const_spec((1, D_MODEL)), _const_spec((D_MODEL, 9 * ATTN_W)),
                  _const_spec((ROW_TILE, ROW_TILE)), _const_spec((256, 256)),
                  _const_spec((1, ATTN_W)), _const_spec((1, ATTN_W)),
                  pl.BlockSpec((ROW_TILE, 128), lambda i: (i, 0)),
                  pl.BlockSpec((ROW_TILE, 128), lambda i: (i, 0))],
        out_specs=(spec,) * 3,
        out_shape=(jax.ShapeDtypeStruct((rows, 3 * ATTN_W), F32),) * 3,
        compiler_params=_cparams("arbitrary"),
        name="qkv_sample",
    )(x2d, *common, cos, sin)


def _attn_geometry(dil, n_a):
    nres = P16 // dil
    qa = ROW_TILE // nres
    halo = 0 if qa >= n_a else max(P16, -(-KEYS_BACK // nres))
    klen = min(qa + halo, n_a)
    return nres, qa, halo, klen


def _attn_bias_tables(dil, n_a):
    nres, qa, halo, klen = _attn_geometry(dil, n_a)
    m = np.arange(nres * qa)
    n = np.arange(nres * klen)
    qoff = nres * (m % qa) + m // qa
    koff = nres * (n % klen) + n // klen
    tables = []
    for delta in (0, halo):
        dist = qoff[:, None] - koff[None, :] + nres * delta
        tables.append(np.where((dist >= 0) & (dist <= KEYS_BACK), 0.0, NEG_BIG).astype(np.float32))
    return jnp.asarray(np.stack(tables))


def _attn_prompt_kernel(bias_ref, q_ref, k_ref, v_ref, o_ref, l_ref, *, dil, n_a):
    nres, qa, halo, klen = _attn_geometry(dil, n_a)
    n_atiles = n_a // qa
    lane = lax.broadcasted_iota(jnp.int32, (nres * qa, 128), 1)
    head1 = lane >= HEAD_DIM

    def tile(t, carry):
        c = t // n_atiles
        at = t % n_atiles
        a0 = pl.multiple_of(at * qa, P16)
        k0 = pl.multiple_of(jnp.maximum(at * qa - halo, 0), P16)
        rows = [c + dil * b for b in range(nres)]
        q = jnp.concatenate([q_ref[r, pl.ds(a0, qa), :] for r in rows], axis=0)
        k = jnp.concatenate([k_ref[r, pl.ds(k0, klen), :] for r in rows], axis=0)
        v = jnp.concatenate([v_ref[r, pl.ds(k0, klen), :] for r in rows], axis=0)
        bias = bias_ref[jnp.minimum(at, 1)]
        q = q * jnp.asarray(HEAD_DIM ** -0.5, BF16)
        outs, lses = [], []
        for hh in range(2):
            qm = jnp.where(head1 == bool(hh), q, jnp.zeros_like(q))
            s = lax.dot_general(qm, k, (((1,), (1,)), ((), ())), preferred_element_type=F32) + bias
            mx = jnp.max(s, axis=-1, keepdims=True)
            p = jnp.exp(s - mx)
            den = jnp.sum(p, axis=-1, keepdims=True)
            o = jnp.dot(p.astype(BF16), v, preferred_element_type=F32)
            outs.append(o / den)
            lses.append(mx + jnp.log(den))
        o = jnp.where(head1, outs[1], outs[0]).astype(BF16)
        lse = jnp.where(head1, lses[1], lses[0])
        for b, r in enumerate(rows):
            o_ref[r, pl.ds(a0, qa), :] = o[b * qa:(b + 1) * qa]
            l_ref[r, pl.ds(a0, qa), :] = lse[b * qa:(b + 1) * qa]
        return carry

    lax.fori_loop(0, dil * n_atiles, tile, 0)


def _attn_prompt(q, k, v, g):
    batch, _, n_a, _ = q.shape
    dil = DILATIONS[g]
    bias = _attn_bias_tables(dil, n_a)
    in_spec = pl.BlockSpec((None, P16, n_a, 128), lambda b, h: (b, 0, 0, g * 4 + h))
    out_spec = pl.BlockSpec((None, P16, n_a, 128), lambda b, h: (b, 0, 0, h))
    return pl.pallas_call(
        functools.partial(_attn_prompt_kernel, dil=dil, n_a=n_a),
        grid=(batch, 4),
        in_specs=[_const_spec(bias.shape), in_spec, in_spec, in_spec],
        out_specs=(out_spec, out_spec),
        out_shape=(jax.ShapeDtypeStruct((batch, P16, n_a, ATTN_W), BF16),
                   jax.ShapeDtypeStruct((batch, P16, n_a, ATTN_W), F32)),
        compiler_params=_cparams("arbitrary", "arbitrary"),
        name=f"attn_prompt_d{dil}",
    )(bias, q, k, v)


def _attn_sample_kernel(q_ref, kn_ref, vn_ref, c_ref, o_ref, l_ref, nc_ref, *, dil, win):
    n_new = q_ref.shape[0]
    rows = n_new * N_HEADS
    row = lax.broadcasted_iota(jnp.int32, (rows, ATTN_W), 0)
    lane = lax.broadcasted_iota(jnp.int32, (rows, ATTN_W), 1)
    own_head = (lane // HEAD_DIM) == (row & (N_HEADS - 1))
    q_rep = jnp.broadcast_to(q_ref[...][:, None, :], (n_new, N_HEADS, ATTN_W)).reshape(rows, ATTN_W)
    q_exp = jnp.where(own_head, q_rep * (HEAD_DIM ** -0.5), 0.0)
    kc = c_ref[:, :ATTN_W].astype(BF16)
    vc = c_ref[:, ATTN_W:].astype(BF16)
    kn, vn = kn_ref[...], vn_ref[...]
    nt = (((1,), (1,)), ((), ()))
    s_c = lax.dot_general(q_exp.astype(BF16), kc, nt, preferred_element_type=F32)
    s_n = lax.dot_general(q_exp, kn, nt, preferred_element_type=F32)
    tok_c = lax.broadcasted_iota(jnp.int32, (rows, win), 0) // N_HEADS
    dist_c = win + tok_c - lax.broadcasted_iota(jnp.int32, (rows, win), 1)
    ok_c = (dist_c <= win) & ((dist_c & (dil - 1)) == 0)
    tok_n = lax.broadcasted_iota(jnp.int32, (rows, n_new), 0) // N_HEADS
    dist_n = tok_n - lax.broadcasted_iota(jnp.int32, (rows, n_new), 1)
    ok_n = (dist_n >= 0) & ((dist_n & (dil - 1)) == 0)
    s_c = jnp.where(ok_c, s_c, NEG_BIG)
    s_n = jnp.where(ok_n, s_n, NEG_BIG)
    mx = jnp.maximum(jnp.max(s_c, axis=-1, keepdims=True), jnp.max(s_n, axis=-1, keepdims=True))
    p_c = jnp.exp(s_c - mx)
    p_n = jnp.exp(s_n - mx)
    den = jnp.sum(p_c, axis=-1, keepdims=True) + jnp.sum(p_n, axis=-1, keepdims=True)
    o = (jnp.dot(p_c.astype(BF16), vc, preferred_element_type=F32)
         + jnp.dot(p_n, vn, preferred_element_type=F32)) / den
    lse = mx + jnp.log(den)
    o_ref[...] = jnp.sum(jnp.where(own_head, o, 0.0).reshape(n_new, N_HEADS, ATTN_W), axis=1)
    l_ref[...] = jnp.sum(jnp.where(own_head, lse, 0.0).reshape(n_new, N_HEADS, ATTN_W), axis=1)
    nc_ref[:win - n_new, :] = c_ref[n_new:, :]
    nc_ref[win - n_new:, :ATTN_W] = kn
    nc_ref[win - n_new:, ATTN_W:] = vn


def _attn_sample(q, kn, vn, cache, g, n_new):
    batch, win, _ = cache.shape
    dil = DILATIONS[g]
    tok_spec = pl.BlockSpec((n_new, ATTN_W), lambda b: (b, g))
    out_spec = pl.BlockSpec((n_new, ATTN_W), lambda b: (b, 0))
    c_spec = pl.BlockSpec((None, win, 2 * ATTN_W), lambda b: (b, 0, 0))
    return pl.pallas_call(
        functools.partial(_attn_sample_kernel, dil=dil, win=win),
        grid=(batch,),
        in_specs=[tok_spec, tok_spec, tok_spec, c_spec],
        out_specs=(out_spec, out_spec, c_spec),
        out_shape=(jax.ShapeDtypeStruct((batch * n_new, ATTN_W), F32),
                   jax.ShapeDtypeStruct((batch * n_new, ATTN_W), F32),
                   jax.ShapeDtypeStruct(cache.shape, F32)),
        compiler_params=_cparams("arbitrary"),
        name=f"attn_sample_d{dil}",
    )(q, kn, vn, cache)


def _merge_wo_kernel(o0, o1, o2, l0, l1, l2, x_ref, w_ref, perm_ref, y_ref, *, permute):
    os_ = [r[...].astype(F32).reshape(ROW_TILE, ATTN_W) for r in (o0, o1, o2)]
    ls = [r[...].reshape(ROW_TILE, ATTN_W) for r in (l0, l1, l2)]
    mx = jnp.maximum(jnp.maximum(ls[0], ls[1]), ls[2])
    es = [jnp.exp(l - mx) for l in ls]
    den = es[0] + es[1] + es[2]
    o = ((es[0] * os_[0] + es[1] * os_[1] + es[2] * os_[2]) / den).astype(BF16)
    if permute:
        o = jnp.dot(perm_ref[...], o, preferred_element_type=F32).astype(BF16)
    y_ref[...] = x_ref[...] + jnp.dot(o, w_ref[...], preferred_element_type=F32)


def _merge_wo(os_, ls, x2d, w_o_bf, permute, batch=None, seq=None):
    rows = x2d.shape[0]
    perm = _perm_matrices()[1]
    x_spec = pl.BlockSpec((ROW_TILE, D_MODEL), lambda i: (i, 0))
    if permute:
        n_tb = seq // ROW_TILE
        a_spec = pl.BlockSpec((None, P16, P16, ATTN_W), lambda i: (i // n_tb, 0, i % n_tb, 0))
    else:
        a_spec = pl.BlockSpec((ROW_TILE, ATTN_W), lambda i: (i, 0))
    return pl.pallas_call(
        functools.partial(_merge_wo_kernel, permute=permute),
        grid=(rows // ROW_TILE,),
        in_specs=[a_spec] * 6 + [x_spec, _const_spec((ATTN_W, D_MODEL)), _const_spec((ROW_TILE, ROW_TILE))],
        out_specs=x_spec,
        out_shape=jax.ShapeDtypeStruct((rows, D_MODEL), F32),
        compiler_params=_cparams("arbitrary"),
        name="merge_wo",
    )(*os_, *ls, x2d, w_o_bf, perm)


def _ffn_kernel(x_ref, g_ref, wg_ref, wu_ref, wd_ref, y_ref, *, chunks):
    x = x_ref[...]
    h = _rms(x, g_ref[...]).astype(BF16)
    acc = x
    for lo, hi in chunks:
        gate = jnp.dot(h, wg_ref[:, lo:hi], preferred_element_type=F32)
        up = jnp.dot(h, wu_ref[:, lo:hi], preferred_element_type=F32)
        act = (jax.nn.silu(gate) * up).astype(BF16)
        acc = acc + jnp.dot(act, wd_ref[lo:hi, :], preferred_element_type=F32)
    y_ref[...] = acc


def _ffn(x2d, norm_g, wg_bf, wu_bf, wd_bf, tm):
    rows = x2d.shape[0]
    d_ff = wg_bf.shape[1]
    step = 1024
    chunks = tuple((lo, min(lo + step, d_ff)) for lo in range(0, d_ff, step))
    x_spec = pl.BlockSpec((tm, D_MODEL), lambda i: (i, 0))
    w_in = pl.BlockSpec((D_MODEL, d_ff), lambda i: (0, 0), pipeline_mode=pl.Buffered(1))
    w_out = pl.BlockSpec((d_ff, D_MODEL), lambda i: (0, 0), pipeline_mode=pl.Buffered(1))
    return pl.pallas_call(
        functools.partial(_ffn_kernel, chunks=chunks),
        grid=(rows // tm,),
        in_specs=[x_spec, _const_spec((1, D_MODEL)), w_in, w_in, w_out],
        out_specs=x_spec,
        out_shape=jax.ShapeDtypeStruct((rows, D_MODEL), F32),
        compiler_params=_cparams("arbitrary"),
        name="ffn",
    )(x2d, norm_g.reshape(1, D_MODEL), wg_bf, wu_bf, wd_bf)


def _norm_matmul_kernel(x_ref, g_ref, w_ref, y_ref):
    h = _rms(x_ref[...], g_ref[...]).astype(BF16)
    y_ref[...] = jnp.dot(h, w_ref[...], preferred_element_type=F32)


def _norm_matmul(x2d, norm_g, w_bf, tm):
    rows = x2d.shape[0]
    n = w_bf.shape[1]
    return pl.pallas_call(
        _norm_matmul_kernel,
        grid=(rows // tm,),
        in_specs=[pl.BlockSpec((tm, D_MODEL), lambda i: (i, 0)), _const_spec((1, D_MODEL)),
                  _const_spec((D_MODEL, n))],
        out_specs=pl.BlockSpec((tm, n), lambda i: (i, 0)),
        out_shape=jax.ShapeDtypeStruct((rows, n), F32),
        compiler_params=_cparams("arbitrary"),
        name="ssm_in_proj",
    )(x2d, norm_g.reshape(1, D_MODEL), w_bf)


SSM_LANE_TILE = 128
SSM_TOK_TILE = SSM_LANE_TILE * SSM_CHUNK


def _ssm_in_kernel(x_ref, g_ref, w_ref, perm_ref, ut_ref):
    n_blk = SSM_TOK_TILE // ROW_TILE
    blocks = []
    for kb in range(n_blk):
        h = _rms(x_ref[kb * ROW_TILE:(kb + 1) * ROW_TILE, :], g_ref[...]).astype(BF16)
        blocks.append(jnp.dot(perm_ref[...], h, preferred_element_type=F32).astype(BF16))
    hp = jnp.concatenate([blk[j * P16:(j + 1) * P16] for j in range(SSM_CHUNK) for blk in blocks], axis=0)
    u = jnp.dot(hp, w_ref[...], preferred_element_type=F32)
    for j in range(SSM_CHUNK):
        ut_ref[j] = u[j * SSM_LANE_TILE:(j + 1) * SSM_LANE_TILE].T


def _ssm_in(x2d, norm_g, w_bf, perm):
    rows = x2d.shape[0]
    n_tiles = rows // SSM_TOK_TILE
    return pl.pallas_call(
        _ssm_in_kernel,
        grid=(n_tiles,),
        in_specs=[pl.BlockSpec((SSM_TOK_TILE, D_MODEL), lambda i: (i, 0)), _const_spec((1, D_MODEL)),
                  pl.BlockSpec((D_MODEL, D_MODEL), lambda i: (0, 0), pipeline_mode=pl.Buffered(1)),
                  _const_spec((ROW_TILE, ROW_TILE))],
        out_specs=pl.BlockSpec((SSM_CHUNK, D_MODEL, SSM_LANE_TILE), lambda i: (0, 0, i)),
        out_shape=jax.ShapeDtypeStruct((SSM_CHUNK, D_MODEL, n_tiles * SSM_LANE_TILE), F32),
        compiler_params=_cparams("arbitrary"),
        name="ssm_in_t",
    )(x2d, norm_g.reshape(1, D_MODEL), w_bf, perm)


def _ssm_gate_kernel(yt_ref, ut_ref, d_ref, permt_ref, g_ref):
    n_blk = SSM_TOK_TILE // ROW_TILE
    pieces = []
    for i in range(SSM_CHUNK):
        yy = yt_ref[i] + d_ref[...] * ut_ref[i]
        pieces.append(jax.nn.gelu(yy).T.astype(BF16))
    for kb in range(n_blk):
        blk = jnp.concatenate([p[kb * P16:(kb + 1) * P16] for p in pieces], axis=0)
        g_ref[kb * ROW_TILE:(kb + 1) * ROW_TILE, :] = jnp.dot(
            permt_ref[...], blk, preferred_element_type=F32).astype(BF16)


def _ssm_gate(yt, ut, d_skip, permt):
    n_tiles = yt.shape[2] // SSM_LANE_TILE
    half = D_MODEL // 2
    t_spec = pl.BlockSpec((SSM_CHUNK, half, SSM_LANE_TILE), lambda i, c: (0, c, i))
    d_b = jnp.broadcast_to(d_skip[:, None], (D_MODEL, SSM_LANE_TILE))
    return pl.pallas_call(
        _ssm_gate_kernel,
        grid=(n_tiles, 2),
        in_specs=[t_spec, t_spec, pl.BlockSpec((half, SSM_LANE_TILE), lambda i, c: (c, 0)),
                  _const_spec((ROW_TILE, ROW_TILE))],
        out_specs=pl.BlockSpec((SSM_TOK_TILE, half), lambda i, c: (i, c)),
        out_shape=jax.ShapeDtypeStruct((n_tiles * SSM_TOK_TILE, D_MODEL), BF16),
        compiler_params=_cparams("arbitrary", "arbitrary"),
        name="ssm_gate_t",
    )(yt, ut, d_b, permt)


def _glu_act_kernel(g_ref, x_ref, w_ref, o_ref):
    z = jnp.dot(g_ref[...], w_ref[...], preferred_element_type=F32)
    o_ref[...] = x_ref[...] + z[:, :D_MODEL] * jax.nn.sigmoid(z[:, D_MODEL:])


def _glu_act(g2d, x2d, w_bf, tm):
    rows = x2d.shape[0]
    spec = pl.BlockSpec((tm, D_MODEL), lambda i: (i, 0))
    return pl.pallas_call(
        _glu_act_kernel,
        grid=(rows // tm,),
        in_specs=[spec, spec, _const_spec((D_MODEL, 2 * D_MODEL))],
        out_specs=spec,
        out_shape=jax.ShapeDtypeStruct((rows, D_MODEL), F32),
        compiler_params=_cparams("arbitrary"),
        name="ssm_glu_act",
    )(g2d, x2d, w_bf)


def _ssm_core_kernel(u_ref, mt_ref, ft_ref, et_ref, a_ref, x0_ref, y_ref, xf_ref, *, n_chunks, has_init):
    chunk, _, n_lanes = u_ref.shape
    u = u_ref[...].reshape(chunk * SSM_GROUP, n_lanes).astype(BF16)
    y = jnp.dot(mt_ref[...], u, preferred_element_type=F32)
    s = jnp.dot(ft_ref[...], u, preferred_element_type=F32)
    sre, sim = s[:SSM_STATE], s[SSM_STATE:]
    are, aim = a_ref[0], a_ref[1]
    if n_chunks == 1:
        x0 = x0_ref[...]
        xre, xim = x0[:SSM_STATE], x0[SSM_STATE:]
        xf_ref[:SSM_STATE, :] = are * xre - aim * xim + sre
        xf_ref[SSM_STATE:, :] = are * xim + aim * xre + sim
        y = y + jnp.dot(et_ref[...], x0.astype(BF16), preferred_element_type=F32)
    else:
        assert not has_init and n_chunks % 128 == 0
        reps = n_lanes // 128
        pos = lax.broadcasted_iota(jnp.int32, (SSM_STATE, n_lanes), 1) & (n_chunks - 1)
        shift = 1
        while shift < n_chunks:
            keep = pos >= shift
            tre = jnp.where(keep, pltpu.roll(sre, shift, 1), 0.0)
            tim = jnp.where(keep, pltpu.roll(sim, shift, 1), 0.0)
            bre = jnp.concatenate([are] * reps, axis=1)
            bim = jnp.concatenate([aim] * reps, axis=1)
            sre, sim = sre + bre * tre - bim * tim, sim + bre * tim + bim * tre
            are, aim = are * are - aim * aim, 2.0 * are * aim
            shift *= 2
        for b in range(n_lanes // n_chunks):
            lo, hi = (b + 1) * n_chunks - 128, (b + 1) * n_chunks
            xf_ref[:SSM_STATE, b * 128:(b + 1) * 128] = sre[:, lo:hi]
            xf_ref[SSM_STATE:, b * 128:(b + 1) * 128] = sim[:, lo:hi]
        keep = pos >= 1
        xin = jnp.concatenate([jnp.where(keep, pltpu.roll(sre, 1, 1), 0.0),
                               jnp.where(keep, pltpu.roll(sim, 1, 1), 0.0)], axis=0)
        y = y + jnp.dot(et_ref[...], xin.astype(BF16), preferred_element_type=F32)
    y_ref[...] = y.reshape(chunk, SSM_GROUP, n_lanes)


def _ssm_core(ut, mats, x0t, n_chunks):
    mt, ft, et, a_pow = mats
    chunk, _, n_lanes = ut.shape
    groups = SSM_GROUPS
    has_init = x0t is not None
    a_lanes = n_lanes if n_chunks == 1 else 128
    a_b = jnp.broadcast_to(a_pow[:, :, :, None], a_pow.shape + (a_lanes,))
    if x0t is None:
        x0t = jnp.zeros((groups, 2 * SSM_STATE, 128), F32)
    xf_lanes = n_lanes if n_chunks == 1 else (n_lanes // n_chunks) * 128

    def gspec(shape):
        return pl.BlockSpec((None,) + tuple(shape[1:]), lambda g: (g,) + (0,) * (len(shape) - 1))

    t_spec = pl.BlockSpec((chunk, SSM_GROUP, n_lanes), lambda g: (0, g, 0))
    return pl.pallas_call(
        functools.partial(_ssm_core_kernel, n_chunks=n_chunks, has_init=has_init),
        grid=(groups,),
        in_specs=[t_spec, gspec(mt.shape), gspec(ft.shape), gspec(et.shape), gspec(a_b.shape),
                  gspec(x0t.shape)],
        out_specs=(t_spec, gspec((groups, 2 * SSM_STATE, xf_lanes))),
        out_shape=(jax.ShapeDtypeStruct((chunk, D_MODEL, n_lanes), F32),
                   jax.ShapeDtypeStruct((groups, 2 * SSM_STATE, xf_lanes), F32)),
        compiler_params=_cparams("arbitrary"),
        name=f"ssm_core_{n_chunks}",
    )(ut, mt, ft, et, a_b, x0t)


def _ssm_matrices(a_re, a_im, log_dt, b_re, b_im, c_re, c_im, chunk):
    hp = lax.Precision.HIGHEST
    lam = lax.complex(a_re, a_im)
    dt = jnp.exp(log_dt)[:, None]
    abar = jnp.exp(lam * dt)
    bbar = ((abar - 1.0) / lam)[..., None] * lax.complex(b_re, b_im)
    steps = jnp.arange(chunk + 1, dtype=F32)[:, None, None]
    pw = jnp.exp(steps * (lam * dt)[None])
    w = bbar[None] * pw[..., None]
    kl = (jnp.einsum("ngpd,gcp->gcdn", w.real, c_re, precision=hp)
          - jnp.einsum("ngpd,gcp->gcdn", w.imag, c_im, precision=hp))
    i = np.arange(chunk)
    lag = i[:, None] - i[None, :]
    mt = jnp.where(jnp.asarray(lag >= 0), kl[:, :, :, np.maximum(lag, 0)], 0.0)
    mt = mt.transpose(0, 3, 1, 4, 2).reshape(SSM_GROUPS, SSM_GROUP * chunk, SSM_GROUP * chunk)
    wf = w[chunk - 1 - i]
    ft = jnp.stack([wf.real, wf.imag], axis=0).transpose(2, 0, 3, 1, 4)
    ft = ft.reshape(SSM_GROUPS, 2 * SSM_STATE, SSM_GROUP * chunk)
    ve = lax.complex(c_re, c_im)[None] * pw[1:, :, None, :]
    et = jnp.stack([ve.real, -ve.imag], axis=0).transpose(2, 1, 3, 0, 4)
    et = et.reshape(SSM_GROUPS, SSM_GROUP * chunk, 2 * SSM_STATE)
    a_pow = jnp.stack([pw[chunk].real, pw[chunk].imag], axis=1)
    return mt.astype(BF16), ft.astype(BF16), et.astype(BF16), a_pow


def _glu_kernel(y_ref, u_ref, d_ref, x_ref, w_ref, o_ref):
    yy = y_ref[...] + d_ref[...] * u_ref[...]
    z = jnp.dot(jax.nn.gelu(yy).astype(BF16), w_ref[...], preferred_element_type=F32)
    o_ref[...] = x_ref[...] + z[:, :D_MODEL] * jax.nn.sigmoid(z[:, D_MODEL:])


def _glu(y2d, u2d, d_skip, x2d, w_bf, tm):
    rows = x2d.shape[0]
    spec = pl.BlockSpec((tm, D_MODEL), lambda i: (i, 0))
    return pl.pallas_call(
        _glu_kernel,
        grid=(rows // tm,),
        in_specs=[spec, spec, _const_spec((1, D_MODEL)), spec, _const_spec((D_MODEL, 2 * D_MODEL))],
        out_specs=spec,
        out_shape=jax.ShapeDtypeStruct((rows, D_MODEL), F32),
        compiler_params=_cparams("arbitrary"),
        name="ssm_glu",
    )(y2d, u2d, d_skip.reshape(1, D_MODEL), x2d, w_bf)


def _ssm_mixer_prompt(x2d, batch, seq, norm_g, w_in_bf, mats, d_skip, w_glu_bf, perms, tm):
    assert seq % SSM_TOK_TILE == 0
    n_chunks = seq // SSM_CHUNK
    ut = _ssm_in(x2d, norm_g, w_in_bf, perms[0])
    yt, xf = _ssm_core(ut, mats, None, n_chunks)
    g2d = _ssm_gate(yt, ut, d_skip, perms[1])
    xf = xf.reshape(SSM_GROUPS, 2, SSM_STATE, batch, 128)[..., 127]
    return _glu_act(g2d, x2d, w_glu_bf, tm), xf.transpose(3, 0, 2, 1)


def _ssm_mixer_sample(x2d, batch, seq, norm_g, w_in_bf, mats, d_skip, w_glu_bf, state0, tm):
    u = _norm_matmul(x2d, norm_g, w_in_bf, tm)
    ut = u.reshape(batch, seq, D_MODEL).transpose(1, 2, 0)
    x0t = state0.transpose(1, 3, 2, 0).reshape(SSM_GROUPS, 2 * SSM_STATE, batch)
    yt, xf = _ssm_core(ut, mats, x0t, 1)
    y = yt.transpose(2, 0, 1).reshape(batch * seq, D_MODEL)
    state = xf.reshape(SSM_GROUPS, 2, SSM_STATE, batch).transpose(3, 0, 2, 1)
    return _glu(y, u, d_skip, x2d, w_glu_bf, tm), state


def kernel(x_prompt, x_sample, cache_kv_w128, cache_kv_w512, cache_kv_w2048, state_ssm, norm_mix, norm_ffn,
           w_qkv, q_norm, k_norm, w_o, ssm_w_in, ssm_a_re, ssm_a_im, ssm_log_dt, ssm_b_re, ssm_b_im,
           ssm_c_re, ssm_c_im, ssm_d, ssm_w_glu, ffn_w_gate, ffn_w_up, ffn_w_down):
    batch, seq, _ = x_prompt.shape
    dec_batch, dec_seq, _ = x_sample.shape
    caches = (cache_kv_w128, cache_kv_w512, cache_kv_w2048)
    assert seq % ROW_TILE == 0 and seq >= WINDOWS[-1] and (dec_batch * dec_seq) % ROW_TILE == 0
    assert all(c.shape[2] == w for c, w in zip(caches, WINDOWS))
    n_a = seq // P16
    tm_p = 512

    xp = x_prompt.reshape(batch * seq, D_MODEL)
    xs = x_sample.reshape(dec_batch * dec_seq, D_MODEL)

    common = _qkv_common_inputs(norm_mix[0], w_qkv[0].astype(BF16), q_norm[0], k_norm[0])
    w_o_bf = w_o[0].astype(BF16)
    qp, kp, vp, t0, t1, t2 = _qkv_prompt(xp, batch, seq, common)
    outs = [_attn_prompt(qp, kp, vp, g) for g in range(3)]
    xp = _merge_wo([o for o, _ in outs], [l for _, l in outs], xp, w_o_bf, True, batch, seq)
    kv_prompt = []
    for tail, win in zip((t0, t1, t2), WINDOWS):
        kv = tail.transpose(0, 2, 1, 3).reshape(batch, win, 2, N_HEADS, HEAD_DIM)
        kv_prompt.append(kv[None])

    pos_s = jnp.tile(PAST_LEN + jnp.arange(dec_seq), dec_batch)
    qs, ks, vs = _qkv_sample(xs, pos_s, common)
    s_outs = [_attn_sample(qs, ks, vs, caches[g][0].reshape(dec_batch, WINDOWS[g], 2 * ATTN_W), g, dec_seq)
              for g in range(3)]
    xs = _merge_wo([o for o, _, _ in s_outs], [l for _, l, _ in s_outs], xs, w_o_bf, False)
    kv_sample = [nc.reshape(1, dec_batch, win, 2, N_HEADS, HEAD_DIM)
                 for (_, _, nc), win in zip(s_outs, WINDOWS)]

    ffn_w = [(ffn_w_gate[i].astype(BF16), ffn_w_up[i].astype(BF16), ffn_w_down[i].astype(BF16))
             for i in range(2)]
    xp = _ffn(xp, norm_ffn[0], *ffn_w[0], tm_p)
    xs = _ffn(xs, norm_ffn[0], *ffn_w[0], ROW_TILE)

    ssm_p = (ssm_a_re[0], ssm_a_im[0], ssm_log_dt[0], ssm_b_re[0], ssm_b_im[0], ssm_c_re[0], ssm_c_im[0])
    w_in_bf = ssm_w_in[0].astype(BF16)
    w_glu_bf = ssm_w_glu[0].astype(BF16)
    xp, st_p = _ssm_mixer_prompt(xp, batch, seq, norm_mix[1], w_in_bf, _ssm_matrices(*ssm_p, SSM_CHUNK),
                                 ssm_d[0], w_glu_bf, _perm_matrices(), tm_p)
    xs, st_s = _ssm_mixer_sample(xs, dec_batch, dec_seq, norm_mix[1], w_in_bf, _ssm_matrices(*ssm_p, dec_seq),
                                 ssm_d[0], w_glu_bf, state_ssm[0], ROW_TILE)
    xp = _ffn(xp, norm_ffn[1], *ffn_w[1], tm_p)
    xs = _ffn(xs, norm_ffn[1], *ffn_w[1], ROW_TILE)

    return (xp.reshape(batch, seq, D_MODEL), xs.reshape(dec_batch, dec_seq, D_MODEL),
            kv_prompt[0], kv_prompt[1], kv_prompt[2], st_p[None],
            kv_sample[0], kv_sample[1], kv_sample[2], st_s[None])
```

```python
import functools
import math

import numpy as np
import jax
import jax.numpy as jnp
from jax import lax
from jax.experimental import pallas as pl
from jax.experimental.pallas import tpu as pltpu

F32 = jnp.float32
BF16 = jnp.bfloat16

D_MODEL = 1024
HEAD_DIM = 64
N_HEADS = 8
ATTN_W = N_HEADS * HEAD_DIM
DILATIONS = (1, 4, 16)
WINDOWS = (128, 512, 2048)
KEYS_BACK = 128
PAST_LEN = 16384
ROPE_THETA = 10000.0
RMS_EPS = 1e-6
SSM_GROUP = 16
SSM_GROUPS = D_MODEL // SSM_GROUP
SSM_STATE = 64
SSM_CHUNK = 16
ROW_TILE = 256
P16 = 16
NEG_BIG = -1e30
VMEM_LIMIT = 56 * 1024 * 1024


def _cparams(*sem):
    return pltpu.CompilerParams(dimension_semantics=tuple(sem), vmem_limit_bytes=VMEM_LIMIT)


def _const_spec(shape):
    nd = len(shape)
    return pl.BlockSpec(shape, lambda *_: (0,) * nd)


def _perm_matrices():
    m = np.arange(ROW_TILE)
    to_p16 = np.zeros((ROW_TILE, ROW_TILE), np.float32)
    to_p16[m, P16 * (m % P16) + m // P16] = 1.0
    return jnp.asarray(to_p16, BF16), jnp.asarray(to_p16.T, BF16)


def _rms(x, gain):
    ms = jnp.mean(x * x, axis=-1, keepdims=True)
    return x * lax.rsqrt(ms + RMS_EPS) * gain


def _qkv_slabs(x_ref, g_ref, w_ref, perm_ref, ones_ref, qg_ref, kg_ref, cos_ref, sin_ref, permute):
    h = _rms(x_ref[...], g_ref[...]).astype(BF16)
    if permute:
        h = jnp.dot(perm_ref[...], h, preferred_element_type=F32).astype(BF16)
    cos = jnp.concatenate([cos_ref[...]] * 4, axis=1)
    sin = jnp.concatenate([sin_ref[...]] * 4, axis=1)
    lane = lax.broadcasted_iota(jnp.int32, (ROW_TILE, ATTN_W), 1)
    first_half = (lane & (HEAD_DIM - 1)) < (HEAD_DIM // 2)
    ones = ones_ref[...]
    out = {}
    for s in range(3):
        for g in range(3):
            col = (s * 3 + g) * ATTN_W
            y = jnp.dot(h, w_ref[:, col:col + ATTN_W], preferred_element_type=F32)
            if s < 2:
                gain = (qg_ref if s == 0 else kg_ref)[...]
                yy = (y * y).astype(BF16)
                ss = jnp.concatenate(
                    [jnp.dot(yy[:, :256], ones, preferred_element_type=F32),
                     jnp.dot(yy[:, 256:], ones, preferred_element_type=F32)], axis=1)
                yn = y * lax.rsqrt(ss * (1.0 / HEAD_DIM) + RMS_EPS) * gain
                swapped = jnp.where(first_half,
                                    pltpu.roll(yn, ATTN_W - HEAD_DIM // 2, 1),
                                    pltpu.roll(yn, HEAD_DIM // 2, 1))
                y = yn * cos + swapped * sin
            out[(s, g)] = y
    return out


def _split3_bf16(x):
    hi = x.astype(BF16)
    r1 = x - hi.astype(F32)
    mid = r1.astype(BF16)
    lo = (r1 - mid.astype(F32)).astype(BF16)
    return hi, mid, lo


def _qkv_prompt_kernel(x_ref, g_ref, w_ref, perm_ref, permt_ref, ones_ref, qg_ref, kg_ref, cos_ref, sin_ref,
                       q_ref, k_ref, v_ref, t0_ref, t1_ref, t2_ref, *, tail_first):
    slabs = _qkv_slabs(x_ref, g_ref, w_ref, perm_ref, ones_ref, qg_ref, kg_ref, cos_ref, sin_ref, True)
    tails = (t0_ref, t1_ref, t2_ref)
    t = pl.program_id(1)
    for g in range(3):
        lo, hi = g * ATTN_W, (g + 1) * ATTN_W
        for s, ref in enumerate((q_ref, k_ref, v_ref)):
            ref[:, :, lo:hi] = slabs[(s, g)].reshape(P16, P16, ATTN_W).astype(BF16)

        @pl.when(t >= tail_first[g])
        def _(g=g):
            kv = jnp.concatenate([slabs[(1, g)], slabs[(2, g)]], axis=1)
            nat = sum(jnp.dot(permt_ref[...], part, preferred_element_type=F32) for part in _split3_bf16(kv))
            keep = tails[g].shape[-1]
            tails[g][...] = nat[ROW_TILE - keep:].T.reshape(2, N_HEADS, HEAD_DIM, keep)


def _qkv_sample_kernel(x_ref, g_ref, w_ref, perm_ref, ones_ref, qg_ref, kg_ref, cos_ref, sin_ref,
                       q_ref, k_ref, v_ref):
    slabs = _qkv_slabs(x_ref, g_ref, w_ref, perm_ref, ones_ref, qg_ref, kg_ref, cos_ref, sin_ref, False)
    for g in range(3):
        lo, hi = g * ATTN_W, (g + 1) * ATTN_W
        for s, ref in enumerate((q_ref, k_ref, v_ref)):
            ref[:, lo:hi] = slabs[(s, g)]


def _qkv_common_inputs(norm_g, w_qkv_bf, q_gain, k_gain):
    ones = np.kron(np.eye(4, dtype=np.float32), np.ones((HEAD_DIM, HEAD_DIM), np.float32))
    return (norm_g.reshape(1, D_MODEL), w_qkv_bf, _perm_matrices()[0], jnp.asarray(ones, BF16),
            jnp.tile(q_gain, N_HEADS).reshape(1, ATTN_W), jnp.tile(k_gain, N_HEADS).reshape(1, ATTN_W))


def _rope_tables(pos):
    half = HEAD_DIM // 2
    inv = ROPE_THETA ** (-jnp.arange(half, dtype=F32) / half)
    ang = pos.astype(F32)[:, None] * inv[None, :]
    cos, sin = jnp.cos(ang), jnp.sin(ang)
    cos = jnp.concatenate([cos, cos, cos, cos], axis=1)
    sin = jnp.concatenate([-sin, sin, -sin, sin], axis=1)
    return cos, sin


def _p16_positions(seq):
    t = np.arange(seq).reshape(seq // ROW_TILE, P16, P16)
    return jnp.asarray(t.transpose(0, 2, 1).reshape(seq))


def _qkv_prompt(x2d, batch, seq, common):
    n_tb = seq // ROW_TILE
    n_a = seq // P16
    cos, sin = _rope_tables(_p16_positions(seq))
    qkv_shape = jax.ShapeDtypeStruct((batch, P16, n_a, 3 * ATTN_W), BF16)
    out_shape = (qkv_shape,) * 3 + tuple(
        jax.ShapeDtypeStruct((batch, 2, N_HEADS, HEAD_DIM, w), F32) for w in WINDOWS)
    qkv_spec = pl.BlockSpec((None, P16, P16, 3 * ATTN_W), lambda b, t: (b, 0, t, 0))
    tail_blk = tuple(min(w, ROW_TILE) for w in WINDOWS)
    tail_first = tuple(n_tb - w // blk for w, blk in zip(WINDOWS, tail_blk))

    def tail_spec(blk, first):
        return pl.BlockSpec((None, 2, N_HEADS, HEAD_DIM, blk),
                            lambda b, t: (b, 0, 0, 0, jnp.maximum(t - first, 0)))

    norm_g, w_bf, perm, ones, qg, kg = common
    return pl.pallas_call(
        functools.partial(_qkv_prompt_kernel, tail_first=tail_first),
        grid=(batch, n_tb),
        in_specs=[pl.BlockSpec((ROW_TILE, D_MODEL), lambda b, t: (b * n_tb + t, 0)),
                  _const_spec((1, D_MODEL)), _const_spec((D_MODEL, 9 * ATTN_W)),
                  _const_spec((ROW_TILE, ROW_TILE)), _const_spec((ROW_TILE, ROW_TILE)), _const_spec((256, 256)),
                  _const_spec((1, ATTN_W)), _const_spec((1, ATTN_W)),
                  pl.BlockSpec((ROW_TILE, 128), lambda b, t: (t, 0)),
                  pl.BlockSpec((ROW_TILE, 128), lambda b, t: (t, 0))],
        out_specs=(qkv_spec,) * 3 + tuple(tail_spec(blk, first) for blk, first in zip(tail_blk, tail_first)),
        out_shape=out_shape,
        compiler_params=_cparams("arbitrary", "arbitrary"),
        name="qkv_prompt",
    )(x2d, norm_g, w_bf, perm, _perm_matrices()[1], ones, qg, kg, cos, sin)


def _qkv_sample(x2d, pos, common):
    rows = x2d.shape[0]
    cos, sin = _rope_tables(pos)
    spec = pl.BlockSpec((ROW_TILE, 3 * ATTN_W), lambda i: (i, 0))
    return pl.pallas_call(
        _qkv_sample_kernel,
        grid=(rows // ROW_TILE,),
        in_specs=[pl.BlockSpec((ROW_TILE, D_MODEL), lambda i: (i, 0)),
                  _const_spec((1, D_MODEL)), _const_spec((D_MODEL, 9 * ATTN_W)),
                  _const_spec((ROW_TILE, ROW_TILE)), _const_spec((256, 256)),
                  _const_spec((1, ATTN_W)), _const_spec((1, ATTN_W)),
                  pl.BlockSpec((ROW_TILE, 128), lambda i: (i, 0)),
                  pl.BlockSpec((ROW_TILE, 128), lambda i: (i, 0))],
        out_specs=(spec,) * 3,
        out_shape=(jax.ShapeDtypeStruct((rows, 3 * ATTN_W), F32),) * 3,
        compiler_params=_cparams("arbitrary"),
        name="qkv_sample",
    )(x2d, *common, cos, sin)


def _attn_geometry(dil, n_a):
    nres = P16 // dil
    qa = ROW_TILE // nres
    halo = 0 if qa >= n_a else max(P16, -(-KEYS_BACK // nres))
    klen = min(qa + halo, n_a)
    return nres, qa, halo, klen


def _attn_bias_tables(dil, n_a):
    nres, qa, halo, klen = _attn_geometry(dil, n_a)
    m = np.arange(nres * qa)
    n = np.arange(nres * klen)
    qoff = nres * (m % qa) + m // qa
    koff = nres * (n % klen) + n // klen
    tables = []
    for delta in (0, halo):
        dist = qoff[:, None] - koff[None, :] + nres * delta
        tables.append(np.where((dist >= 0) & (dist <= KEYS_BACK), 0.0, NEG_BIG).astype(np.float32))
    return jnp.asarray(np.stack(tables))


def _attn_group(bias_ref, q_ref, k_ref, v_ref, o_ref, l_ref, dil, n_a):
    nres, qa, halo, klen = _attn_geometry(dil, n_a)
    n_atiles = n_a // qa
    lane = lax.broadcasted_iota(jnp.int32, (nres * qa, 128), 1)
    head1 = lane >= HEAD_DIM
    head_masks = (jnp.logical_not(head1), head1)

    def tile(t, carry):
        c = t // n_atiles
        at = t % n_atiles
        a0 = pl.multiple_of(at * qa, P16)
        k0 = pl.multiple_of(jnp.maximum(at * qa - halo, 0), P16)
        rows = [c + dil * b for b in range(nres)]
        q = jnp.concatenate([q_ref[r, pl.ds(a0, qa), :] for r in rows], axis=0)
        k = jnp.concatenate([k_ref[r, pl.ds(k0, klen), :] for r in rows], axis=0)
        v = jnp.concatenate([v_ref[r, pl.ds(k0, klen), :] for r in rows], axis=0)
        bias = bias_ref[jnp.minimum(at, 1)]
        q = q * jnp.asarray(HEAD_DIM ** -0.5, BF16)
        outs, lses = [], []
        for hh in range(2):
            qm = jnp.where(head_masks[hh], q, jnp.zeros_like(q))
            s = lax.dot_general(qm, k, (((1,), (1,)), ((), ())), preferred_element_type=F32) + bias
            mx = jnp.max(s, axis=-1, keepdims=True)
            p = jnp.exp(s - mx)
            den = jnp.sum(p, axis=-1, keepdims=True)
            o = jnp.dot(p.astype(BF16), v, preferred_element_type=F32)
            outs.append(o / den)
            lses.append(mx + jnp.log(den))
        o = jnp.where(head1, outs[1], outs[0])
        lse = jnp.where(head1, lses[1], lses[0])
        for b, r in enumerate(rows):
            o_ref[r, pl.ds(a0, qa), :] = o[b * qa:(b + 1) * qa]
            l_ref[r, pl.ds(a0, qa), :] = lse[b * qa:(b + 1) * qa]
        return carry

    lax.fori_loop(0, dil * n_atiles, tile, 0, unroll=2)


def _attn_prompt_kernel(b0, b1, b2, q0, k0, v0, q1, k1, v1, q2, k2, v2, o_ref, og_ref, lg_ref, *, n_a):
    groups = ((b0, q0, k0, v0), (b1, q1, k1, v1), (b2, q2, k2, v2))
    for g, (bias_ref, q_ref, k_ref, v_ref) in enumerate(groups):
        _attn_group(bias_ref, q_ref, k_ref, v_ref, og_ref.at[g], lg_ref.at[g], DILATIONS[g], n_a)

    def merge(r, carry):
        ls = [lg_ref[g, r] for g in range(3)]
        mx = jnp.maximum(jnp.maximum(ls[0], ls[1]), ls[2])
        es = [jnp.exp(l - mx) for l in ls]
        num = es[0] * og_ref[0, r] + es[1] * og_ref[1, r] + es[2] * og_ref[2, r]
        o_ref[r] = (num / (es[0] + es[1] + es[2])).astype(BF16)
        return carry

    lax.fori_loop(0, P16, merge, 0)


def _attn_prompt(q, k, v):
    batch, _, n_a, _ = q.shape
    biases = [_attn_bias_tables(dil, n_a) for dil in DILATIONS]
    in_specs = [_const_spec(b.shape) for b in biases]
    for g in range(3):
        in_specs += [pl.BlockSpec((None, P16, n_a, 128), lambda b, h, g=g: (b, 0, 0, g * 4 + h))] * 3
    operands = list(biases)
    for g in range(3):
        operands += [q, k, v]
    return pl.pallas_call(
        functools.partial(_attn_prompt_kernel, n_a=n_a),
        grid=(batch, 4),
        in_specs=in_specs,
        out_specs=pl.BlockSpec((None, P16, n_a, 128), lambda b, h: (b, 0, 0, h)),
        out_shape=jax.ShapeDtypeStruct((batch, P16, n_a, ATTN_W), BF16),
        scratch_shapes=[pltpu.VMEM((3, P16, n_a, 128), F32), pltpu.VMEM((3, P16, n_a, 128), F32)],
        compiler_params=_cparams("arbitrary", "arbitrary"),
        name="attn_prompt",
    )(*operands)


def _attn_sample_kernel(q_ref, n_ref, c_ref, o_ref, l_ref, nc_ref, *, dil, win):
    n_new = q_ref.shape[1]
    nt = (((1,), (1,)), ((), ()))
    dist_c = (win + lax.broadcasted_iota(jnp.int32, (n_new, win), 0)
              - lax.broadcasted_iota(jnp.int32, (n_new, win), 1))
    ok_c = (dist_c <= win) & ((dist_c & (dil - 1)) == 0)
    dist_n = (lax.broadcasted_iota(jnp.int32, (n_new, n_new), 0)
              - lax.broadcasted_iota(jnp.int32, (n_new, n_new), 1))
    ok_n = (dist_n >= 0) & ((dist_n & (dil - 1)) == 0)
    lane = lax.broadcasted_iota(jnp.int32, (HEAD_DIM, 128), 1)
    pad_rows = jnp.zeros((128 - n_new, 128), F32)
    for h in range(N_HEADS):
        q = q_ref[h] * (HEAD_DIM ** -0.5)
        kt, vt = c_ref[0, h], c_ref[1, h]
        s_c = jnp.dot(q[:, :HEAD_DIM].astype(BF16), kt.astype(BF16), preferred_element_type=F32)
        s_n = lax.dot_general(q, n_ref[0, h], nt, preferred_element_type=F32)
        s_c = jnp.where(ok_c, s_c, NEG_BIG)
        s_n = jnp.where(ok_n, s_n, NEG_BIG)
        mx = jnp.maximum(jnp.max(s_c, axis=-1, keepdims=True), jnp.max(s_n, axis=-1, keepdims=True))
        p_c = jnp.exp(s_c - mx)
        p_n = jnp.exp(s_n - mx)
        den = jnp.sum(p_c, axis=-1, keepdims=True) + jnp.sum(p_n, axis=-1, keepdims=True)
        o = (lax.dot_general(p_c.astype(BF16), vt.astype(BF16), nt, preferred_element_type=F32)
             + jnp.dot(p_n, n_ref[1, h], preferred_element_type=F32)[:, :HEAD_DIM])
        o_ref[h] = o / den
        l_ref[h] = jnp.broadcast_to(mx + jnp.log(den), (n_new, HEAD_DIM))
        for kv, old in enumerate((kt, vt)):
            rolled = pltpu.roll(old, win - n_new, 1)
            new_t = jnp.concatenate([pad_rows, n_ref[kv, h]], axis=0).T[:HEAD_DIM]
            if win > 128:
                nc_ref[kv, h, :, :win - 128] = rolled[:, :win - 128]
            nc_ref[kv, h, :, win - 128:] = jnp.where(lane < 128 - n_new, rolled[:, win - 128:], new_t)


def _attn_sample(q_p, new_p, cache_t, g):
    batch, _, _, n_new, _ = q_p.shape
    win = cache_t.shape[-1]
    dil = DILATIONS[g]
    c_spec = pl.BlockSpec((None, 2, N_HEADS, HEAD_DIM, win), lambda b: (b, 0, 0, 0, 0))
    out_spec = pl.BlockSpec((None, N_HEADS, n_new, HEAD_DIM), lambda b: (b, 0, 0, 0))
    out_shape = jax.ShapeDtypeStruct((batch, N_HEADS, n_new, HEAD_DIM), F32)
    return pl.pallas_call(
        functools.partial(_attn_sample_kernel, dil=dil, win=win),
        grid=(batch,),
        in_specs=[pl.BlockSpec((None, None, N_HEADS, n_new, 128), lambda b: (b, g, 0, 0, 0)),
                  pl.BlockSpec((None, None, 2, N_HEADS, n_new, 128), lambda b: (b, g, 0, 0, 0, 0)),
                  c_spec],
        out_specs=(out_spec, out_spec, c_spec),
        out_shape=(out_shape, out_shape, jax.ShapeDtypeStruct(cache_t.shape, F32)),
        compiler_params=_cparams("arbitrary"),
        name=f"attn_sample_d{dil}",
    )(q_p, new_p, cache_t)


def _merge_wo_kernel(o0, o1, o2, l0, l1, l2, x_ref, w_ref, y_ref):
    os_ = [r[...] for r in (o0, o1, o2)]
    ls = [r[...] for r in (l0, l1, l2)]
    mx = jnp.maximum(jnp.maximum(ls[0], ls[1]), ls[2])
    es = [jnp.exp(l - mx) for l in ls]
    den = es[0] + es[1] + es[2]
    o = ((es[0] * os_[0] + es[1] * os_[1] + es[2] * os_[2]) / den).astype(BF16)
    y_ref[...] = x_ref[...] + jnp.dot(o, w_ref[...], preferred_element_type=F32)


def _merge_wo(os_, ls, x2d, w_o_bf):
    rows = x2d.shape[0]
    x_spec = pl.BlockSpec((ROW_TILE, D_MODEL), lambda i: (i, 0))
    a_spec = pl.BlockSpec((ROW_TILE, ATTN_W), lambda i: (i, 0))
    return pl.pallas_call(
        _merge_wo_kernel,
        grid=(rows // ROW_TILE,),
        in_specs=[a_spec] * 6 + [x_spec, _const_spec((ATTN_W, D_MODEL))],
        out_specs=x_spec,
        out_shape=jax.ShapeDtypeStruct((rows, D_MODEL), F32),
        compiler_params=_cparams("arbitrary"),
        name="merge_wo",
    )(*os_, *ls, x2d, w_o_bf)


def _wo_p16_kernel(o_ref, x_ref, w_ref, permt_ref, y_ref):
    o = o_ref[...].astype(F32).reshape(ROW_TILE, ATTN_W).astype(BF16)
    o = jnp.dot(permt_ref[...], o, preferred_element_type=F32).astype(BF16)
    y_ref[...] = x_ref[...] + jnp.dot(o, w_ref[...], preferred_element_type=F32)


def _wo_p16(o_p16, x2d, w_o_bf, seq):
    rows = x2d.shape[0]
    n_tb = seq // ROW_TILE
    x_spec = pl.BlockSpec((ROW_TILE, D_MODEL), lambda i: (i, 0))
    return pl.pallas_call(
        _wo_p16_kernel,
        grid=(rows // ROW_TILE,),
        in_specs=[pl.BlockSpec((None, P16, P16, ATTN_W), lambda i: (i // n_tb, 0, i % n_tb, 0)),
                  x_spec, _const_spec((ATTN_W, D_MODEL)), _const_spec((ROW_TILE, ROW_TILE))],
        out_specs=x_spec,
        out_shape=jax.ShapeDtypeStruct((rows, D_MODEL), F32),
        compiler_params=_cparams("arbitrary"),
        name="wo_p16",
    )(o_p16, x2d, w_o_bf, _perm_matrices()[1])


def _ffn_kernel(x_ref, g_ref, wg_ref, wu_ref, wd_ref, y_ref, *, chunks):
    x = x_ref[...]
    h = _rms(x, g_ref[...]).astype(BF16)
    acc = x
    for lo, hi in chunks:
        gate = jnp.dot(h, wg_ref[:, lo:hi], preferred_element_type=F32)
        up = jnp.dot(h, wu_ref[:, lo:hi], preferred_element_type=F32)
        act = (jax.nn.silu(gate) * up).astype(BF16)
        acc = acc + jnp.dot(act, wd_ref[lo:hi, :], preferred_element_type=F32)
    y_ref[...] = acc


def _ffn(x2d, norm_g, wg_bf, wu_bf, wd_bf, tm):
    rows = x2d.shape[0]
    d_ff = wg_bf.shape[1]
    step = 1024
    chunks = tuple((lo, min(lo + step, d_ff)) for lo in range(0, d_ff, step))
    x_spec = pl.BlockSpec((tm, D_MODEL), lambda i: (i, 0))
    w_in = pl.BlockSpec((D_MODEL, d_ff), lambda i: (0, 0), pipeline_mode=pl.Buffered(1))
    w_out = pl.BlockSpec((d_ff, D_MODEL), lambda i: (0, 0), pipeline_mode=pl.Buffered(1))
    return pl.pallas_call(
        functools.partial(_ffn_kernel, chunks=chunks),
        grid=(rows // tm,),
        in_specs=[x_spec, _const_spec((1, D_MODEL)), w_in, w_in, w_out],
        out_specs=x_spec,
        out_shape=jax.ShapeDtypeStruct((rows, D_MODEL), F32),
        compiler_params=_cparams("arbitrary"),
        name="ffn",
    )(x2d, norm_g.reshape(1, D_MODEL), wg_bf, wu_bf, wd_bf)


def _norm_matmul_kernel(x_ref, g_ref, w_ref, y_ref):
    h = _rms(x_ref[...], g_ref[...]).astype(BF16)
    y_ref[...] = jnp.dot(h, w_ref[...], preferred_element_type=F32)


def _norm_matmul(x2d, norm_g, w_bf, tm):
    rows = x2d.shape[0]
    n = w_bf.shape[1]
    return pl.pallas_call(
        _norm_matmul_kernel,
        grid=(rows // tm,),
        in_specs=[pl.BlockSpec((tm, D_MODEL), lambda i: (i, 0)), _const_spec((1, D_MODEL)),
                  _const_spec((D_MODEL, n))],
        out_specs=pl.BlockSpec((tm, n), lambda i: (i, 0)),
        out_shape=jax.ShapeDtypeStruct((rows, n), F32),
        compiler_params=_cparams("arbitrary"),
        name="ssm_in_proj",
    )(x2d, norm_g.reshape(1, D_MODEL), w_bf)


SSM_LANE_TILE = 128
SSM_TOK_TILE = SSM_LANE_TILE * SSM_CHUNK


def _ssm_in_kernel(x_ref, g_ref, w_ref, perm_ref, ut_ref):
    n_blk = SSM_TOK_TILE // ROW_TILE
    blocks = []
    for kb in range(n_blk):
        h = _rms(x_ref[kb * ROW_TILE:(kb + 1) * ROW_TILE, :], g_ref[...]).astype(BF16)
        blocks.append(jnp.dot(perm_ref[...], h, preferred_element_type=F32).astype(BF16))
    hp = jnp.concatenate([blk[j * P16:(j + 1) * P16] for j in range(SSM_CHUNK) for blk in blocks], axis=0)
    u = jnp.dot(hp, w_ref[...], preferred_element_type=F32)
    for j in range(SSM_CHUNK):
        ut_ref[j] = u[j * SSM_LANE_TILE:(j + 1) * SSM_LANE_TILE].T


def _ssm_in(x2d, norm_g, w_bf, perm):
    rows = x2d.shape[0]
    n_tiles = rows // SSM_TOK_TILE
    return pl.pallas_call(
        _ssm_in_kernel,
        grid=(n_tiles,),
        in_specs=[pl.BlockSpec((SSM_TOK_TILE, D_MODEL), lambda i: (i, 0)), _const_spec((1, D_MODEL)),
                  pl.BlockSpec((D_MODEL, D_MODEL), lambda i: (0, 0), pipeline_mode=pl.Buffered(1)),
                  _const_spec((ROW_TILE, ROW_TILE))],
        out_specs=pl.BlockSpec((SSM_CHUNK, D_MODEL, SSM_LANE_TILE), lambda i: (0, 0, i)),
        out_shape=jax.ShapeDtypeStruct((SSM_CHUNK, D_MODEL, n_tiles * SSM_LANE_TILE), F32),
        compiler_params=_cparams("arbitrary"),
        name="ssm_in_t",
    )(x2d, norm_g.reshape(1, D_MODEL), w_bf, perm)


def _ssm_gate_kernel(yt_ref, ut_ref, d_ref, permt_ref, g_ref):
    n_blk = SSM_TOK_TILE // ROW_TILE
    pieces = []
    for i in range(SSM_CHUNK):
        yy = yt_ref[i] + d_ref[...] * ut_ref[i]
        pieces.append(jax.nn.gelu(yy).T.astype(BF16))
    for kb in range(n_blk):
        blk = jnp.concatenate([p[kb * P16:(kb + 1) * P16] for p in pieces], axis=0)
        g_ref[kb * ROW_TILE:(kb + 1) * ROW_TILE, :] = jnp.dot(
            permt_ref[...], blk, preferred_element_type=F32).astype(BF16)


def _ssm_gate(yt, ut, d_skip, permt):
    n_tiles = yt.shape[2] // SSM_LANE_TILE
    half = D_MODEL // 2
    t_spec = pl.BlockSpec((SSM_CHUNK, half, SSM_LANE_TILE), lambda i, c: (0, c, i))
    d_b = jnp.broadcast_to(d_skip[:, None], (D_MODEL, SSM_LANE_TILE))
    return pl.pallas_call(
        _ssm_gate_kernel,
        grid=(n_tiles, 2),
        in_specs=[t_spec, t_spec, pl.BlockSpec((half, SSM_LANE_TILE), lambda i, c: (c, 0)),
                  _const_spec((ROW_TILE, ROW_TILE))],
        out_specs=pl.BlockSpec((SSM_TOK_TILE, half), lambda i, c: (i, c)),
        out_shape=jax.ShapeDtypeStruct((n_tiles * SSM_TOK_TILE, D_MODEL), BF16),
        compiler_params=_cparams("arbitrary", "arbitrary"),
        name="ssm_gate_t",
    )(yt, ut, d_b, permt)


def _glu_act_kernel(g_ref, x_ref, w_ref, o_ref):
    z = jnp.dot(g_ref[...], w_ref[...], preferred_element_type=F32)
    o_ref[...] = x_ref[...] + z[:, :D_MODEL] * jax.nn.sigmoid(z[:, D_MODEL:])


def _glu_act(g2d, x2d, w_bf, tm):
    rows = x2d.shape[0]
    spec = pl.BlockSpec((tm, D_MODEL), lambda i: (i, 0))
    return pl.pallas_call(
        _glu_act_kernel,
        grid=(rows // tm,),
        in_specs=[spec, spec, _const_spec((D_MODEL, 2 * D_MODEL))],
        out_specs=spec,
        out_shape=jax.ShapeDtypeStruct((rows, D_MODEL), F32),
        compiler_params=_cparams("arbitrary"),
        name="ssm_glu_act",
    )(g2d, x2d, w_bf)


def _ssm_core_kernel(u_ref, mt_ref, ft_ref, et_ref, a_ref, x0_ref, y_ref, xf_ref, *, n_chunks, has_init):
    chunk, _, n_lanes = u_ref.shape
    u = u_ref[...].reshape(chunk * SSM_GROUP, n_lanes).astype(BF16)
    y = jnp.dot(mt_ref[...], u, preferred_element_type=F32)
    s = jnp.dot(ft_ref[...], u, preferred_element_type=F32)
    sre, sim = s[:SSM_STATE], s[SSM_STATE:]
    are, aim = a_ref[0], a_ref[1]
    if n_chunks == 1:
        x0 = x0_ref[...]
        xre, xim = x0[:SSM_STATE], x0[SSM_STATE:]
        xf_ref[:SSM_STATE, :] = are * xre - aim * xim + sre
        xf_ref[SSM_STATE:, :] = are * xim + aim * xre + sim
        y = y + jnp.dot(et_ref[...], x0.astype(BF16), preferred_element_type=F32)
    else:
        assert not has_init and n_chunks % 128 == 0
        reps = n_lanes // 128
        pos = lax.broadcasted_iota(jnp.int32, (SSM_STATE, n_lanes), 1) & (n_chunks - 1)
        shift = 1
        while shift < n_chunks:
            keep = pos >= shift
            tre = jnp.where(keep, pltpu.roll(sre, shift, 1), 0.0)
            tim = jnp.where(keep, pltpu.roll(sim, shift, 1), 0.0)
            bre = jnp.concatenate([are] * reps, axis=1)
            bim = jnp.concatenate([aim] * reps, axis=1)
            sre, sim = sre + bre * tre - bim * tim, sim + bre * tim + bim * tre
            are, aim = are * are - aim * aim, 2.0 * are * aim
            shift *= 2
        for b in range(n_lanes // n_chunks):
            lo, hi = (b + 1) * n_chunks - 128, (b + 1) * n_chunks
            xf_ref[:SSM_STATE, b * 128:(b + 1) * 128] = sre[:, lo:hi]
            xf_ref[SSM_STATE:, b * 128:(b + 1) * 128] = sim[:, lo:hi]
        keep = pos >= 1
        xin = jnp.concatenate([jnp.where(keep, pltpu.roll(sre, 1, 1), 0.0),
                               jnp.where(keep, pltpu.roll(sim, 1, 1), 0.0)], axis=0)
        y = y + jnp.dot(et_ref[...], xin.astype(BF16), preferred_element_type=F32)
    y_ref[...] = y.reshape(chunk, SSM_GROUP, n_lanes)


def _ssm_core(ut, mats, x0t, n_chunks):
    mt, ft, et, a_pow = mats
    chunk, _, n_lanes = ut.shape
    groups = SSM_GROUPS
    has_init = x0t is not None
    a_lanes = n_lanes if n_chunks == 1 else 128
    a_b = jnp.broadcast_to(a_pow[:, :, :, None], a_pow.shape + (a_lanes,))
    if x0t is None:
        x0t = jnp.zeros((groups, 2 * SSM_STATE, 128), F32)
    xf_lanes = n_lanes if n_chunks == 1 else (n_lanes // n_chunks) * 128

    def gspec(shape):
        return pl.BlockSpec((None,) + tuple(shape[1:]), lambda g: (g,) + (0,) * (len(shape) - 1))

    t_spec = pl.BlockSpec((chunk, SSM_GROUP, n_lanes), lambda g: (0, g, 0))
    return pl.pallas_call(
        functools.partial(_ssm_core_kernel, n_chunks=n_chunks, has_init=has_init),
        grid=(groups,),
        in_specs=[t_spec, gspec(mt.shape), gspec(ft.shape), gspec(et.shape), gspec(a_b.shape),
                  gspec(x0t.shape)],
        out_specs=(t_spec, gspec((groups, 2 * SSM_STATE, xf_lanes))),
        out_shape=(jax.ShapeDtypeStruct((chunk, D_MODEL, n_lanes), F32),
                   jax.ShapeDtypeStruct((groups, 2 * SSM_STATE, xf_lanes), F32)),
        compiler_params=_cparams("arbitrary"),
        name=f"ssm_core_{n_chunks}",
    )(ut, mt, ft, et, a_b, x0t)


def _ssm_matrices(a_re, a_im, log_dt, b_re, b_im, c_re, c_im, chunk):
    hp = lax.Precision.HIGHEST
    lam = lax.complex(a_re, a_im)
    dt = jnp.exp(log_dt)[:, None]
    abar = jnp.exp(lam * dt)
    bbar = ((abar - 1.0) / lam)[..., None] * lax.complex(b_re, b_im)
    steps = jnp.arange(chunk + 1, dtype=F32)[:, None, None]
    pw = jnp.exp(steps * (lam * dt)[None])
    w = bbar[None] * pw[..., None]
    kl = (jnp.einsum("ngpd,gcp->gcdn", w.real, c_re, precision=hp)
          - jnp.einsum("ngpd,gcp->gcdn", w.imag, c_im, precision=hp))
    i = np.arange(chunk)
    lag = i[:, None] - i[None, :]
    mt = jnp.where(jnp.asarray(lag >= 0), kl[:, :, :, np.maximum(lag, 0)], 0.0)
    mt = mt.transpose(0, 3, 1, 4, 2).reshape(SSM_GROUPS, SSM_GROUP * chunk, SSM_GROUP * chunk)
    wf = w[chunk - 1 - i]
    ft = jnp.stack([wf.real, wf.imag], axis=0).transpose(2, 0, 3, 1, 4)
    ft = ft.reshape(SSM_GROUPS, 2 * SSM_STATE, SSM_GROUP * chunk)
    ve = lax.complex(c_re, c_im)[None] * pw[1:, :, None, :]
    et = jnp.stack([ve.real, -ve.imag], axis=0).transpose(2, 1, 3, 0, 4)
    et = et.reshape(SSM_GROUPS, SSM_GROUP * chunk, 2 * SSM_STATE)
    a_pow = jnp.stack([pw[chunk].real, pw[chunk].imag], axis=1)
    return mt.astype(BF16), ft.astype(BF16), et.astype(BF16), a_pow


def _glu_kernel(y_ref, u_ref, d_ref, x_ref, w_ref, o_ref):
    yy = y_ref[...] + d_ref[...] * u_ref[...]
    z = jnp.dot(jax.nn.gelu(yy).astype(BF16), w_ref[...], preferred_element_type=F32)
    o_ref[...] = x_ref[...] + z[:, :D_MODEL] * jax.nn.sigmoid(z[:, D_MODEL:])


def _glu(y2d, u2d, d_skip, x2d, w_bf, tm):
    rows = x2d.shape[0]
    spec = pl.BlockSpec((tm, D_MODEL), lambda i: (i, 0))
    return pl.pallas_call(
        _glu_kernel,
        grid=(rows // tm,),
        in_specs=[spec, spec, _const_spec((1, D_MODEL)), spec, _const_spec((D_MODEL, 2 * D_MODEL))],
        out_specs=spec,
        out_shape=jax.ShapeDtypeStruct((rows, D_MODEL), F32),
        compiler_params=_cparams("arbitrary"),
        name="ssm_glu",
    )(y2d, u2d, d_skip.reshape(1, D_MODEL), x2d, w_bf)


def _ssm_mixer_prompt(x2d, batch, seq, norm_g, w_in_bf, mats, d_skip, w_glu_bf, perms, tm):
    assert seq % SSM_TOK_TILE == 0
    n_chunks = seq // SSM_CHUNK
    ut = _ssm_in(x2d, norm_g, w_in_bf, perms[0])
    yt, xf = _ssm_core(ut, mats, None, n_chunks)
    g2d = _ssm_gate(yt, ut, d_skip, perms[1])
    xf = xf.reshape(SSM_GROUPS, 2, SSM_STATE, batch, 128)[..., 127]
    return _glu_act(g2d, x2d, w_glu_bf, tm), xf.transpose(3, 0, 2, 1)


def _ssm_mixer_sample(x2d, batch, seq, norm_g, w_in_bf, mats, d_skip, w_glu_bf, state0, tm):
    u = _norm_matmul(x2d, norm_g, w_in_bf, tm)
    ut = u.reshape(batch, seq, D_MODEL).transpose(1, 2, 0)
    x0t = state0.transpose(1, 3, 2, 0).reshape(SSM_GROUPS, 2 * SSM_STATE, batch)
    yt, xf = _ssm_core(ut, mats, x0t, 1)
    y = yt.transpose(2, 0, 1).reshape(batch * seq, D_MODEL)
    state = xf.reshape(SSM_GROUPS, 2, SSM_STATE, batch).transpose(3, 0, 2, 1)
    return _glu(y, u, d_skip, x2d, w_glu_bf, tm), state


def kernel(x_prompt, x_sample, cache_kv_w128, cache_kv_w512, cache_kv_w2048, state_ssm, norm_mix, norm_ffn,
           w_qkv, q_norm, k_norm, w_o, ssm_w_in, ssm_a_re, ssm_a_im, ssm_log_dt, ssm_b_re, ssm_b_im,
           ssm_c_re, ssm_c_im, ssm_d, ssm_w_glu, ffn_w_gate, ffn_w_up, ffn_w_down):
    batch, seq, _ = x_prompt.shape
    dec_batch, dec_seq, _ = x_sample.shape
    caches = (cache_kv_w128, cache_kv_w512, cache_kv_w2048)
    assert seq % ROW_TILE == 0 and seq >= WINDOWS[-1] and (dec_batch * dec_seq) % ROW_TILE == 0
    assert all(c.shape[2] == w for c, w in zip(caches, WINDOWS))
    n_a = seq // P16
    tm_p = 512

    xp = x_prompt.reshape(batch * seq, D_MODEL)
    xs = x_sample.reshape(dec_batch * dec_seq, D_MODEL)

    common = _qkv_common_inputs(norm_mix[0], w_qkv[0].astype(BF16), q_norm[0], k_norm[0])
    w_o_bf = w_o[0].astype(BF16)
    qp, kp, vp, t0, t1, t2 = _qkv_prompt(xp, batch, seq, common)
    xp = _wo_p16(_attn_prompt(qp, kp, vp), xp, w_o_bf, seq)
    kv_prompt = [tail.transpose(0, 4, 1, 2, 3)[None] for tail in (t0, t1, t2)]

    pos_s = jnp.tile(PAST_LEN + jnp.arange(dec_seq), dec_batch)
    qs, ks, vs = _qkv_sample(xs, pos_s, common)

    def heads_padded(a):
        a = a.reshape(dec_batch, dec_seq, 3, N_HEADS, HEAD_DIM).transpose(0, 2, 3, 1, 4)
        return jnp.pad(a, ((0, 0),) * 4 + ((0, 128 - HEAD_DIM),))

    q_p = heads_padded(qs)
    new_p = jnp.stack([heads_padded(ks), heads_padded(vs)], axis=2)
    s_outs = [_attn_sample(q_p, new_p, caches[g][0].transpose(0, 2, 3, 4, 1), g) for g in range(3)]

    def rows_by_token(a):
        return a.transpose(0, 2, 1, 3).reshape(dec_batch * dec_seq, ATTN_W)

    xs = _merge_wo([rows_by_token(o) for o, _, _ in s_outs], [rows_by_token(l) for _, l, _ in s_outs],
                   xs, w_o_bf)
    kv_sample = [nc.transpose(0, 4, 1, 2, 3)[None] for _, _, nc in s_outs]

    ffn_w = [(ffn_w_gate[i].astype(BF16), ffn_w_up[i].astype(BF16), ffn_w_down[i].astype(BF16))
             for i in range(2)]
    xp = _ffn(xp, norm_ffn[0], *ffn_w[0], tm_p)
    xs = _ffn(xs, norm_ffn[0], *ffn_w[0], ROW_TILE)

    ssm_p = (ssm_a_re[0], ssm_a_im[0], ssm_log_dt[0], ssm_b_re[0], ssm_b_im[0], ssm_c_re[0], ssm_c_im[0])
    w_in_bf = ssm_w_in[0].astype(BF16)
    w_glu_bf = ssm_w_glu[0].astype(BF16)
    xp, st_p = _ssm_mixer_prompt(xp, batch, seq, norm_mix[1], w_in_bf, _ssm_matrices(*ssm_p, SSM_CHUNK),
                                 ssm_d[0], w_glu_bf, _perm_matrices(), tm_p)
    xs, st_s = _ssm_mixer_sample(xs, dec_batch, dec_seq, norm_mix[1], w_in_bf, _ssm_matrices(*ssm_p, dec_seq),
                                 ssm_d[0], w_glu_bf, state_ssm[0], ROW_TILE)
    xp = _ffn(xp, norm_ffn[1], *ffn_w[1], tm_p)
    xs = _ffn(xs, norm_ffn[1], *ffn_w[1], ROW_TILE)

    return (xp.reshape(batch, seq, D_MODEL), xs.reshape(dec_batch, dec_seq, D_MODEL),
            kv_prompt[0], kv_prompt[1], kv_prompt[2], st_p[None],
            kv_sample[0], kv_sample[1], kv_sample[2], st_s[None])
```

```python
import functools
import math

import numpy as np
import jax
import jax.numpy as jnp
from jax import lax
from jax.experimental import pallas as pl
from jax.experimental.pallas import tpu as pltpu

F32 = jnp.float32
BF16 = jnp.bfloat16

D_MODEL = 1024
HEAD_DIM = 64
N_HEADS = 8
ATTN_W = N_HEADS * HEAD_DIM
DILATIONS = (1, 4, 16)
WINDOWS = (128, 512, 2048)
KEYS_BACK = 128
PAST_LEN = 16384
ROPE_THETA = 10000.0
RMS_EPS = 1e-6
SSM_GROUP = 16
SSM_GROUPS = D_MODEL // SSM_GROUP
SSM_STATE = 64
SSM_CHUNK = 16
ROW_TILE = 256
QKV_TILE = 512
P16 = 16
NEG_BIG = -1e30
VMEM_LIMIT = 56 * 1024 * 1024


def _cparams(*sem):
    return pltpu.CompilerParams(dimension_semantics=tuple(sem), vmem_limit_bytes=VMEM_LIMIT)


def _const_spec(shape):
    nd = len(shape)
    return pl.BlockSpec(shape, lambda *_: (0,) * nd)


def _perm_matrices():
    m = np.arange(ROW_TILE)
    to_p16 = np.zeros((ROW_TILE, ROW_TILE), np.float32)
    to_p16[m, P16 * (m % P16) + m // P16] = 1.0
    return jnp.asarray(to_p16, BF16), jnp.asarray(to_p16.T, BF16)


def _rms(x, gain):
    ms = jnp.mean(x * x, axis=-1, keepdims=True)
    return x * lax.rsqrt(ms + RMS_EPS) * gain


class _QkvSlabs:
    def __init__(self, x_ref, g_ref, w_ref, perm_ref, ones_ref, qg_ref, kg_ref, cos_ref, sin_ref, permute):
        rows = x_ref.shape[0]
        h = _rms(x_ref[...], g_ref[...]).astype(BF16)
        if permute:
            h = jnp.concatenate(
                [jnp.dot(perm_ref[...], h[i:i + ROW_TILE], preferred_element_type=F32).astype(BF16)
                 for i in range(0, rows, ROW_TILE)], axis=0)
        self.h, self.w_ref, self.ones = h, w_ref, ones_ref[...]
        self.gains = (qg_ref, kg_ref)
        self.cos = jnp.concatenate([cos_ref[...]] * 4, axis=1)
        self.sin = jnp.concatenate([sin_ref[...]] * 4, axis=1)
        lane = lax.broadcasted_iota(jnp.int32, (rows, ATTN_W), 1)
        self.first_half = (lane & (HEAD_DIM - 1)) < (HEAD_DIM // 2)

    def slab(self, s, g):
        col = (s * 3 + g) * ATTN_W
        y = jnp.dot(self.h, self.w_ref[:, col:col + ATTN_W], preferred_element_type=F32)
        if s == 2:
            return y
        yy = (y * y).astype(BF16)
        ss = jnp.concatenate([jnp.dot(yy[:, :256], self.ones, preferred_element_type=F32),
                              jnp.dot(yy[:, 256:], self.ones, preferred_element_type=F32)], axis=1)
        yn = y * lax.rsqrt(ss * (1.0 / HEAD_DIM) + RMS_EPS) * self.gains[s][...]
        swapped = jnp.where(self.first_half,
                            pltpu.roll(yn, ATTN_W - HEAD_DIM // 2, 1),
                            pltpu.roll(yn, HEAD_DIM // 2, 1))
        return yn * self.cos + swapped * self.sin


def _split3_bf16(x):
    hi = x.astype(BF16)
    r1 = x - hi.astype(F32)
    mid = r1.astype(BF16)
    lo = (r1 - mid.astype(F32)).astype(BF16)
    return hi, mid, lo


def _qkv_prompt_kernel(x_ref, g_ref, w_ref, perm_ref, permt_ref, ones_ref, qg_ref, kg_ref, cos_ref, sin_ref,
                       q0_ref, k0_ref, v0_ref, q_ref, k_ref, v_ref, t0_ref, t1_ref, t2_ref, *, tail_first):
    slabs = _QkvSlabs(x_ref, g_ref, w_ref, perm_ref, ones_ref, qg_ref, kg_ref, cos_ref, sin_ref, True)
    rows = x_ref.shape[0]
    tails = (t0_ref, t1_ref, t2_ref)
    t = pl.program_id(1)
    for g in range(3):
        y = [slabs.slab(s, g) for s in range(3)]
        for s, (ref0, ref) in enumerate(((q0_ref, q_ref), (k0_ref, k_ref), (v0_ref, v_ref))):
            for i in range(rows // ROW_TILE):
                blk = y[s][i * ROW_TILE:(i + 1) * ROW_TILE].reshape(P16, P16, ATTN_W)
                if g == 0:
                    ref0[:, i * P16:(i + 1) * P16, :] = blk
                else:
                    ref[:, i * P16:(i + 1) * P16, (g - 1) * ATTN_W:g * ATTN_W] = blk.astype(BF16)

        @pl.when(t >= tail_first[g])
        def _(g=g, y=y):
            keep = tails[g].shape[-1]
            for i in range((rows - keep) // ROW_TILE, rows // ROW_TILE):
                kv = jnp.concatenate([y[1][i * ROW_TILE:(i + 1) * ROW_TILE],
                                      y[2][i * ROW_TILE:(i + 1) * ROW_TILE]], axis=1)
                nat = sum(jnp.dot(permt_ref[...], part, preferred_element_type=F32)
                          for part in _split3_bf16(kv))
                lo = max(i * ROW_TILE, rows - keep)
                width = (i + 1) * ROW_TILE - lo
                tails[g][:, :, :, lo - (rows - keep):lo - (rows - keep) + width] = (
                    nat[ROW_TILE - width:].T.reshape(2, N_HEADS, HEAD_DIM, width))


def _qkv_sample_kernel(x_ref, g_ref, w_ref, perm_ref, ones_ref, qg_ref, kg_ref, cos_ref, sin_ref,
                       q_ref, k_ref, v_ref):
    slabs = _QkvSlabs(x_ref, g_ref, w_ref, perm_ref, ones_ref, qg_ref, kg_ref, cos_ref, sin_ref, False)
    for g in range(3):
        lo, hi = g * ATTN_W, (g + 1) * ATTN_W
        for s, ref in enumerate((q_ref, k_ref, v_ref)):
            ref[:, lo:hi] = slabs.slab(s, g)


def _qkv_common_inputs(norm_g, w_qkv_bf, q_gain, k_gain):
    ones = np.kron(np.eye(4, dtype=np.float32), np.ones((HEAD_DIM, HEAD_DIM), np.float32))
    return (norm_g.reshape(1, D_MODEL), w_qkv_bf, _perm_matrices()[0], jnp.asarray(ones, BF16),
            jnp.tile(q_gain, N_HEADS).reshape(1, ATTN_W), jnp.tile(k_gain, N_HEADS).reshape(1, ATTN_W))


def _rope_tables(pos):
    half = HEAD_DIM // 2
    inv = ROPE_THETA ** (-jnp.arange(half, dtype=F32) / half)
    ang = pos.astype(F32)[:, None] * inv[None, :]
    cos, sin = jnp.cos(ang), jnp.sin(ang)
    cos = jnp.concatenate([cos, cos, cos, cos], axis=1)
    sin = jnp.concatenate([-sin, sin, -sin, sin], axis=1)
    return cos, sin


def _p16_positions(seq):
    t = np.arange(seq).reshape(seq // ROW_TILE, P16, P16)
    return jnp.asarray(t.transpose(0, 2, 1).reshape(seq))


def _qkv_prompt(x2d, batch, seq, common):
    n_tb = seq // QKV_TILE
    n_a = seq // P16
    a_blk = QKV_TILE // P16
    cos, sin = _rope_tables(_p16_positions(seq))
    out_shape = ((jax.ShapeDtypeStruct((batch, P16, n_a, ATTN_W), F32),) * 3
                 + (jax.ShapeDtypeStruct((batch, P16, n_a, 2 * ATTN_W), BF16),) * 3
                 + tuple(jax.ShapeDtypeStruct((batch, 2, N_HEADS, HEAD_DIM, w), F32) for w in WINDOWS))
    g0_spec = pl.BlockSpec((None, P16, a_blk, ATTN_W), lambda b, t: (b, 0, t, 0))
    g12_spec = pl.BlockSpec((None, P16, a_blk, 2 * ATTN_W), lambda b, t: (b, 0, t, 0))
    tail_blk = tuple(min(w, QKV_TILE) for w in WINDOWS)
    tail_first = tuple(n_tb - w // blk for w, blk in zip(WINDOWS, tail_blk))

    def tail_spec(blk, first):
        return pl.BlockSpec((None, 2, N_HEADS, HEAD_DIM, blk),
                            lambda b, t: (b, 0, 0, 0, jnp.maximum(t - first, 0)))

    norm_g, w_bf, perm, ones, qg, kg = common
    return pl.pallas_call(
        functools.partial(_qkv_prompt_kernel, tail_first=tail_first),
        grid=(batch, n_tb),
        in_specs=[pl.BlockSpec((QKV_TILE, D_MODEL), lambda b, t: (b * n_tb + t, 0)),
                  _const_spec((1, D_MODEL)),
                  pl.BlockSpec((D_MODEL, 9 * ATTN_W), lambda b, t: (0, 0), pipeline_mode=pl.Buffered(1)),
                  _const_spec((ROW_TILE, ROW_TILE)), _const_spec((ROW_TILE, ROW_TILE)), _const_spec((256, 256)),
                  _const_spec((1, ATTN_W)), _const_spec((1, ATTN_W)),
                  pl.BlockSpec((QKV_TILE, 128), lambda b, t: (t, 0)),
                  pl.BlockSpec((QKV_TILE, 128), lambda b, t: (t, 0))],
        out_specs=((g0_spec,) * 3 + (g12_spec,) * 3
                   + tuple(tail_spec(blk, first) for blk, first in zip(tail_blk, tail_first))),
        out_shape=out_shape,
        compiler_params=_cparams("arbitrary", "arbitrary"),
        name="qkv_prompt",
    )(x2d, norm_g, w_bf, perm, _perm_matrices()[1], ones, qg, kg, cos, sin)


def _qkv_sample(x2d, pos, common):
    rows = x2d.shape[0]
    cos, sin = _rope_tables(pos)
    spec = pl.BlockSpec((ROW_TILE, 3 * ATTN_W), lambda i: (i, 0))
    return pl.pallas_call(
        _qkv_sample_kernel,
        grid=(rows // ROW_TILE,),
        in_specs=[pl.BlockSpec((ROW_TILE, D_MODEL), lambda i: (i, 0)),
                  _const_spec((1, D_MODEL)), _const_spec((D_MODEL, 9 * ATTN_W)),
                  _const_spec((ROW_TILE, ROW_TILE)), _const_spec((256, 256)),
                  _const_spec((1, ATTN_W)), _const_spec((1, ATTN_W)),
                  pl.BlockSpec((ROW_TILE, 128), lambda i: (i, 0)),
                  pl.BlockSpec((ROW_TILE, 128), lambda i: (i, 0))],
        out_specs=(spec,) * 3,
        out_shape=(jax.ShapeDtypeStruct((rows, 3 * ATTN_W), F32),) * 3,
        compiler_params=_cparams("arbitrary"),
        name="qkv_sample",
    )(x2d, *common, cos, sin)


ATTN_Q = 128


def _attn_geometry(dil):
    nres = P16 // dil
    qa = ATTN_Q // nres
    return nres, qa


def _attn_bias_tables(dil):
    nres, qa = _attn_geometry(dil)
    m = np.arange(ATTN_Q)
    n = np.arange(2 * ATTN_Q)
    qoff = nres * (m % qa) + m // qa
    koff = nres * (n % (2 * qa)) + n // (2 * qa)
    tables = []
    for delta in (0, qa):
        dist = qoff[:, None] - koff[None, :] + nres * delta
        bias = np.where((dist >= 0) & (dist <= KEYS_BACK), 0.0, NEG_BIG).astype(np.float32)
        tables.append(np.concatenate([bias, bias], axis=0))
    return jnp.asarray(np.stack(tables))


def _attn_group(bias_ref, q_ref, k_ref, v_ref, o_ref, l_ref, dil, n_a):
    nres, qa = _attn_geometry(dil)
    n_atiles = n_a // qa
    head1 = lax.broadcasted_iota(jnp.int32, (ATTN_Q, 128), 1) >= HEAD_DIM
    zeros = jnp.zeros((ATTN_Q, 128), BF16)

    def tile(t, carry):
        c = t // n_atiles
        at = t % n_atiles
        a0 = pl.multiple_of(at * qa, qa)
        k0 = pl.multiple_of(jnp.maximum(at * qa - qa, 0), qa)
        rows = [c + dil * b for b in range(nres)]
        q = jnp.concatenate([q_ref[r, pl.ds(a0, qa), :] for r in rows], axis=0).astype(BF16)
        k = jnp.concatenate([k_ref[r, pl.ds(k0, 2 * qa), :] for r in rows], axis=0).astype(BF16)
        v = jnp.concatenate([v_ref[r, pl.ds(k0, 2 * qa), :] for r in rows], axis=0).astype(BF16)
        q = q * jnp.asarray(HEAD_DIM ** -0.5, BF16)
        q2 = jnp.concatenate([jnp.where(head1, zeros, q), jnp.where(head1, q, zeros)], axis=0)
        s = lax.dot_general(q2, k, (((1,), (1,)), ((), ())), preferred_element_type=F32)
        s = s + bias_ref[jnp.minimum(at, 1)]
        mx = jnp.max(s, axis=-1, keepdims=True)
        p = jnp.exp(s - mx)
        den = jnp.sum(p, axis=-1, keepdims=True)
        o2 = jnp.dot(p.astype(BF16), v, preferred_element_type=F32) / den
        lse2 = mx + jnp.log(den)
        o = jnp.where(head1, o2[ATTN_Q:], o2[:ATTN_Q])
        lse = jnp.where(head1, lse2[ATTN_Q:], lse2[:ATTN_Q])
        for b, r in enumerate(rows):
            o_ref[r, pl.ds(a0, qa), :] = o[b * qa:(b + 1) * qa]
            l_ref[r, pl.ds(a0, qa), :] = lse[b * qa:(b + 1) * qa]
        return carry

    lax.fori_loop(0, dil * n_atiles, tile, 0, unroll=4)


def _attn_prompt_kernel(b0, b1, b2, q0, k0, v0, q1, k1, v1, q2, k2, v2, o_ref, og_ref, lg_ref, *, n_a):
    groups = ((b0, q0, k0, v0), (b1, q1, k1, v1), (b2, q2, k2, v2))
    for g, (bias_ref, q_ref, k_ref, v_ref) in enumerate(groups):
        _attn_group(bias_ref, q_ref, k_ref, v_ref, og_ref.at[g], lg_ref.at[g], DILATIONS[g], n_a)

    def merge(r, carry):
        ls = [lg_ref[g, r] for g in range(3)]
        mx = jnp.maximum(jnp.maximum(ls[0], ls[1]), ls[2])
        es = [jnp.exp(l - mx) for l in ls]
        num = es[0] * og_ref[0, r] + es[1] * og_ref[1, r] + es[2] * og_ref[2, r]
        o_ref[r] = (num / (es[0] + es[1] + es[2])).astype(BF16)
        return carry

    lax.fori_loop(0, P16, merge, 0)


def _attn_prompt(qkv0, qkv12):
    batch, _, n_a, _ = qkv0[0].shape
    biases = [_attn_bias_tables(dil) for dil in DILATIONS]
    in_specs = [_const_spec(b.shape) for b in biases]
    operands = list(biases)
    for g in range(3):
        in_specs += [pl.BlockSpec((None, P16, n_a, 128), lambda b, h, g=g: (b, 0, 0, max(g - 1, 0) * 4 + h))] * 3
        operands += list(qkv0 if g == 0 else qkv12)
    return pl.pallas_call(
        functools.partial(_attn_prompt_kernel, n_a=n_a),
        grid=(batch, 4),
        in_specs=in_specs,
        out_specs=pl.BlockSpec((None, P16, n_a, 128), lambda b, h: (b, 0, 0, h)),
        out_shape=jax.ShapeDtypeStruct((batch, P16, n_a, ATTN_W), BF16),
        scratch_shapes=[pltpu.VMEM((3, P16, n_a, 128), F32), pltpu.VMEM((3, P16, n_a, 128), F32)],
        compiler_params=_cparams("arbitrary", "arbitrary"),
        name="attn_prompt",
    )(*operands)


def _attn_sample_kernel(q_ref, n_ref, c_ref, o_ref, l_ref, nc_ref, *, dil, win):
    n_new = q_ref.shape[1]
    nt = (((1,), (1,)), ((), ()))
    dist_c = (win + lax.broadcasted_iota(jnp.int32, (n_new, win), 0)
              - lax.broadcasted_iota(jnp.int32, (n_new, win), 1))
    ok_c = (dist_c <= win) & ((dist_c & (dil - 1)) == 0)
    dist_n = (lax.broadcasted_iota(jnp.int32, (n_new, n_new), 0)
              - lax.broadcasted_iota(jnp.int32, (n_new, n_new), 1))
    ok_n = (dist_n >= 0) & ((dist_n & (dil - 1)) == 0)
    lane = lax.broadcasted_iota(jnp.int32, (HEAD_DIM, 128), 1)
    pad_rows = jnp.zeros((128 - n_new, 128), F32)
    for h in range(N_HEADS):
        q = q_ref[h] * (HEAD_DIM ** -0.5)
        kt, vt = c_ref[0, h], c_ref[1, h]
        s_c = jnp.dot(q[:, :HEAD_DIM].astype(BF16), kt.astype(BF16), preferred_element_type=F32)
        s_n = lax.dot_general(q, n_ref[0, h], nt, preferred_element_type=F32)
        s_c = jnp.where(ok_c, s_c, NEG_BIG)
        s_n = jnp.where(ok_n, s_n, NEG_BIG)
        mx = jnp.maximum(jnp.max(s_c, axis=-1, keepdims=True), jnp.max(s_n, axis=-1, keepdims=True))
        p_c = jnp.exp(s_c - mx)
        p_n = jnp.exp(s_n - mx)
        den = jnp.sum(p_c, axis=-1, keepdims=True) + jnp.sum(p_n, axis=-1, keepdims=True)
        o = (lax.dot_general(p_c.astype(BF16), vt.astype(BF16), nt, preferred_element_type=F32)
             + jnp.dot(p_n, n_ref[1, h], preferred_element_type=F32)[:, :HEAD_DIM])
        o_ref[h] = o / den
        l_ref[h] = jnp.broadcast_to(mx + jnp.log(den), (n_new, HEAD_DIM))
        for kv, old in enumerate((kt, vt)):
            rolled = pltpu.roll(old, win - n_new, 1)
            new_t = jnp.concatenate([pad_rows, n_ref[kv, h]], axis=0).T[:HEAD_DIM]
            if win > 128:
                nc_ref[kv, h, :, :win - 128] = rolled[:, :win - 128]
            nc_ref[kv, h, :, win - 128:] = jnp.where(lane < 128 - n_new, rolled[:, win - 128:], new_t)


def _attn_sample(q_p, new_p, cache_t, g):
    batch, _, _, n_new, _ = q_p.shape
    win = cache_t.shape[-1]
    dil = DILATIONS[g]
    c_spec = pl.BlockSpec((None, 2, N_HEADS, HEAD_DIM, win), lambda b: (b, 0, 0, 0, 0))
    out_spec = pl.BlockSpec((None, N_HEADS, n_new, HEAD_DIM), lambda b: (b, 0, 0, 0))
    out_shape = jax.ShapeDtypeStruct((batch, N_HEADS, n_new, HEAD_DIM), F32)
    return pl.pallas_call(
        functools.partial(_attn_sample_kernel, dil=dil, win=win),
        grid=(batch,),
        in_specs=[pl.BlockSpec((None, None, N_HEADS, n_new, 128), lambda b: (b, g, 0, 0, 0)),
                  pl.BlockSpec((None, None, 2, N_HEADS, n_new, 128), lambda b: (b, g, 0, 0, 0, 0)),
                  c_spec],
        out_specs=(out_spec, out_spec, c_spec),
        out_shape=(out_shape, out_shape, jax.ShapeDtypeStruct(cache_t.shape, F32)),
        compiler_params=_cparams("arbitrary"),
        name=f"attn_sample_d{dil}",
    )(q_p, new_p, cache_t)


def _merge_wo_kernel(o0, o1, o2, l0, l1, l2, x_ref, w_ref, y_ref):
    os_ = [r[...] for r in (o0, o1, o2)]
    ls = [r[...] for r in (l0, l1, l2)]
    mx = jnp.maximum(jnp.maximum(ls[0], ls[1]), ls[2])
    es = [jnp.exp(l - mx) for l in ls]
    den = es[0] + es[1] + es[2]
    o = ((es[0] * os_[0] + es[1] * os_[1] + es[2] * os_[2]) / den).astype(BF16)
    y_ref[...] = x_ref[...] + jnp.dot(o, w_ref[...], preferred_element_type=F32)


def _merge_wo(os_, ls, x2d, w_o_bf):
    rows = x2d.shape[0]
    x_spec = pl.BlockSpec((ROW_TILE, D_MODEL), lambda i: (i, 0))
    a_spec = pl.BlockSpec((ROW_TILE, ATTN_W), lambda i: (i, 0))
    return pl.pallas_call(
        _merge_wo_kernel,
        grid=(rows // ROW_TILE,),
        in_specs=[a_spec] * 6 + [x_spec, _const_spec((ATTN_W, D_MODEL))],
        out_specs=x_spec,
        out_shape=jax.ShapeDtypeStruct((rows, D_MODEL), F32),
        compiler_params=_cparams("arbitrary"),
        name="merge_wo",
    )(*os_, *ls, x2d, w_o_bf)


def _wo_p16_kernel(o_ref, x_ref, w_ref, permt_ref, y_ref):
    blocks = []
    for i in range(QKV_TILE // ROW_TILE):
        o = o_ref[:, i * P16:(i + 1) * P16, :].astype(F32).reshape(ROW_TILE, ATTN_W).astype(BF16)
        blocks.append(jnp.dot(permt_ref[...], o, preferred_element_type=F32).astype(BF16))
    o = jnp.concatenate(blocks, axis=0)
    y_ref[...] = x_ref[...] + jnp.dot(o, w_ref[...], preferred_element_type=F32)


def _wo_p16(o_p16, x2d, w_o_bf, seq):
    rows = x2d.shape[0]
    n_tb = seq // QKV_TILE
    x_spec = pl.BlockSpec((QKV_TILE, D_MODEL), lambda i: (i, 0))
    return pl.pallas_call(
        _wo_p16_kernel,
        grid=(rows // QKV_TILE,),
        in_specs=[pl.BlockSpec((None, P16, QKV_TILE // P16, ATTN_W), lambda i: (i // n_tb, 0, i % n_tb, 0)),
                  x_spec, _const_spec((ATTN_W, D_MODEL)), _const_spec((ROW_TILE, ROW_TILE))],
        out_specs=x_spec,
        out_shape=jax.ShapeDtypeStruct((rows, D_MODEL), F32),
        compiler_params=_cparams("arbitrary"),
        name="wo_p16",
    )(o_p16, x2d, w_o_bf, _perm_matrices()[1])


def _ffn_kernel(x_ref, g_ref, wg_ref, wu_ref, wd_ref, y_ref, *, chunks):
    x = x_ref[...]
    h = _rms(x, g_ref[...]).astype(BF16)
    acc = x
    for lo, hi in chunks:
        gate = jnp.dot(h, wg_ref[:, lo:hi], preferred_element_type=F32)
        up = jnp.dot(h, wu_ref[:, lo:hi], preferred_element_type=F32)
        act = (jax.nn.silu(gate) * up).astype(BF16)
        acc = acc + jnp.dot(act, wd_ref[lo:hi, :], preferred_element_type=F32)
    y_ref[...] = acc


def _ffn(x2d, norm_g, wg_bf, wu_bf, wd_bf, tm):
    rows = x2d.shape[0]
    d_ff = wg_bf.shape[1]
    step = 1024
    chunks = tuple((lo, min(lo + step, d_ff)) for lo in range(0, d_ff, step))
    x_spec = pl.BlockSpec((tm, D_MODEL), lambda i: (i, 0))
    w_in = pl.BlockSpec((D_MODEL, d_ff), lambda i: (0, 0), pipeline_mode=pl.Buffered(1))
    w_out = pl.BlockSpec((d_ff, D_MODEL), lambda i: (0, 0), pipeline_mode=pl.Buffered(1))
    return pl.pallas_call(
        functools.partial(_ffn_kernel, chunks=chunks),
        grid=(rows // tm,),
        in_specs=[x_spec, _const_spec((1, D_MODEL)), w_in, w_in, w_out],
        out_specs=x_spec,
        out_shape=jax.ShapeDtypeStruct((rows, D_MODEL), F32),
        compiler_params=_cparams("arbitrary"),
        name="ffn",
    )(x2d, norm_g.reshape(1, D_MODEL), wg_bf, wu_bf, wd_bf)


def _norm_matmul_kernel(x_ref, g_ref, w_ref, y_ref):
    h = _rms(x_ref[...], g_ref[...]).astype(BF16)
    y_ref[...] = jnp.dot(h, w_ref[...], preferred_element_type=F32)


def _norm_matmul(x2d, norm_g, w_bf, tm):
    rows = x2d.shape[0]
    n = w_bf.shape[1]
    return pl.pallas_call(
        _norm_matmul_kernel,
        grid=(rows // tm,),
        in_specs=[pl.BlockSpec((tm, D_MODEL), lambda i: (i, 0)), _const_spec((1, D_MODEL)),
                  _const_spec((D_MODEL, n))],
        out_specs=pl.BlockSpec((tm, n), lambda i: (i, 0)),
        out_shape=jax.ShapeDtypeStruct((rows, n), F32),
        compiler_params=_cparams("arbitrary"),
        name="ssm_in_proj",
    )(x2d, norm_g.reshape(1, D_MODEL), w_bf)


SSM_LANE_TILE = 128
SSM_TOK_TILE = SSM_LANE_TILE * SSM_CHUNK


def _ssm_in_kernel(x_ref, g_ref, w_ref, perm_ref, ut_ref):
    n_blk = SSM_TOK_TILE // ROW_TILE
    blocks = []
    for kb in range(n_blk):
        h = _rms(x_ref[kb * ROW_TILE:(kb + 1) * ROW_TILE, :], g_ref[...]).astype(BF16)
        blocks.append(jnp.dot(perm_ref[...], h, preferred_element_type=F32).astype(BF16))
    hp = jnp.concatenate([blk[j * P16:(j + 1) * P16] for j in range(SSM_CHUNK) for blk in blocks], axis=0)
    u = jnp.dot(hp, w_ref[...], preferred_element_type=F32)
    for j in range(SSM_CHUNK):
        ut_ref[j] = u[j * SSM_LANE_TILE:(j + 1) * SSM_LANE_TILE].T


def _ssm_in(x2d, norm_g, w_bf, perm):
    rows = x2d.shape[0]
    n_tiles = rows // SSM_TOK_TILE
    return pl.pallas_call(
        _ssm_in_kernel,
        grid=(n_tiles,),
        in_specs=[pl.BlockSpec((SSM_TOK_TILE, D_MODEL), lambda i: (i, 0)), _const_spec((1, D_MODEL)),
                  pl.BlockSpec((D_MODEL, D_MODEL), lambda i: (0, 0), pipeline_mode=pl.Buffered(1)),
                  _const_spec((ROW_TILE, ROW_TILE))],
        out_specs=pl.BlockSpec((SSM_CHUNK, D_MODEL, SSM_LANE_TILE), lambda i: (0, 0, i)),
        out_shape=jax.ShapeDtypeStruct((SSM_CHUNK, D_MODEL, n_tiles * SSM_LANE_TILE), F32),
        compiler_params=_cparams("arbitrary"),
        name="ssm_in_t",
    )(x2d, norm_g.reshape(1, D_MODEL), w_bf, perm)


def _ssm_gate_kernel(yt_ref, ut_ref, d_ref, permt_ref, g_ref):
    n_blk = SSM_TOK_TILE // ROW_TILE
    pieces = []
    for i in range(SSM_CHUNK):
        yy = yt_ref[i] + d_ref[...] * ut_ref[i]
        pieces.append(jax.nn.gelu(yy).T.astype(BF16))
    for kb in range(n_blk):
        blk = jnp.concatenate([p[kb * P16:(kb + 1) * P16] for p in pieces], axis=0)
        g_ref[kb * ROW_TILE:(kb + 1) * ROW_TILE, :] = jnp.dot(
            permt_ref[...], blk, preferred_element_type=F32).astype(BF16)


def _ssm_gate(yt, ut, d_skip, permt):
    n_tiles = yt.shape[2] // SSM_LANE_TILE
    half = D_MODEL // 2
    t_spec = pl.BlockSpec((SSM_CHUNK, half, SSM_LANE_TILE), lambda i, c: (0, c, i))
    d_b = jnp.broadcast_to(d_skip[:, None], (D_MODEL, SSM_LANE_TILE))
    return pl.pallas_call(
        _ssm_gate_kernel,
        grid=(n_tiles, 2),
        in_specs=[t_spec, t_spec, pl.BlockSpec((half, SSM_LANE_TILE), lambda i, c: (c, 0)),
                  _const_spec((ROW_TILE, ROW_TILE))],
        out_specs=pl.BlockSpec((SSM_TOK_TILE, half), lambda i, c: (i, c)),
        out_shape=jax.ShapeDtypeStruct((n_tiles * SSM_TOK_TILE, D_MODEL), BF16),
        compiler_params=_cparams("arbitrary", "arbitrary"),
        name="ssm_gate_t",
    )(yt, ut, d_b, permt)


def _glu_act_kernel(g_ref, x_ref, w_ref, o_ref):
    z = jnp.dot(g_ref[...], w_ref[...], preferred_element_type=F32)
    o_ref[...] = x_ref[...] + z[:, :D_MODEL] * jax.nn.sigmoid(z[:, D_MODEL:])


def _glu_act(g2d, x2d, w_bf, tm):
    rows = x2d.shape[0]
    spec = pl.BlockSpec((tm, D_MODEL), lambda i: (i, 0))
    return pl.pallas_call(
        _glu_act_kernel,
        grid=(rows // tm,),
        in_specs=[spec, spec, _const_spec((D_MODEL, 2 * D_MODEL))],
        out_specs=spec,
        out_shape=jax.ShapeDtypeStruct((rows, D_MODEL), F32),
        compiler_params=_cparams("arbitrary"),
        name="ssm_glu_act",
    )(g2d, x2d, w_bf)


def _ssm_core_kernel(u_ref, mt_ref, ft_ref, et_ref, a_ref, x0_ref, y_ref, xf_ref, *, n_chunks, has_init):
    chunk, _, n_lanes = u_ref.shape
    u = u_ref[...].reshape(chunk * SSM_GROUP, n_lanes).astype(BF16)
    y = jnp.dot(mt_ref[...], u, preferred_element_type=F32)
    s = jnp.dot(ft_ref[...], u, preferred_element_type=F32)
    sre, sim = s[:SSM_STATE], s[SSM_STATE:]
    are, aim = a_ref[0], a_ref[1]
    if n_chunks == 1:
        x0 = x0_ref[...]
        xre, xim = x0[:SSM_STATE], x0[SSM_STATE:]
        xf_ref[:SSM_STATE, :] = are * xre - aim * xim + sre
        xf_ref[SSM_STATE:, :] = are * xim + aim * xre + sim
        y = y + jnp.dot(et_ref[...], x0.astype(BF16), preferred_element_type=F32)
    else:
        assert not has_init and n_chunks % 128 == 0
        reps = n_lanes // 128
        pos = lax.broadcasted_iota(jnp.int32, (SSM_STATE, n_lanes), 1) & (n_chunks - 1)
        shift = 1
        while shift < n_chunks:
            keep = pos >= shift
            tre = jnp.where(keep, pltpu.roll(sre, shift, 1), 0.0)
            tim = jnp.where(keep, pltpu.roll(sim, shift, 1), 0.0)
            bre = jnp.concatenate([are] * reps, axis=1)
            bim = jnp.concatenate([aim] * reps, axis=1)
            sre, sim = sre + bre * tre - bim * tim, sim + bre * tim + bim * tre
            are, aim = are * are - aim * aim, 2.0 * are * aim
            shift *= 2
        for b in range(n_lanes // n_chunks):
            lo, hi = (b + 1) * n_chunks - 128, (b + 1) * n_chunks
            xf_ref[:SSM_STATE, b * 128:(b + 1) * 128] = sre[:, lo:hi]
            xf_ref[SSM_STATE:, b * 128:(b + 1) * 128] = sim[:, lo:hi]
        keep = pos >= 1
        xin = jnp.concatenate([jnp.where(keep, pltpu.roll(sre, 1, 1), 0.0),
                               jnp.where(keep, pltpu.roll(sim, 1, 1), 0.0)], axis=0)
        y = y + jnp.dot(et_ref[...], xin.astype(BF16), preferred_element_type=F32)
    y_ref[...] = y.reshape(chunk, SSM_GROUP, n_lanes)


def _ssm_core(ut, mats, x0t, n_chunks):
    mt, ft, et, a_pow = mats
    chunk, _, n_lanes = ut.shape
    groups = SSM_GROUPS
    has_init = x0t is not None
    a_lanes = n_lanes if n_chunks == 1 else 128
    a_b = jnp.broadcast_to(a_pow[:, :, :, None], a_pow.shape + (a_lanes,))
    if x0t is None:
        x0t = jnp.zeros((groups, 2 * SSM_STATE, 128), F32)
    xf_lanes = n_lanes if n_chunks == 1 else (n_lanes // n_chunks) * 128

    def gspec(shape):
        return pl.BlockSpec((None,) + tuple(shape[1:]), lambda g: (g,) + (0,) * (len(shape) - 1))

    t_spec = pl.BlockSpec((chunk, SSM_GROUP, n_lanes), lambda g: (0, g, 0))
    return pl.pallas_call(
        functools.partial(_ssm_core_kernel, n_chunks=n_chunks, has_init=has_init),
        grid=(groups,),
        in_specs=[t_spec, gspec(mt.shape), gspec(ft.shape), gspec(et.shape), gspec(a_b.shape),
                  gspec(x0t.shape)],
        out_specs=(t_spec, gspec((groups, 2 * SSM_STATE, xf_lanes))),
        out_shape=(jax.ShapeDtypeStruct((chunk, D_MODEL, n_lanes), F32),
                   jax.ShapeDtypeStruct((groups, 2 * SSM_STATE, xf_lanes), F32)),
        compiler_params=_cparams("arbitrary"),
        name=f"ssm_core_{n_chunks}",
    )(ut, mt, ft, et, a_b, x0t)


def _cexp(mag_arg, ang):
    mag = jnp.exp(mag_arg)
    return mag * jnp.cos(ang), mag * jnp.sin(ang)


def _ssm_matrices_kernel(ar_row, ai_row, ar_col, ai_col, ld_ref, bre_ref, bim_ref, cre_ref, cim_ref,
                         mt_ref, ft_ref, et_ref, ap_ref, *, chunk):
    hp = lax.Precision.HIGHEST
    rows = SSM_GROUP * chunk
    dt = jnp.exp(ld_ref[...])
    lr, li = ar_row[...] * dt, ai_row[...] * dt
    step = (lax.broadcasted_iota(jnp.int32, (rows, 128), 0) >> 4).astype(F32)
    cre, cim = cre_ref[...], cim_ref[...]
    pr, pi = _cexp(step * lr, step * li)
    r0_re, r0_im = (cre * pr - cim * pi)[:, :SSM_STATE], (cre * pi + cim * pr)[:, :SSM_STATE]
    pr, pi = _cexp((step + 1.0) * lr, (step + 1.0) * li)
    lane = lax.broadcasted_iota(jnp.int32, (rows, 128), 1)
    et_ref[...] = jnp.where(lane < SSM_STATE, cre * pr - cim * pi, -(cre * pi + cim * pr)).astype(BF16)
    pr, pi = _cexp(chunk * lr, chunk * li)
    ap_ref[0:1, :] = pr
    ap_ref[1:2, :] = pi
    ar, ai = ar_col[...], ai_col[...]
    abr, abi = _cexp(ar * dt, ai * dt)
    inv = 1.0 / (ar * ar + ai * ai)
    zr = ((abr - 1.0) * ar + abi * ai) * inv
    zi = (abi * ar - (abr - 1.0) * ai) * inv
    bre, bim = bre_ref[...], bim_ref[...]
    bbr, bbi = zr * bre - zi * bim, zr * bim + zi * bre
    back = (chunk - 1 - (lax.broadcasted_iota(jnp.int32, (SSM_STATE, rows), 1) >> 4)).astype(F32)
    pr, pi = _cexp(back * (ar * dt), back * (ai * dt))
    ft_ref[:SSM_STATE, :] = (pr * bbr - pi * bbi).astype(BF16)
    ft_ref[SSM_STATE:, :] = (pr * bbi + pi * bbr).astype(BF16)
    tw = (jnp.dot(r0_re, bbr, precision=hp, preferred_element_type=F32)
          - jnp.dot(r0_im, bbi, precision=hp, preferred_element_type=F32))
    row_blk = lax.broadcasted_iota(jnp.int32, (rows, rows), 0) >> 4
    col_blk = lax.broadcasted_iota(jnp.int32, (rows, rows), 1) >> 4
    mt = jnp.zeros((rows, rows), F32)
    for j in range(chunk):
        shifted = tw if j == 0 else pltpu.roll(tw, SSM_GROUP * j, 0)
        mt = jnp.where((col_blk == j) & (row_blk >= j), shifted, mt)
    mt_ref[...] = mt.astype(BF16)


def _ssm_matrices(a_re, a_im, log_dt, b_re, b_im, c_re, c_im, chunk):
    groups, rows = SSM_GROUPS, SSM_GROUP * chunk

    def gspec(*shape):
        return pl.BlockSpec((None,) + shape, lambda g: (g,) + (0,) * len(shape))

    row2 = lambda a: jnp.tile(a, (1, 2)).reshape(groups, 1, 2 * SSM_STATE)
    col = lambda a: a.reshape(groups, SSM_STATE, 1)
    b_t = lambda b: jnp.tile(b, (1, 1, chunk))
    c_t = lambda c: jnp.tile(c, (1, chunk, 2))
    mt, ft, et, a_pow = pl.pallas_call(
        functools.partial(_ssm_matrices_kernel, chunk=chunk),
        grid=(groups,),
        in_specs=[gspec(1, 128), gspec(1, 128), gspec(SSM_STATE, 1), gspec(SSM_STATE, 1), gspec(1, 1),
                  gspec(SSM_STATE, rows), gspec(SSM_STATE, rows), gspec(rows, 128), gspec(rows, 128)],
        out_specs=(gspec(rows, rows), gspec(2 * SSM_STATE, rows), gspec(rows, 2 * SSM_STATE), gspec(2, 128)),
        out_shape=(jax.ShapeDtypeStruct((groups, rows, rows), BF16),
                   jax.ShapeDtypeStruct((groups, 2 * SSM_STATE, rows), BF16),
                   jax.ShapeDtypeStruct((groups, rows, 2 * SSM_STATE), BF16),
                   jax.ShapeDtypeStruct((groups, 2, 128), F32)),
        compiler_params=_cparams("arbitrary"),
        name=f"ssm_matrices_{chunk}",
    )(row2(a_re), row2(a_im), col(a_re), col(a_im), log_dt.reshape(groups, 1, 1),
      b_t(b_re), b_t(b_im), c_t(c_re), c_t(c_im))
    return mt, ft, et, a_pow[:, :, :SSM_STATE]


def _glu_kernel(y_ref, u_ref, d_ref, x_ref, w_ref, o_ref):
    yy = y_ref[...] + d_ref[...] * u_ref[...]
    z = jnp.dot(jax.nn.gelu(yy).astype(BF16), w_ref[...], preferred_element_type=F32)
    o_ref[...] = x_ref[...] + z[:, :D_MODEL] * jax.nn.sigmoid(z[:, D_MODEL:])


def _glu(y2d, u2d, d_skip, x2d, w_bf, tm):
    rows = x2d.shape[0]
    spec = pl.BlockSpec((tm, D_MODEL), lambda i: (i, 0))
    return pl.pallas_call(
        _glu_kernel,
        grid=(rows // tm,),
        in_specs=[spec, spec, _const_spec((1, D_MODEL)), spec, _const_spec((D_MODEL, 2 * D_MODEL))],
        out_specs=spec,
        out_shape=jax.ShapeDtypeStruct((rows, D_MODEL), F32),
        compiler_params=_cparams("arbitrary"),
        name="ssm_glu",
    )(y2d, u2d, d_skip.reshape(1, D_MODEL), x2d, w_bf)


def _ssm_mixer_prompt(x2d, batch, seq, norm_g, w_in_bf, mats, d_skip, w_glu_bf, perms, tm):
    assert seq % SSM_TOK_TILE == 0
    n_chunks = seq // SSM_CHUNK
    ut = _ssm_in(x2d, norm_g, w_in_bf, perms[0])
    yt, xf = _ssm_core(ut, mats, None, n_chunks)
    g2d = _ssm_gate(yt, ut, d_skip, perms[1])
    xf = xf.reshape(SSM_GROUPS, 2, SSM_STATE, batch, 128)[..., 127]
    return _glu_act(g2d, x2d, w_glu_bf, tm), xf.transpose(3, 0, 2, 1)


def _ssm_mixer_sample(x2d, batch, seq, norm_g, w_in_bf, mats, d_skip, w_glu_bf, state0, tm):
    u = _norm_matmul(x2d, norm_g, w_in_bf, tm)
    ut = u.reshape(batch, seq, D_MODEL).transpose(1, 2, 0)
    x0t = state0.transpose(1, 3, 2, 0).reshape(SSM_GROUPS, 2 * SSM_STATE, batch)
    yt, xf = _ssm_core(ut, mats, x0t, 1)
    y = yt.transpose(2, 0, 1).reshape(batch * seq, D_MODEL)
    state = xf.reshape(SSM_GROUPS, 2, SSM_STATE, batch).transpose(3, 0, 2, 1)
    return _glu(y, u, d_skip, x2d, w_glu_bf, tm), state


def kernel(x_prompt, x_sample, cache_kv_w128, cache_kv_w512, cache_kv_w2048, state_ssm, norm_mix, norm_ffn,
           w_qkv, q_norm, k_norm, w_o, ssm_w_in, ssm_a_re, ssm_a_im, ssm_log_dt, ssm_b_re, ssm_b_im,
           ssm_c_re, ssm_c_im, ssm_d, ssm_w_glu, ffn_w_gate, ffn_w_up, ffn_w_down):
    batch, seq, _ = x_prompt.shape
    dec_batch, dec_seq, _ = x_sample.shape
    caches = (cache_kv_w128, cache_kv_w512, cache_kv_w2048)
    assert seq % ROW_TILE == 0 and seq >= WINDOWS[-1] and (dec_batch * dec_seq) % ROW_TILE == 0
    assert all(c.shape[2] == w for c, w in zip(caches, WINDOWS))
    n_a = seq // P16
    tm_p = 512

    xp = x_prompt.reshape(batch * seq, D_MODEL)
    xs = x_sample.reshape(dec_batch * dec_seq, D_MODEL)

    common = _qkv_common_inputs(norm_mix[0], w_qkv[0].astype(BF16), q_norm[0], k_norm[0])
    w_o_bf = w_o[0].astype(BF16)
    q0, k0, v0, q12, k12, v12, t0, t1, t2 = _qkv_prompt(xp, batch, seq, common)
    xp = _wo_p16(_attn_prompt((q0, k0, v0), (q12, k12, v12)), xp, w_o_bf, seq)
    kv_prompt = [tail.transpose(0, 4, 1, 2, 3)[None] for tail in (t0, t1, t2)]

    pos_s = jnp.tile(PAST_LEN + jnp.arange(dec_seq), dec_batch)
    qs, ks, vs = _qkv_sample(xs, pos_s, common)

    def heads_padded(a):
        a = a.reshape(dec_batch, dec_seq, 3, N_HEADS, HEAD_DIM).transpose(0, 2, 3, 1, 4)
        return jnp.pad(a, ((0, 0),) * 4 + ((0, 128 - HEAD_DIM),))

    q_p = heads_padded(qs)
    new_p = jnp.stack([heads_padded(ks), heads_padded(vs)], axis=2)
    s_outs = [_attn_sample(q_p, new_p, caches[g][0].transpose(0, 2, 3, 4, 1), g) for g in range(3)]

    def rows_by_token(a):
        return a.transpose(0, 2, 1, 3).reshape(dec_batch * dec_seq, ATTN_W)

    xs = _merge_wo([rows_by_token(o) for o, _, _ in s_outs], [rows_by_token(l) for _, l, _ in s_outs],
                   xs, w_o_bf)
    kv_sample = [nc.transpose(0, 4, 1, 2, 3)[None] for _, _, nc in s_outs]

    ffn_w = [(ffn_w_gate[i].astype(BF16), ffn_w_up[i].astype(BF16), ffn_w_down[i].astype(BF16))
             for i in range(2)]
    xp = _ffn(xp, norm_ffn[0], *ffn_w[0], tm_p)
    xs = _ffn(xs, norm_ffn[0], *ffn_w[0], ROW_TILE)

    ssm_p = (ssm_a_re[0], ssm_a_im[0], ssm_log_dt[0], ssm_b_re[0], ssm_b_im[0], ssm_c_re[0], ssm_c_im[0])
    w_in_bf = ssm_w_in[0].astype(BF16)
    w_glu_bf = ssm_w_glu[0].astype(BF16)
    xp, st_p = _ssm_mixer_prompt(xp, batch, seq, norm_mix[1], w_in_bf, _ssm_matrices(*ssm_p, SSM_CHUNK),
                                 ssm_d[0], w_glu_bf, _perm_matrices(), tm_p)
    xs, st_s = _ssm_mixer_sample(xs, dec_batch, dec_seq, norm_mix[1], w_in_bf, _ssm_matrices(*ssm_p, dec_seq),
                                 ssm_d[0], w_glu_bf, state_ssm[0], ROW_TILE)
    xp = _ffn(xp, norm_ffn[1], *ffn_w[1], tm_p)
    xs = _ffn(xs, norm_ffn[1], *ffn_w[1], ROW_TILE)

    return (xp.reshape(batch, seq, D_MODEL), xs.reshape(dec_batch, dec_seq, D_MODEL),
            kv_prompt[0], kv_prompt[1], kv_prompt[2], st_p[None],
            kv_sample[0], kv_sample[1], kv_sample[2], st_s[None])
```

```python
import functools
import math

import numpy as np
import jax
import jax.numpy as jnp
from jax import lax
from jax.experimental import pallas as pl
from jax.experimental.pallas import tpu as pltpu

F32 = jnp.float32
BF16 = jnp.bfloat16

D_MODEL = 1024
HEAD_DIM = 64
N_HEADS = 8
ATTN_W = N_HEADS * HEAD_DIM
DILATIONS = (1, 4, 16)
WINDOWS = (128, 512, 2048)
KEYS_BACK = 128
PAST_LEN = 16384
ROPE_THETA = 10000.0
RMS_EPS = 1e-6
SSM_GROUP = 16
SSM_GROUPS = D_MODEL // SSM_GROUP
SSM_STATE = 64
SSM_CHUNK = 16
ROW_TILE = 256
QKV_TILE = 512
FFN_TILE = 1024
P16 = 16
NEG_BIG = -1e30
VMEM_LIMIT = 56 * 1024 * 1024


def _cparams(*sem):
    return pltpu.CompilerParams(dimension_semantics=tuple(sem), vmem_limit_bytes=VMEM_LIMIT)


def _const_spec(shape):
    nd = len(shape)
    return pl.BlockSpec(shape, lambda *_: (0,) * nd)


def _perm_matrices():
    m = np.arange(ROW_TILE)
    to_p16 = np.zeros((ROW_TILE, ROW_TILE), np.float32)
    to_p16[m, P16 * (m % P16) + m // P16] = 1.0
    return jnp.asarray(to_p16, BF16), jnp.asarray(to_p16.T, BF16)


def _rms(x, gain):
    ms = jnp.mean(x * x, axis=-1, keepdims=True)
    return x * lax.rsqrt(ms + RMS_EPS) * gain


class _QkvSlabs:
    def __init__(self, x_ref, g_ref, w_ref, perm_ref, ones_ref, qg_ref, kg_ref, cos_ref, sin_ref, permute):
        rows = x_ref.shape[0]
        h = _rms(x_ref[...], g_ref[...]).astype(BF16)
        if permute:
            h = jnp.concatenate(
                [jnp.dot(perm_ref[...], h[i:i + ROW_TILE], preferred_element_type=F32).astype(BF16)
                 for i in range(0, rows, ROW_TILE)], axis=0)
        self.h, self.w_ref, self.ones = h, w_ref, ones_ref[...]
        self.gains = (qg_ref, kg_ref)
        self.cos = jnp.concatenate([cos_ref[...]] * 4, axis=1)
        self.sin = jnp.concatenate([sin_ref[...]] * 4, axis=1)
        lane = lax.broadcasted_iota(jnp.int32, (rows, ATTN_W), 1)
        self.first_half = (lane & (HEAD_DIM - 1)) < (HEAD_DIM // 2)

    def slab(self, s, g):
        col = (s * 3 + g) * ATTN_W
        y = jnp.dot(self.h, self.w_ref[:, col:col + ATTN_W], preferred_element_type=F32)
        if s == 2:
            return y
        yy = (y * y).astype(BF16)
        ss = jnp.concatenate([jnp.dot(yy[:, :256], self.ones, preferred_element_type=F32),
                              jnp.dot(yy[:, 256:], self.ones, preferred_element_type=F32)], axis=1)
        yn = y * lax.rsqrt(ss * (1.0 / HEAD_DIM) + RMS_EPS) * self.gains[s][...]
        swapped = jnp.where(self.first_half,
                            pltpu.roll(yn, ATTN_W - HEAD_DIM // 2, 1),
                            pltpu.roll(yn, HEAD_DIM // 2, 1))
        return yn * self.cos + swapped * self.sin


def _split3_bf16(x):
    hi = x.astype(BF16)
    r1 = x - hi.astype(F32)
    mid = r1.astype(BF16)
    lo = (r1 - mid.astype(F32)).astype(BF16)
    return hi, mid, lo


def _qkv_prompt_kernel(x_ref, g_ref, w_ref, perm_ref, permt_ref, ones_ref, qg_ref, kg_ref, cos_ref, sin_ref,
                       q0_ref, k0_ref, v0_ref, q_ref, k_ref, v_ref, t0_ref, t1_ref, t2_ref, *, tail_first):
    slabs = _QkvSlabs(x_ref, g_ref, w_ref, perm_ref, ones_ref, qg_ref, kg_ref, cos_ref, sin_ref, True)
    rows = x_ref.shape[0]
    tails = (t0_ref, t1_ref, t2_ref)
    t = pl.program_id(1)
    for g in range(3):
        y = [slabs.slab(s, g) for s in range(3)]
        for s, (ref0, ref) in enumerate(((q0_ref, q_ref), (k0_ref, k_ref), (v0_ref, v_ref))):
            for i in range(rows // ROW_TILE):
                blk = y[s][i * ROW_TILE:(i + 1) * ROW_TILE].reshape(P16, P16, ATTN_W)
                if g == 0:
                    ref0[:, i * P16:(i + 1) * P16, :] = blk
                else:
                    ref[:, i * P16:(i + 1) * P16, (g - 1) * ATTN_W:g * ATTN_W] = blk.astype(BF16)

        @pl.when(t >= tail_first[g])
        def _(g=g, y=y):
            keep = tails[g].shape[-1]
            for i in range((rows - keep) // ROW_TILE, rows // ROW_TILE):
                kv = jnp.concatenate([y[1][i * ROW_TILE:(i + 1) * ROW_TILE],
                                      y[2][i * ROW_TILE:(i + 1) * ROW_TILE]], axis=1)
                nat = sum(jnp.dot(permt_ref[...], part, preferred_element_type=F32)
                          for part in _split3_bf16(kv))
                lo = max(i * ROW_TILE, rows - keep)
                width = (i + 1) * ROW_TILE - lo
                tails[g][:, :, :, lo - (rows - keep):lo - (rows - keep) + width] = (
                    nat[ROW_TILE - width:].T.reshape(2, N_HEADS, HEAD_DIM, width))


def _qkv_sample_kernel(x_ref, g_ref, w_ref, perm_ref, ones_ref, qg_ref, kg_ref, cos_ref, sin_ref,
                       q_ref, k_ref, v_ref):
    slabs = _QkvSlabs(x_ref, g_ref, w_ref, perm_ref, ones_ref, qg_ref, kg_ref, cos_ref, sin_ref, False)
    for g in range(3):
        lo, hi = g * ATTN_W, (g + 1) * ATTN_W
        for s, ref in enumerate((q_ref, k_ref, v_ref)):
            ref[:, lo:hi] = slabs.slab(s, g)


def _qkv_common_inputs(norm_g, w_qkv_bf, q_gain, k_gain):
    ones = np.kron(np.eye(4, dtype=np.float32), np.ones((HEAD_DIM, HEAD_DIM), np.float32))
    return (norm_g.reshape(1, D_MODEL), w_qkv_bf, _perm_matrices()[0], jnp.asarray(ones, BF16),
            jnp.tile(q_gain, N_HEADS).reshape(1, ATTN_W), jnp.tile(k_gain, N_HEADS).reshape(1, ATTN_W))


def _rope_tables(pos):
    half = HEAD_DIM // 2
    inv = ROPE_THETA ** (-jnp.arange(half, dtype=F32) / half)
    ang = pos.astype(F32)[:, None] * inv[None, :]
    cos, sin = jnp.cos(ang), jnp.sin(ang)
    cos = jnp.concatenate([cos, cos, cos, cos], axis=1)
    sin = jnp.concatenate([-sin, sin, -sin, sin], axis=1)
    return cos, sin


def _p16_positions(seq):
    t = np.arange(seq).reshape(seq // ROW_TILE, P16, P16)
    return jnp.asarray(t.transpose(0, 2, 1).reshape(seq))


def _qkv_prompt(x2d, batch, seq, common):
    n_tb = seq // QKV_TILE
    n_a = seq // P16
    a_blk = QKV_TILE // P16
    cos, sin = _rope_tables(_p16_positions(seq))
    out_shape = ((jax.ShapeDtypeStruct((batch, P16, n_a, ATTN_W), F32),) * 3
                 + (jax.ShapeDtypeStruct((batch, P16, n_a, 2 * ATTN_W), BF16),) * 3
                 + tuple(jax.ShapeDtypeStruct((batch, 2, N_HEADS, HEAD_DIM, w), F32) for w in WINDOWS))
    g0_spec = pl.BlockSpec((None, P16, a_blk, ATTN_W), lambda b, t: (b, 0, t, 0))
    g12_spec = pl.BlockSpec((None, P16, a_blk, 2 * ATTN_W), lambda b, t: (b, 0, t, 0))
    tail_blk = tuple(min(w, QKV_TILE) for w in WINDOWS)
    tail_first = tuple(n_tb - w // blk for w, blk in zip(WINDOWS, tail_blk))

    def tail_spec(blk, first):
        return pl.BlockSpec((None, 2, N_HEADS, HEAD_DIM, blk),
                            lambda b, t: (b, 0, 0, 0, jnp.maximum(t - first, 0)))

    norm_g, w_bf, perm, ones, qg, kg = common
    return pl.pallas_call(
        functools.partial(_qkv_prompt_kernel, tail_first=tail_first),
        grid=(batch, n_tb),
        in_specs=[pl.BlockSpec((QKV_TILE, D_MODEL), lambda b, t: (b * n_tb + t, 0)),
                  _const_spec((1, D_MODEL)),
                  pl.BlockSpec((D_MODEL, 9 * ATTN_W), lambda b, t: (0, 0), pipeline_mode=pl.Buffered(1)),
                  _const_spec((ROW_TILE, ROW_TILE)), _const_spec((ROW_TILE, ROW_TILE)), _const_spec((256, 256)),
                  _const_spec((1, ATTN_W)), _const_spec((1, ATTN_W)),
                  pl.BlockSpec((QKV_TILE, 128), lambda b, t: (t, 0)),
                  pl.BlockSpec((QKV_TILE, 128), lambda b, t: (t, 0))],
        out_specs=((g0_spec,) * 3 + (g12_spec,) * 3
                   + tuple(tail_spec(blk, first) for blk, first in zip(tail_blk, tail_first))),
        out_shape=out_shape,
        compiler_params=_cparams("arbitrary", "arbitrary"),
        name="qkv_prompt",
    )(x2d, norm_g, w_bf, perm, _perm_matrices()[1], ones, qg, kg, cos, sin)


def _qkv_sample(x2d, pos, common):
    rows = x2d.shape[0]
    cos, sin = _rope_tables(pos)
    spec = pl.BlockSpec((ROW_TILE, 3 * ATTN_W), lambda i: (i, 0))
    return pl.pallas_call(
        _qkv_sample_kernel,
        grid=(rows // ROW_TILE,),
        in_specs=[pl.BlockSpec((ROW_TILE, D_MODEL), lambda i: (i, 0)),
                  _const_spec((1, D_MODEL)), _const_spec((D_MODEL, 9 * ATTN_W)),
                  _const_spec((ROW_TILE, ROW_TILE)), _const_spec((256, 256)),
                  _const_spec((1, ATTN_W)), _const_spec((1, ATTN_W)),
                  pl.BlockSpec((ROW_TILE, 128), lambda i: (i, 0)),
                  pl.BlockSpec((ROW_TILE, 128), lambda i: (i, 0))],
        out_specs=(spec,) * 3,
        out_shape=(jax.ShapeDtypeStruct((rows, 3 * ATTN_W), F32),) * 3,
        compiler_params=_cparams("arbitrary"),
        name="qkv_sample",
    )(x2d, *common, cos, sin)


ATTN_Q = 128


def _attn_geometry(dil):
    nres = P16 // dil
    qa = ATTN_Q // nres
    return nres, qa


def _attn_bias_tables(dil):
    nres, qa = _attn_geometry(dil)
    m = np.arange(ATTN_Q)
    n = np.arange(2 * ATTN_Q)
    qoff = nres * (m % qa) + m // qa
    koff = nres * (n % (2 * qa)) + n // (2 * qa)
    tables = []
    for delta in (0, qa):
        dist = qoff[:, None] - koff[None, :] + nres * delta
        bias = np.where((dist >= 0) & (dist <= KEYS_BACK), 0.0, NEG_BIG).astype(np.float32)
        tables.append(np.concatenate([bias, bias], axis=0))
    return jnp.asarray(np.stack(tables))


def _attn_group(bias_ref, q_ref, k_ref, v_ref, o_ref, l_ref, dil, n_a):
    nres, qa = _attn_geometry(dil)
    n_atiles = n_a // qa
    head1 = lax.broadcasted_iota(jnp.int32, (ATTN_Q, 128), 1) >= HEAD_DIM
    zeros = jnp.zeros((ATTN_Q, 128), BF16)
    ones = jnp.ones((2 * ATTN_Q, 128), BF16)

    def tile(t, carry):
        c = t // n_atiles
        at = t % n_atiles
        a0 = pl.multiple_of(at * qa, qa)
        k0 = pl.multiple_of(jnp.maximum(at * qa - qa, 0), qa)
        rows = [c + dil * b for b in range(nres)]
        q = jnp.concatenate([q_ref[r, pl.ds(a0, qa), :] for r in rows], axis=0).astype(BF16)
        k = jnp.concatenate([k_ref[r, pl.ds(k0, 2 * qa), :] for r in rows], axis=0).astype(BF16)
        v = jnp.concatenate([v_ref[r, pl.ds(k0, 2 * qa), :] for r in rows], axis=0).astype(BF16)
        q = q * jnp.asarray(HEAD_DIM ** -0.5, BF16)
        q2 = jnp.concatenate([jnp.where(head1, zeros, q), jnp.where(head1, q, zeros)], axis=0)
        s = lax.dot_general(q2, k, (((1,), (1,)), ((), ())), preferred_element_type=F32)
        s = s + bias_ref[jnp.minimum(at, 1)]
        mx = jnp.max(s, axis=-1, keepdims=True)
        p = jnp.exp(s - mx).astype(BF16)
        ov = jnp.dot(p, jnp.concatenate([v, ones], axis=1), preferred_element_type=F32)
        den = ov[:, 128:]
        o2 = ov[:, :128] / den
        lse2 = mx + jnp.log(den)
        o = jnp.where(head1, o2[ATTN_Q:], o2[:ATTN_Q])
        lse = jnp.where(head1, lse2[ATTN_Q:], lse2[:ATTN_Q])
        for b, r in enumerate(rows):
            o_ref[r, pl.ds(a0, qa), :] = o[b * qa:(b + 1) * qa]
            l_ref[r, pl.ds(a0, qa), :] = lse[b * qa:(b + 1) * qa]
        return carry

    lax.fori_loop(0, dil * n_atiles, tile, 0, unroll=4)


def _attn_prompt_kernel(b0, b1, b2, q0, k0, v0, q1, k1, v1, q2, k2, v2, o_ref, og_ref, lg_ref, *, n_a):
    groups = ((b0, q0, k0, v0), (b1, q1, k1, v1), (b2, q2, k2, v2))
    for g, (bias_ref, q_ref, k_ref, v_ref) in enumerate(groups):
        _attn_group(bias_ref, q_ref, k_ref, v_ref, og_ref.at[g], lg_ref.at[g], DILATIONS[g], n_a)

    def merge(r, carry):
        ls = [lg_ref[g, r] for g in range(3)]
        mx = jnp.maximum(jnp.maximum(ls[0], ls[1]), ls[2])
        es = [jnp.exp(l - mx) for l in ls]
        num = es[0] * og_ref[0, r] + es[1] * og_ref[1, r] + es[2] * og_ref[2, r]
        o_ref[r] = (num / (es[0] + es[1] + es[2])).astype(BF16)
        return carry

    lax.fori_loop(0, P16, merge, 0)


def _attn_prompt(qkv0, qkv12):
    batch, _, n_a, _ = qkv0[0].shape
    biases = [_attn_bias_tables(dil) for dil in DILATIONS]
    in_specs = [_const_spec(b.shape) for b in biases]
    operands = list(biases)
    for g in range(3):
        in_specs += [pl.BlockSpec((None, P16, n_a, 128), lambda b, h, g=g: (b, 0, 0, max(g - 1, 0) * 4 + h))] * 3
        operands += list(qkv0 if g == 0 else qkv12)
    return pl.pallas_call(
        functools.partial(_attn_prompt_kernel, n_a=n_a),
        grid=(batch, 4),
        in_specs=in_specs,
        out_specs=pl.BlockSpec((None, P16, n_a, 128), lambda b, h: (b, 0, 0, h)),
        out_shape=jax.ShapeDtypeStruct((batch, P16, n_a, ATTN_W), BF16),
        scratch_shapes=[pltpu.VMEM((3, P16, n_a, 128), F32), pltpu.VMEM((3, P16, n_a, 128), F32)],
        compiler_params=_cparams("arbitrary", "arbitrary"),
        name="attn_prompt",
    )(*operands)


def _attn_sample_kernel(q_ref, n_ref, c_ref, o_ref, l_ref, nc_ref, *, dil, win):
    n_new = q_ref.shape[1]
    nt = (((1,), (1,)), ((), ()))
    dist_c = (win + lax.broadcasted_iota(jnp.int32, (n_new, win), 0)
              - lax.broadcasted_iota(jnp.int32, (n_new, win), 1))
    ok_c = (dist_c <= win) & ((dist_c & (dil - 1)) == 0)
    dist_n = (lax.broadcasted_iota(jnp.int32, (n_new, n_new), 0)
              - lax.broadcasted_iota(jnp.int32, (n_new, n_new), 1))
    ok_n = (dist_n >= 0) & ((dist_n & (dil - 1)) == 0)
    lane = lax.broadcasted_iota(jnp.int32, (HEAD_DIM, 128), 1)
    pad_rows = jnp.zeros((128 - n_new, 128), F32)
    for h in range(N_HEADS):
        q = q_ref[h] * (HEAD_DIM ** -0.5)
        kt, vt = c_ref[0, h], c_ref[1, h]
        s_c = jnp.dot(q[:, :HEAD_DIM].astype(BF16), kt.astype(BF16), preferred_element_type=F32)
        s_n = lax.dot_general(q, n_ref[0, h], nt, preferred_element_type=F32)
        s_c = jnp.where(ok_c, s_c, NEG_BIG)
        s_n = jnp.where(ok_n, s_n, NEG_BIG)
        mx = jnp.maximum(jnp.max(s_c, axis=-1, keepdims=True), jnp.max(s_n, axis=-1, keepdims=True))
        p_c = jnp.exp(s_c - mx)
        p_n = jnp.exp(s_n - mx)
        den = jnp.sum(p_c, axis=-1, keepdims=True) + jnp.sum(p_n, axis=-1, keepdims=True)
        o = (lax.dot_general(p_c.astype(BF16), vt.astype(BF16), nt, preferred_element_type=F32)
             + jnp.dot(p_n, n_ref[1, h], preferred_element_type=F32)[:, :HEAD_DIM])
        o_ref[h] = o / den
        l_ref[h] = jnp.broadcast_to(mx + jnp.log(den), (n_new, HEAD_DIM))
        for kv, old in enumerate((kt, vt)):
            rolled = pltpu.roll(old, win - n_new, 1)
            new_t = jnp.concatenate([pad_rows, n_ref[kv, h]], axis=0).T[:HEAD_DIM]
            if win > 128:
                nc_ref[kv, h, :, :win - 128] = rolled[:, :win - 128]
            nc_ref[kv, h, :, win - 128:] = jnp.where(lane < 128 - n_new, rolled[:, win - 128:], new_t)


def _attn_sample(q_p, new_p, cache_t, g):
    batch, _, _, n_new, _ = q_p.shape
    win = cache_t.shape[-1]
    dil = DILATIONS[g]
    c_spec = pl.BlockSpec((None, 2, N_HEADS, HEAD_DIM, win), lambda b: (b, 0, 0, 0, 0))
    out_spec = pl.BlockSpec((None, N_HEADS, n_new, HEAD_DIM), lambda b: (b, 0, 0, 0))
    out_shape = jax.ShapeDtypeStruct((batch, N_HEADS, n_new, HEAD_DIM), F32)
    return pl.pallas_call(
        functools.partial(_attn_sample_kernel, dil=dil, win=win),
        grid=(batch,),
        in_specs=[pl.BlockSpec((None, None, N_HEADS, n_new, 128), lambda b: (b, g, 0, 0, 0)),
                  pl.BlockSpec((None, None, 2, N_HEADS, n_new, 128), lambda b: (b, g, 0, 0, 0, 0)),
                  c_spec],
        out_specs=(out_spec, out_spec, c_spec),
        out_shape=(out_shape, out_shape, jax.ShapeDtypeStruct(cache_t.shape, F32)),
        compiler_params=_cparams("arbitrary"),
        name=f"attn_sample_d{dil}",
    )(q_p, new_p, cache_t)


def _merge_wo_kernel(o0, o1, o2, l0, l1, l2, x_ref, w_ref, y_ref):
    os_ = [r[...] for r in (o0, o1, o2)]
    ls = [r[...] for r in (l0, l1, l2)]
    mx = jnp.maximum(jnp.maximum(ls[0], ls[1]), ls[2])
    es = [jnp.exp(l - mx) for l in ls]
    den = es[0] + es[1] + es[2]
    o = ((es[0] * os_[0] + es[1] * os_[1] + es[2] * os_[2]) / den).astype(BF16)
    y_ref[...] = x_ref[...] + jnp.dot(o, w_ref[...], preferred_element_type=F32)


def _merge_wo(os_, ls, x2d, w_o_bf):
    rows = x2d.shape[0]
    x_spec = pl.BlockSpec((ROW_TILE, D_MODEL), lambda i: (i, 0))
    a_spec = pl.BlockSpec((ROW_TILE, ATTN_W), lambda i: (i, 0))
    return pl.pallas_call(
        _merge_wo_kernel,
        grid=(rows // ROW_TILE,),
        in_specs=[a_spec] * 6 + [x_spec, _const_spec((ATTN_W, D_MODEL))],
        out_specs=x_spec,
        out_shape=jax.ShapeDtypeStruct((rows, D_MODEL), F32),
        compiler_params=_cparams("arbitrary"),
        name="merge_wo",
    )(*os_, *ls, x2d, w_o_bf)


def _wo_p16_kernel(o_ref, x_ref, w_ref, permt_ref, y_ref):
    blocks = []
    for i in range(QKV_TILE // ROW_TILE):
        o = o_ref[:, i * P16:(i + 1) * P16, :].astype(F32).reshape(ROW_TILE, ATTN_W).astype(BF16)
        blocks.append(jnp.dot(permt_ref[...], o, preferred_element_type=F32).astype(BF16))
    o = jnp.concatenate(blocks, axis=0)
    y_ref[...] = x_ref[...] + jnp.dot(o, w_ref[...], preferred_element_type=F32)


def _wo_p16(o_p16, x2d, w_o_bf, seq):
    rows = x2d.shape[0]
    n_tb = seq // QKV_TILE
    x_spec = pl.BlockSpec((QKV_TILE, D_MODEL), lambda i: (i, 0))
    return pl.pallas_call(
        _wo_p16_kernel,
        grid=(rows // QKV_TILE,),
        in_specs=[pl.BlockSpec((None, P16, QKV_TILE // P16, ATTN_W), lambda i: (i // n_tb, 0, i % n_tb, 0)),
                  x_spec, _const_spec((ATTN_W, D_MODEL)), _const_spec((ROW_TILE, ROW_TILE))],
        out_specs=x_spec,
        out_shape=jax.ShapeDtypeStruct((rows, D_MODEL), F32),
        compiler_params=_cparams("arbitrary"),
        name="wo_p16",
    )(o_p16, x2d, w_o_bf, _perm_matrices()[1])


def _ffn_kernel(x_ref, g_ref, wg_ref, wu_ref, wd_ref, y_ref, *, chunks):
    x = x_ref[...]
    h = _rms(x, g_ref[...]).astype(BF16)
    acc = x
    for lo, hi in chunks:
        gate = jnp.dot(h, wg_ref[:, lo:hi], preferred_element_type=F32)
        up = jnp.dot(h, wu_ref[:, lo:hi], preferred_element_type=F32)
        act = (jax.nn.silu(gate) * up).astype(BF16)
        acc = acc + jnp.dot(act, wd_ref[lo:hi, :], preferred_element_type=F32)
    y_ref[...] = acc


def _ffn(x2d, norm_g, wg_bf, wu_bf, wd_bf, tm):
    rows = x2d.shape[0]
    d_ff = wg_bf.shape[1]
    step = 1024
    chunks = tuple((lo, min(lo + step, d_ff)) for lo in range(0, d_ff, step))
    x_spec = pl.BlockSpec((tm, D_MODEL), lambda i: (i, 0))
    w_in = pl.BlockSpec((D_MODEL, d_ff), lambda i: (0, 0), pipeline_mode=pl.Buffered(1))
    w_out = pl.BlockSpec((d_ff, D_MODEL), lambda i: (0, 0), pipeline_mode=pl.Buffered(1))
    return pl.pallas_call(
        functools.partial(_ffn_kernel, chunks=chunks),
        grid=(rows // tm,),
        in_specs=[x_spec, _const_spec((1, D_MODEL)), w_in, w_in, w_out],
        out_specs=x_spec,
        out_shape=jax.ShapeDtypeStruct((rows, D_MODEL), F32),
        compiler_params=_cparams("arbitrary"),
        name="ffn",
    )(x2d, norm_g.reshape(1, D_MODEL), wg_bf, wu_bf, wd_bf)


def _norm_matmul_kernel(x_ref, g_ref, w_ref, y_ref):
    h = _rms(x_ref[...], g_ref[...]).astype(BF16)
    y_ref[...] = jnp.dot(h, w_ref[...], preferred_element_type=F32)


def _norm_matmul(x2d, norm_g, w_bf, tm):
    rows = x2d.shape[0]
    n = w_bf.shape[1]
    return pl.pallas_call(
        _norm_matmul_kernel,
        grid=(rows // tm,),
        in_specs=[pl.BlockSpec((tm, D_MODEL), lambda i: (i, 0)), _const_spec((1, D_MODEL)),
                  _const_spec((D_MODEL, n))],
        out_specs=pl.BlockSpec((tm, n), lambda i: (i, 0)),
        out_shape=jax.ShapeDtypeStruct((rows, n), F32),
        compiler_params=_cparams("arbitrary"),
        name="ssm_in_proj",
    )(x2d, norm_g.reshape(1, D_MODEL), w_bf)


SSM_LANE_TILE = 128
SSM_TOK_TILE = SSM_LANE_TILE * SSM_CHUNK


def _ssm_in_kernel(x_ref, g_ref, w_ref, perm_ref, ut_ref):
    n_blk = SSM_TOK_TILE // ROW_TILE
    blocks = []
    for kb in range(n_blk):
        h = _rms(x_ref[kb * ROW_TILE:(kb + 1) * ROW_TILE, :], g_ref[...]).astype(BF16)
        blocks.append(jnp.dot(perm_ref[...], h, preferred_element_type=F32).astype(BF16))
    hp = jnp.concatenate([blk[j * P16:(j + 1) * P16] for j in range(SSM_CHUNK) for blk in blocks], axis=0)
    u = jnp.dot(hp, w_ref[...], preferred_element_type=F32)
    for j in range(SSM_CHUNK):
        ut_ref[j] = u[j * SSM_LANE_TILE:(j + 1) * SSM_LANE_TILE].T


def _ssm_in(x2d, norm_g, w_bf, perm):
    rows = x2d.shape[0]
    n_tiles = rows // SSM_TOK_TILE
    return pl.pallas_call(
        _ssm_in_kernel,
        grid=(n_tiles,),
        in_specs=[pl.BlockSpec((SSM_TOK_TILE, D_MODEL), lambda i: (i, 0)), _const_spec((1, D_MODEL)),
                  pl.BlockSpec((D_MODEL, D_MODEL), lambda i: (0, 0), pipeline_mode=pl.Buffered(1)),
                  _const_spec((ROW_TILE, ROW_TILE))],
        out_specs=pl.BlockSpec((SSM_CHUNK, D_MODEL, SSM_LANE_TILE), lambda i: (0, 0, i)),
        out_shape=jax.ShapeDtypeStruct((SSM_CHUNK, D_MODEL, n_tiles * SSM_LANE_TILE), F32),
        compiler_params=_cparams("arbitrary"),
        name="ssm_in_t",
    )(x2d, norm_g.reshape(1, D_MODEL), w_bf, perm)


def _ssm_gate_kernel(yt_ref, ut_ref, d_ref, permt_ref, g_ref):
    n_blk = SSM_TOK_TILE // ROW_TILE
    pieces = []
    for i in range(SSM_CHUNK):
        yy = yt_ref[i] + d_ref[...] * ut_ref[i]
        pieces.append(jax.nn.gelu(yy).T.astype(BF16))
    for kb in range(n_blk):
        blk = jnp.concatenate([p[kb * P16:(kb + 1) * P16] for p in pieces], axis=0)
        g_ref[kb * ROW_TILE:(kb + 1) * ROW_TILE, :] = jnp.dot(
            permt_ref[...], blk, preferred_element_type=F32).astype(BF16)


def _ssm_gate(yt, ut, d_skip, permt):
    n_tiles = yt.shape[2] // SSM_LANE_TILE
    half = D_MODEL // 2
    t_spec = pl.BlockSpec((SSM_CHUNK, half, SSM_LANE_TILE), lambda i, c: (0, c, i))
    d_b = jnp.broadcast_to(d_skip[:, None], (D_MODEL, SSM_LANE_TILE))
    return pl.pallas_call(
        _ssm_gate_kernel,
        grid=(n_tiles, 2),
        in_specs=[t_spec, t_spec, pl.BlockSpec((half, SSM_LANE_TILE), lambda i, c: (c, 0)),
                  _const_spec((ROW_TILE, ROW_TILE))],
        out_specs=pl.BlockSpec((SSM_TOK_TILE, half), lambda i, c: (i, c)),
        out_shape=jax.ShapeDtypeStruct((n_tiles * SSM_TOK_TILE, D_MODEL), BF16),
        compiler_params=_cparams("arbitrary", "arbitrary"),
        name="ssm_gate_t",
    )(yt, ut, d_b, permt)


def _glu_act_kernel(g_ref, x_ref, w_ref, o_ref):
    z = jnp.dot(g_ref[...], w_ref[...], preferred_element_type=F32)
    o_ref[...] = x_ref[...] + z[:, :D_MODEL] * jax.nn.sigmoid(z[:, D_MODEL:])


def _glu_act(g2d, x2d, w_bf, tm):
    rows = x2d.shape[0]
    spec = pl.BlockSpec((tm, D_MODEL), lambda i: (i, 0))
    return pl.pallas_call(
        _glu_act_kernel,
        grid=(rows // tm,),
        in_specs=[spec, spec, _const_spec((D_MODEL, 2 * D_MODEL))],
        out_specs=spec,
        out_shape=jax.ShapeDtypeStruct((rows, D_MODEL), F32),
        compiler_params=_cparams("arbitrary"),
        name="ssm_glu_act",
    )(g2d, x2d, w_bf)


def _ssm_core_kernel(u_ref, mt_ref, ft_ref, et_ref, a_ref, x0_ref, y_ref, xf_ref, *, n_chunks, has_init):
    chunk, _, n_lanes = u_ref.shape
    u = u_ref[...].reshape(chunk * SSM_GROUP, n_lanes).astype(BF16)
    y = jnp.dot(mt_ref[...], u, preferred_element_type=F32)
    s = jnp.dot(ft_ref[...], u, preferred_element_type=F32)
    sre, sim = s[:SSM_STATE], s[SSM_STATE:]
    are, aim = a_ref[0], a_ref[1]
    if n_chunks == 1:
        x0 = x0_ref[...]
        xre, xim = x0[:SSM_STATE], x0[SSM_STATE:]
        xf_ref[:SSM_STATE, :] = are * xre - aim * xim + sre
        xf_ref[SSM_STATE:, :] = are * xim + aim * xre + sim
        y = y + jnp.dot(et_ref[...], x0.astype(BF16), preferred_element_type=F32)
    else:
        assert not has_init and n_chunks % 128 == 0
        reps = n_lanes // 128
        pos = lax.broadcasted_iota(jnp.int32, (SSM_STATE, n_lanes), 1) & (n_chunks - 1)
        shift = 1
        while shift < n_chunks:
            keep = pos >= shift
            tre = jnp.where(keep, pltpu.roll(sre, shift, 1), 0.0)
            tim = jnp.where(keep, pltpu.roll(sim, shift, 1), 0.0)
            bre = jnp.concatenate([are] * reps, axis=1)
            bim = jnp.concatenate([aim] * reps, axis=1)
            sre, sim = sre + bre * tre - bim * tim, sim + bre * tim + bim * tre
            are, aim = are * are - aim * aim, 2.0 * are * aim
            shift *= 2
        for b in range(n_lanes // n_chunks):
            lo, hi = (b + 1) * n_chunks - 128, (b + 1) * n_chunks
            xf_ref[:SSM_STATE, b * 128:(b + 1) * 128] = sre[:, lo:hi]
            xf_ref[SSM_STATE:, b * 128:(b + 1) * 128] = sim[:, lo:hi]
        keep = pos >= 1
        xin = jnp.concatenate([jnp.where(keep, pltpu.roll(sre, 1, 1), 0.0),
                               jnp.where(keep, pltpu.roll(sim, 1, 1), 0.0)], axis=0)
        y = y + jnp.dot(et_ref[...], xin.astype(BF16), preferred_element_type=F32)
    y_ref[...] = y.reshape(chunk, SSM_GROUP, n_lanes)


def _ssm_core(ut, mats, x0t, n_chunks):
    mt, ft, et, a_pow = mats
    chunk, _, n_lanes = ut.shape
    groups = SSM_GROUPS
    has_init = x0t is not None
    a_lanes = n_lanes if n_chunks == 1 else 128
    a_b = jnp.broadcast_to(a_pow[:, :, :, None], a_pow.shape + (a_lanes,))
    if x0t is None:
        x0t = jnp.zeros((groups, 2 * SSM_STATE, 128), F32)
    xf_lanes = n_lanes if n_chunks == 1 else (n_lanes // n_chunks) * 128

    def gspec(shape):
        return pl.BlockSpec((None,) + tuple(shape[1:]), lambda g: (g,) + (0,) * (len(shape) - 1))

    t_spec = pl.BlockSpec((chunk, SSM_GROUP, n_lanes), lambda g: (0, g, 0))
    return pl.pallas_call(
        functools.partial(_ssm_core_kernel, n_chunks=n_chunks, has_init=has_init),
        grid=(groups,),
        in_specs=[t_spec, gspec(mt.shape), gspec(ft.shape), gspec(et.shape), gspec(a_b.shape),
                  gspec(x0t.shape)],
        out_specs=(t_spec, gspec((groups, 2 * SSM_STATE, xf_lanes))),
        out_shape=(jax.ShapeDtypeStruct((chunk, D_MODEL, n_lanes), F32),
                   jax.ShapeDtypeStruct((groups, 2 * SSM_STATE, xf_lanes), F32)),        compiler_params=_cparams("arbitrary"),
        name=f"ssm_core_{n_chunks}",
    )(ut, mt, ft, et, a_b, x0t)


def _cexp(mag_arg, ang):
    mag = jnp.exp(mag_arg)
    return mag * jnp.cos(ang), mag * jnp.sin(ang)


def _ssm_matrices_kernel(ar_row, ai_row, ar_col, ai_col, ld_ref, bre_ref, bim_ref, cre_ref, cim_ref,
                         tile_ref, tile2_ref, mt_ref, ft_ref, et_ref, ap_ref, *, chunk):
    hp = lax.Precision.HIGHEST
    rows = SSM_GROUP * chunk
    dt = jnp.exp(ld_ref[...])

    def powers(re, im, n):
        out = [(jnp.ones_like(re), jnp.zeros_like(im))]
        for _ in range(n):
            pr, pi = out[-1]
            out.append((pr * re - pi * im, pr * im + pi * re))
        return out

    pw = powers(*_cexp(ar_row[...] * dt, ai_row[...] * dt), chunk)
    cre = jnp.dot(cre_ref[...], tile2_ref[...], precision=hp, preferred_element_type=F32)
    cim = jnp.dot(cim_ref[...], tile2_ref[...], precision=hp, preferred_element_type=F32)
    cre, cim = jnp.concatenate([cre] * chunk, axis=0), jnp.concatenate([cim] * chunk, axis=0)

    def by_step(first):
        return tuple(jnp.concatenate([jnp.broadcast_to(pw[first + i][k], (SSM_GROUP, 128))
                                      for i in range(chunk)], axis=0) for k in range(2))

    pr, pi = by_step(0)
    r0_re, r0_im = (cre * pr - cim * pi)[:, :SSM_STATE], (cre * pi + cim * pr)[:, :SSM_STATE]
    pr, pi = by_step(1)
    lane = lax.broadcasted_iota(jnp.int32, (rows, 128), 1)
    et_ref[...] = jnp.where(lane < SSM_STATE, cre * pr - cim * pi, -(cre * pi + cim * pr)).astype(BF16)
    for k, n in enumerate((chunk, chunk // 2)):
        ap_ref[2 * k:2 * k + 1, :] = pw[n][0]
        ap_ref[2 * k + 1:2 * k + 2, :] = pw[n][1]
    ar, ai = ar_col[...], ai_col[...]
    abr, abi = _cexp(ar * dt, ai * dt)
    inv = 1.0 / (ar * ar + ai * ai)
    zr = ((abr - 1.0) * ar + abi * ai) * inv
    zi = (abi * ar - (abr - 1.0) * ai) * inv
    bre, bim = bre_ref[...], bim_ref[...]
    bbr = jnp.dot(zr * bre - zi * bim, tile_ref[...], precision=hp, preferred_element_type=F32)
    bbi = jnp.dot(zr * bim + zi * bre, tile_ref[...], precision=hp, preferred_element_type=F32)
    pc = powers(abr, abi, chunk - 1)
    col_step = lax.broadcasted_iota(jnp.int32, (SSM_STATE, rows), 1) >> 4
    pr = jnp.zeros((SSM_STATE, rows), F32)
    pi = jnp.zeros((SSM_STATE, rows), F32)
    for j in range(chunk):
        pr = jnp.where(col_step == j, pc[chunk - 1 - j][0], pr)
        pi = jnp.where(col_step == j, pc[chunk - 1 - j][1], pi)
    ft_ref[:SSM_STATE, :] = (pr * bbr - pi * bbi).astype(BF16)
    ft_ref[SSM_STATE:, :] = (pr * bbi + pi * bbr).astype(BF16)
    tw = (jnp.dot(r0_re, bbr, precision=hp, preferred_element_type=F32)
          - jnp.dot(r0_im, bbi, precision=hp, preferred_element_type=F32))
    per_tile = 128 // SSM_GROUP
    row_blk = lax.broadcasted_iota(jnp.int32, (rows, 128), 0) >> 4
    col_blk = lax.broadcasted_iota(jnp.int32, (rows, 128), 1) >> 4
    for lt in range(rows // 128):
        tw_lt = tw[:, lt * 128:(lt + 1) * 128]
        acc = jnp.zeros((rows, 128), F32)
        for jj in range(per_tile):
            j = lt * per_tile + jj
            shifted = tw_lt if j == 0 else pltpu.roll(tw_lt, SSM_GROUP * j, 0)
            acc = jnp.where(col_blk == jj, shifted, acc)
        keep = row_blk >= col_blk + lt * per_tile
        mt_ref[:, lt * 128:(lt + 1) * 128] = jnp.where(keep, acc, 0.0).astype(BF16)


def _ssm_matrices(a_re, a_im, log_dt, b_re, b_im, c_re, c_im, chunk):
    groups, rows = SSM_GROUPS, SSM_GROUP * chunk

    def gspec(*shape):
        return pl.BlockSpec((None,) + shape, lambda g: (g,) + (0,) * len(shape))

    row2 = lambda a: jnp.tile(a, (1, 2)).reshape(groups, 1, 2 * SSM_STATE)
    col = lambda a: a.reshape(groups, SSM_STATE, 1)
    tile = jnp.asarray(np.tile(np.eye(SSM_GROUP, dtype=np.float32), (1, chunk)))
    tile2 = jnp.asarray(np.tile(np.eye(SSM_STATE, dtype=np.float32), (1, 2)))
    mt, ft, et, a_pow = pl.pallas_call(
        functools.partial(_ssm_matrices_kernel, chunk=chunk),
        grid=(groups,),
        in_specs=[gspec(1, 128), gspec(1, 128), gspec(SSM_STATE, 1), gspec(SSM_STATE, 1), gspec(1, 1),
                  gspec(SSM_STATE, SSM_GROUP), gspec(SSM_STATE, SSM_GROUP),
                  gspec(SSM_GROUP, SSM_STATE), gspec(SSM_GROUP, SSM_STATE),
                  _const_spec(tile.shape), _const_spec(tile2.shape)],
        out_specs=(gspec(rows, rows), gspec(2 * SSM_STATE, rows), gspec(rows, 2 * SSM_STATE), gspec(4, 128)),
        out_shape=(jax.ShapeDtypeStruct((groups, rows, rows), BF16),
                   jax.ShapeDtypeStruct((groups, 2 * SSM_STATE, rows), BF16),
                   jax.ShapeDtypeStruct((groups, rows, 2 * SSM_STATE), BF16),
                   jax.ShapeDtypeStruct((groups, 4, 128), F32)),
        compiler_params=_cparams("arbitrary"),
        name="ssm_matrices",
    )(row2(a_re), row2(a_im), col(a_re), col(a_im), log_dt.reshape(groups, 1, 1),
      b_re, b_im, c_re, c_im, tile, tile2)
    a_pow = a_pow[:, :, :SSM_STATE]
    full = (mt, ft, et, a_pow[:, 0:2])
    half_rows = rows // 2
    half = (mt[:, :half_rows, :half_rows], ft[:, :, half_rows:], et[:, :half_rows, :], a_pow[:, 2:4])
    return full, half


def _glu_kernel(y_ref, u_ref, d_ref, x_ref, w_ref, o_ref):
    yy = y_ref[...] + d_ref[...] * u_ref[...]
    z = jnp.dot(jax.nn.gelu(yy).astype(BF16), w_ref[...], preferred_element_type=F32)
    o_ref[...] = x_ref[...] + z[:, :D_MODEL] * jax.nn.sigmoid(z[:, D_MODEL:])


def _glu(y2d, u2d, d_skip, x2d, w_bf, tm):
    rows = x2d.shape[0]
    spec = pl.BlockSpec((tm, D_MODEL), lambda i: (i, 0))
    return pl.pallas_call(
        _glu_kernel,
        grid=(rows // tm,),
        in_specs=[spec, spec, _const_spec((1, D_MODEL)), spec, _const_spec((D_MODEL, 2 * D_MODEL))],
        out_specs=spec,
        out_shape=jax.ShapeDtypeStruct((rows, D_MODEL), F32),
        compiler_params=_cparams("arbitrary"),
        name="ssm_glu",
    )(y2d, u2d, d_skip.reshape(1, D_MODEL), x2d, w_bf)


def _ssm_mixer_prompt(x2d, batch, seq, norm_g, w_in_bf, mats, d_skip, w_glu_bf, perms, tm):
    assert seq % SSM_TOK_TILE == 0
    n_chunks = seq // SSM_CHUNK
    ut = _ssm_in(x2d, norm_g, w_in_bf, perms[0])
    yt, xf = _ssm_core(ut, mats, None, n_chunks)
    g2d = _ssm_gate(yt, ut, d_skip, perms[1])
    xf = xf.reshape(SSM_GROUPS, 2, SSM_STATE, batch, 128)[..., 127]
    return _glu_act(g2d, x2d, w_glu_bf, tm), xf.transpose(3, 0, 2, 1)


def _ssm_mixer_sample(x2d, batch, seq, norm_g, w_in_bf, mats, d_skip, w_glu_bf, state0, tm):
    u = _norm_matmul(x2d, norm_g, w_in_bf, tm)
    ut = u.reshape(batch, seq, D_MODEL).transpose(1, 2, 0)
    x0t = state0.transpose(1, 3, 2, 0).reshape(SSM_GROUPS, 2 * SSM_STATE, batch)
    yt, xf = _ssm_core(ut, mats, x0t, 1)
    y = yt.transpose(2, 0, 1).reshape(batch * seq, D_MODEL)
    state = xf.reshape(SSM_GROUPS, 2, SSM_STATE, batch).transpose(3, 0, 2, 1)
    return _glu(y, u, d_skip, x2d, w_glu_bf, tm), state


def kernel(x_prompt, x_sample, cache_kv_w128, cache_kv_w512, cache_kv_w2048, state_ssm, norm_mix, norm_ffn,
           w_qkv, q_norm, k_norm, w_o, ssm_w_in, ssm_a_re, ssm_a_im, ssm_log_dt, ssm_b_re, ssm_b_im,
           ssm_c_re, ssm_c_im, ssm_d, ssm_w_glu, ffn_w_gate, ffn_w_up, ffn_w_down):
    batch, seq, _ = x_prompt.shape
    dec_batch, dec_seq, _ = x_sample.shape
    caches = (cache_kv_w128, cache_kv_w512, cache_kv_w2048)
    assert seq % ROW_TILE == 0 and seq >= WINDOWS[-1] and (dec_batch * dec_seq) % ROW_TILE == 0
    assert all(c.shape[2] == w for c, w in zip(caches, WINDOWS))
    n_a = seq // P16
    tm_p = 512

    xp = x_prompt.reshape(batch * seq, D_MODEL)
    xs = x_sample.reshape(dec_batch * dec_seq, D_MODEL)

    common = _qkv_common_inputs(norm_mix[0], w_qkv[0].astype(BF16), q_norm[0], k_norm[0])
    w_o_bf = w_o[0].astype(BF16)
    q0, k0, v0, q12, k12, v12, t0, t1, t2 = _qkv_prompt(xp, batch, seq, common)
    xp = _wo_p16(_attn_prompt((q0, k0, v0), (q12, k12, v12)), xp, w_o_bf, seq)
    kv_prompt = [tail.transpose(0, 4, 1, 2, 3)[None] for tail in (t0, t1, t2)]

    pos_s = jnp.tile(PAST_LEN + jnp.arange(dec_seq), dec_batch)
    qs, ks, vs = _qkv_sample(xs, pos_s, common)

    def heads_padded(a):
        a = a.reshape(dec_batch, dec_seq, 3, N_HEADS, HEAD_DIM).transpose(0, 2, 3, 1, 4)
        return jnp.pad(a, ((0, 0),) * 4 + ((0, 128 - HEAD_DIM),))

    q_p = heads_padded(qs)
    new_p = jnp.stack([heads_padded(ks), heads_padded(vs)], axis=2)
    s_outs = [_attn_sample(q_p, new_p, caches[g][0].transpose(0, 2, 3, 4, 1), g) for g in range(3)]

    def rows_by_token(a):
        return a.transpose(0, 2, 1, 3).reshape(dec_batch * dec_seq, ATTN_W)

    xs = _merge_wo([rows_by_token(o) for o, _, _ in s_outs], [rows_by_token(l) for _, l, _ in s_outs],
                   xs, w_o_bf)
    kv_sample = [nc.transpose(0, 4, 1, 2, 3)[None] for _, _, nc in s_outs]

    ffn_w = [(ffn_w_gate[i].astype(BF16), ffn_w_up[i].astype(BF16), ffn_w_down[i].astype(BF16))
             for i in range(2)]
    xp = _ffn(xp, norm_ffn[0], *ffn_w[0], FFN_TILE)
    xs = _ffn(xs, norm_ffn[0], *ffn_w[0], ROW_TILE)

    ssm_p = (ssm_a_re[0], ssm_a_im[0], ssm_log_dt[0], ssm_b_re[0], ssm_b_im[0], ssm_c_re[0], ssm_c_im[0])
    w_in_bf = ssm_w_in[0].astype(BF16)
    w_glu_bf = ssm_w_glu[0].astype(BF16)
    assert 2 * dec_seq == SSM_CHUNK
    mats_full, mats_half = _ssm_matrices(*ssm_p, SSM_CHUNK)
    xp, st_p = _ssm_mixer_prompt(xp, batch, seq, norm_mix[1], w_in_bf, mats_full,
                                 ssm_d[0], w_glu_bf, _perm_matrices(), tm_p)
    xs, st_s = _ssm_mixer_sample(xs, dec_batch, dec_seq, norm_mix[1], w_in_bf, mats_half,
                                 ssm_d[0], w_glu_bf, state_ssm[0], ROW_TILE)
    xp = _ffn(xp, norm_ffn[1], *ffn_w[1], FFN_TILE)
    xs = _ffn(xs, norm_ffn[1], *ffn_w[1], ROW_TILE)

    return (xp.reshape(batch, seq, D_MODEL), xs.reshape(dec_batch, dec_seq, D_MODEL),
            kv_prompt[0], kv_prompt[1], kv_prompt[2], st_p[None],
            kv_sample[0], kv_sample[1], kv_sample[2], st_s[None])
```

```python
import functools
import math

import numpy as np
import jax
import jax.numpy as jnp
from jax import lax
from jax.experimental import pallas as pl
from jax.experimental.pallas import tpu as pltpu

F32 = jnp.float32
BF16 = jnp.bfloat16

D_MODEL = 1024
HEAD_DIM = 64
N_HEADS = 8
ATTN_W = N_HEADS * HEAD_DIM
DILATIONS = (1, 4, 16)
WINDOWS = (128, 512, 2048)
KEYS_BACK = 128
PAST_LEN = 16384
ROPE_THETA = 10000.0
RMS_EPS = 1e-6
SSM_GROUP = 16
SSM_GROUPS = D_MODEL // SSM_GROUP
SSM_STATE = 64
SSM_CHUNK = 16
ROW_TILE = 256
QKV_TILE = 512
P16 = 16
NEG_BIG = -1e30
VMEM_LIMIT = 56 * 1024 * 1024


def _cparams(*sem):
    return pltpu.CompilerParams(dimension_semantics=tuple(sem), vmem_limit_bytes=VMEM_LIMIT)


def _const_spec(shape):
    nd = len(shape)
    return pl.BlockSpec(shape, lambda *_: (0,) * nd)


def _perm_matrices():
    m = np.arange(ROW_TILE)
    to_p16 = np.zeros((ROW_TILE, ROW_TILE), np.float32)
    to_p16[m, P16 * (m % P16) + m // P16] = 1.0
    return jnp.asarray(to_p16, BF16), jnp.asarray(to_p16.T, BF16)


def _rms(x, gain):
    ms = jnp.mean(x * x, axis=-1, keepdims=True)
    return x * lax.rsqrt(ms + RMS_EPS) * gain


class _QkvSlabs:
    def __init__(self, x_ref, g_ref, w_ref, perm_ref, ones_ref, qg_ref, kg_ref, cos_ref, sin_ref, permute):
        rows = x_ref.shape[0]
        h = _rms(x_ref[...], g_ref[...]).astype(BF16)
        if permute:
            h = jnp.concatenate(
                [jnp.dot(perm_ref[...], h[i:i + ROW_TILE], preferred_element_type=F32).astype(BF16)
                 for i in range(0, rows, ROW_TILE)], axis=0)
        self.h, self.w_ref, self.ones = h, w_ref, ones_ref[...]
        self.gains = (qg_ref, kg_ref)
        self.cos = jnp.concatenate([cos_ref[...]] * 4, axis=1)
        self.sin = jnp.concatenate([sin_ref[...]] * 4, axis=1)
        lane = lax.broadcasted_iota(jnp.int32, (rows, ATTN_W), 1)
        self.first_half = (lane & (HEAD_DIM - 1)) < (HEAD_DIM // 2)

    def slab(self, s, g):
        col = (s * 3 + g) * ATTN_W
        y = jnp.dot(self.h, self.w_ref[:, col:col + ATTN_W], preferred_element_type=F32)
        if s == 2:
            return y
        yy = (y * y).astype(BF16)
        ss = jnp.concatenate([jnp.dot(yy[:, :256], self.ones, preferred_element_type=F32),
                              jnp.dot(yy[:, 256:], self.ones, preferred_element_type=F32)], axis=1)
        yn = y * lax.rsqrt(ss * (1.0 / HEAD_DIM) + RMS_EPS) * self.gains[s][...]
        swapped = jnp.where(self.first_half,
                            pltpu.roll(yn, ATTN_W - HEAD_DIM // 2, 1),
                            pltpu.roll(yn, HEAD_DIM // 2, 1))
        return yn * self.cos + swapped * self.sin


def _split3_bf16(x):
    hi = x.astype(BF16)
    r1 = x - hi.astype(F32)
    mid = r1.astype(BF16)
    lo = (r1 - mid.astype(F32)).astype(BF16)
    return hi, mid, lo


def _qkv_prompt_kernel(x_ref, g_ref, w_ref, perm_ref, permt_ref, ones_ref, qg_ref, kg_ref, cos_ref, sin_ref,
                       q0_ref, k0_ref, v0_ref, q_ref, k_ref, v_ref, t0_ref, t1_ref, t2_ref, *, tail_first):
    slabs = _QkvSlabs(x_ref, g_ref, w_ref, perm_ref, ones_ref, qg_ref, kg_ref, cos_ref, sin_ref, True)
    rows = x_ref.shape[0]
    tails = (t0_ref, t1_ref, t2_ref)
    t = pl.program_id(1)
    for g in range(3):
        y = [slabs.slab(s, g) for s in range(3)]
        for s, (ref0, ref) in enumerate(((q0_ref, q_ref), (k0_ref, k_ref), (v0_ref, v_ref))):
            for i in range(rows // ROW_TILE):
                blk = y[s][i * ROW_TILE:(i + 1) * ROW_TILE].reshape(P16, P16, ATTN_W)
                if g == 0:
                    ref0[:, i * P16:(i + 1) * P16, :] = blk
                else:
                    ref[:, i * P16:(i + 1) * P16, (g - 1) * ATTN_W:g * ATTN_W] = blk.astype(BF16)

        @pl.when(t >= tail_first[g])
        def _(g=g, y=y):
            keep = tails[g].shape[-1]
            for i in range((rows - keep) // ROW_TILE, rows // ROW_TILE):
                kv = jnp.concatenate([y[1][i * ROW_TILE:(i + 1) * ROW_TILE],
                                      y[2][i * ROW_TILE:(i + 1) * ROW_TILE]], axis=1)
                nat = sum(jnp.dot(permt_ref[...], part, preferred_element_type=F32)
                          for part in _split3_bf16(kv))
                lo = max(i * ROW_TILE, rows - keep)
                width = (i + 1) * ROW_TILE - lo
                tails[g][:, :, :, lo - (rows - keep):lo - (rows - keep) + width] = (
                    nat[ROW_TILE - width:].T.reshape(2, N_HEADS, HEAD_DIM, width))


def _qkv_sample_kernel(x_ref, g_ref, w_ref, perm_ref, ones_ref, qg_ref, kg_ref, cos_ref, sin_ref,
                       q_ref, k_ref, v_ref):
    slabs = _QkvSlabs(x_ref, g_ref, w_ref, perm_ref, ones_ref, qg_ref, kg_ref, cos_ref, sin_ref, False)
    for g in range(3):
        lo, hi = g * ATTN_W, (g + 1) * ATTN_W
        for s, ref in enumerate((q_ref, k_ref, v_ref)):
            ref[:, lo:hi] = slabs.slab(s, g)


def _qkv_common_inputs(norm_g, w_qkv_bf, q_gain, k_gain):
    ones = np.kron(np.eye(4, dtype=np.float32), np.ones((HEAD_DIM, HEAD_DIM), np.float32))
    return (norm_g.reshape(1, D_MODEL), w_qkv_bf, _perm_matrices()[0], jnp.asarray(ones, BF16),
            jnp.tile(q_gain, N_HEADS).reshape(1, ATTN_W), jnp.tile(k_gain, N_HEADS).reshape(1, ATTN_W))


def _rope_tables(pos):
    half = HEAD_DIM // 2
    inv = ROPE_THETA ** (-jnp.arange(half, dtype=F32) / half)
    ang = pos.astype(F32)[:, None] * inv[None, :]
    cos, sin = jnp.cos(ang), jnp.sin(ang)
    cos = jnp.concatenate([cos, cos, cos, cos], axis=1)
    sin = jnp.concatenate([-sin, sin, -sin, sin], axis=1)
    return cos, sin


def _p16_positions(seq):
    t = np.arange(seq).reshape(seq // ROW_TILE, P16, P16)
    return jnp.asarray(t.transpose(0, 2, 1).reshape(seq))


def _qkv_prompt(x2d, batch, seq, common):
    n_tb = seq // QKV_TILE
    n_a = seq // P16
    a_blk = QKV_TILE // P16
    cos, sin = _rope_tables(_p16_positions(seq))
    out_shape = ((jax.ShapeDtypeStruct((batch, P16, n_a, ATTN_W), F32),) * 3
                 + (jax.ShapeDtypeStruct((batch, P16, n_a, 2 * ATTN_W), BF16),) * 3
                 + tuple(jax.ShapeDtypeStruct((batch, 2, N_HEADS, HEAD_DIM, w), F32) for w in WINDOWS))
    g0_spec = pl.BlockSpec((None, P16, a_blk, ATTN_W), lambda b, t: (b, 0, t, 0))
    g12_spec = pl.BlockSpec((None, P16, a_blk, 2 * ATTN_W), lambda b, t: (b, 0, t, 0))
    tail_blk = tuple(min(w, QKV_TILE) for w in WINDOWS)
    tail_first = tuple(n_tb - w // blk for w, blk in zip(WINDOWS, tail_blk))

    def tail_spec(blk, first):
        return pl.BlockSpec((None, 2, N_HEADS, HEAD_DIM, blk),
                            lambda b, t: (b, 0, 0, 0, jnp.maximum(t - first, 0)))

    norm_g, w_bf, perm, ones, qg, kg = common
    return pl.pallas_call(
        functools.partial(_qkv_prompt_kernel, tail_first=tail_first),
        grid=(batch, n_tb),
        in_specs=[pl.BlockSpec((QKV_TILE, D_MODEL), lambda b, t: (b * n_tb + t, 0)),
                  _const_spec((1, D_MODEL)),
                  pl.BlockSpec((D_MODEL, 9 * ATTN_W), lambda b, t: (0, 0), pipeline_mode=pl.Buffered(1)),
                  _const_spec((ROW_TILE, ROW_TILE)), _const_spec((ROW_TILE, ROW_TILE)), _const_spec((256, 256)),
                  _const_spec((1, ATTN_W)), _const_spec((1, ATTN_W)),
                  pl.BlockSpec((QKV_TILE, 128), lambda b, t: (t, 0)),
                  pl.BlockSpec((QKV_TILE, 128), lambda b, t: (t, 0))],
        out_specs=((g0_spec,) * 3 + (g12_spec,) * 3
                   + tuple(tail_spec(blk, first) for blk, first in zip(tail_blk, tail_first))),
        out_shape=out_shape,
        compiler_params=_cparams("arbitrary", "arbitrary"),
        name="qkv_prompt",
    )(x2d, norm_g, w_bf, perm, _perm_matrices()[1], ones, qg, kg, cos, sin)


def _qkv_sample(x2d, pos, common):
    rows = x2d.shape[0]
    cos, sin = _rope_tables(pos)
    spec = pl.BlockSpec((ROW_TILE, 3 * ATTN_W), lambda i: (i, 0))
    return pl.pallas_call(
        _qkv_sample_kernel,
        grid=(rows // ROW_TILE,),
        in_specs=[pl.BlockSpec((ROW_TILE, D_MODEL), lambda i: (i, 0)),
                  _const_spec((1, D_MODEL)), _const_spec((D_MODEL, 9 * ATTN_W)),
                  _const_spec((ROW_TILE, ROW_TILE)), _const_spec((256, 256)),
                  _const_spec((1, ATTN_W)), _const_spec((1, ATTN_W)),
                  pl.BlockSpec((ROW_TILE, 128), lambda i: (i, 0)),
                  pl.BlockSpec((ROW_TILE, 128), lambda i: (i, 0))],
        out_specs=(spec,) * 3,
        out_shape=(jax.ShapeDtypeStruct((rows, 3 * ATTN_W), F32),) * 3,
        compiler_params=_cparams("arbitrary"),
        name="qkv_sample",
    )(x2d, *common, cos, sin)


ATTN_Q = 128


def _attn_geometry(dil):
    nres = P16 // dil
    qa = ATTN_Q // nres
    return nres, qa


def _attn_bias_tables(dil):
    nres, qa = _attn_geometry(dil)
    m = np.arange(ATTN_Q)
    n = np.arange(2 * ATTN_Q)
    qoff = nres * (m % qa) + m // qa
    koff = nres * (n % (2 * qa)) + n // (2 * qa)
    tables = []
    for delta in (0, qa):
        dist = qoff[:, None] - koff[None, :] + nres * delta
        bias = np.where((dist >= 0) & (dist <= KEYS_BACK), 0.0, NEG_BIG).astype(np.float32)
        tables.append(np.concatenate([bias, bias], axis=0))
    return jnp.asarray(np.stack(tables))


def _attn_group(bias_ref, q_ref, k_ref, v_ref, o_ref, l_ref, dil, n_a):
    nres, qa = _attn_geometry(dil)
    n_atiles = n_a // qa
    head1 = lax.broadcasted_iota(jnp.int32, (ATTN_Q, 128), 1) >= HEAD_DIM
    zeros = jnp.zeros((ATTN_Q, 128), BF16)
    ones = jnp.ones((2 * ATTN_Q, 128), BF16)

    def tile(t, carry):
        c = t // n_atiles
        at = t % n_atiles
        a0 = pl.multiple_of(at * qa, qa)
        k0 = pl.multiple_of(jnp.maximum(at * qa - qa, 0), qa)
        rows = [c + dil * b for b in range(nres)]
        q = jnp.concatenate([q_ref[r, pl.ds(a0, qa), :] for r in rows], axis=0).astype(BF16)
        k = jnp.concatenate([k_ref[r, pl.ds(k0, 2 * qa), :] for r in rows], axis=0).astype(BF16)
        v = jnp.concatenate([v_ref[r, pl.ds(k0, 2 * qa), :] for r in rows], axis=0).astype(BF16)
        q = q * jnp.asarray(HEAD_DIM ** -0.5, BF16)
        q2 = jnp.concatenate([jnp.where(head1, zeros, q), jnp.where(head1, q, zeros)], axis=0)
        s = lax.dot_general(q2, k, (((1,), (1,)), ((), ())), preferred_element_type=F32)
        s = s + bias_ref[jnp.minimum(at, 1)]
        mx = jnp.max(s, axis=-1, keepdims=True)
        p = jnp.exp(s - mx).astype(BF16)
        ov = jnp.dot(p, jnp.concatenate([v, ones], axis=1), preferred_element_type=F32)
        den = ov[:, 128:]
        o2 = ov[:, :128] / den
        lse2 = mx + jnp.log(den)
        o = jnp.where(head1, o2[ATTN_Q:], o2[:ATTN_Q])
        lse = jnp.where(head1, lse2[ATTN_Q:], lse2[:ATTN_Q])
        for b, r in enumerate(rows):
            o_ref[r, pl.ds(a0, qa), :] = o[b * qa:(b + 1) * qa]
            l_ref[r, pl.ds(a0, qa), :] = lse[b * qa:(b + 1) * qa]
        return carry

    lax.fori_loop(0, dil * n_atiles, tile, 0, unroll=4)


def _attn_prompt_kernel(b0, b1, b2, q0, k0, v0, q1, k1, v1, q2, k2, v2, o_ref, og_ref, lg_ref, *, n_a):
    groups = ((b0, q0, k0, v0), (b1, q1, k1, v1), (b2, q2, k2, v2))
    for g, (bias_ref, q_ref, k_ref, v_ref) in enumerate(groups):
        _attn_group(bias_ref, q_ref, k_ref, v_ref, og_ref.at[g], lg_ref.at[g], DILATIONS[g], n_a)

    def merge(r, carry):
        ls = [lg_ref[g, r] for g in range(3)]
        mx = jnp.maximum(jnp.maximum(ls[0], ls[1]), ls[2])
        es = [jnp.exp(l - mx) for l in ls]
        num = es[0] * og_ref[0, r] + es[1] * og_ref[1, r] + es[2] * og_ref[2, r]
        o_ref[r] = (num / (es[0] + es[1] + es[2])).astype(BF16)
        return carry

    lax.fori_loop(0, P16, merge, 0)


def _attn_prompt(qkv0, qkv12):
    batch, _, n_a, _ = qkv0[0].shape
    biases = [_attn_bias_tables(dil) for dil in DILATIONS]
    in_specs = [_const_spec(b.shape) for b in biases]
    operands = list(biases)
    for g in range(3):
        in_specs += [pl.BlockSpec((None, P16, n_a, 128), lambda b, h, g=g: (b, 0, 0, max(g - 1, 0) * 4 + h))] * 3
        operands += list(qkv0 if g == 0 else qkv12)
    return pl.pallas_call(
        functools.partial(_attn_prompt_kernel, n_a=n_a),
        grid=(batch, 4),
        in_specs=in_specs,
        out_specs=pl.BlockSpec((None, P16, n_a, 128), lambda b, h: (b, 0, 0, h)),
        out_shape=jax.ShapeDtypeStruct((batch, P16, n_a, ATTN_W), BF16),
        scratch_shapes=[pltpu.VMEM((3, P16, n_a, 128), F32), pltpu.VMEM((3, P16, n_a, 128), F32)],
        compiler_params=_cparams("arbitrary", "arbitrary"),
        name="attn_prompt",
    )(*operands)


def _attn_sample_kernel(q_ref, n_ref, c0_ref, c1_ref, c2_ref, o_ref, nc0_ref, nc1_ref, nc2_ref):
    n_new = q_ref.shape[2]
    nt = (((1,), (1,)), ((), ()))
    caches = ((c0_ref, nc0_ref), (c1_ref, nc1_ref), (c2_ref, nc2_ref))
    masks = []
    for (c_ref, _), dil in zip(caches, DILATIONS):
        win = c_ref.shape[-1]
        dist_c = (win + lax.broadcasted_iota(jnp.int32, (n_new, win), 0)
                  - lax.broadcasted_iota(jnp.int32, (n_new, win), 1))
        dist_n = (lax.broadcasted_iota(jnp.int32, (n_new, n_new), 0)
                  - lax.broadcasted_iota(jnp.int32, (n_new, n_new), 1))
        masks.append(((dist_c <= win) & ((dist_c & (dil - 1)) == 0),
                      (dist_n >= 0) & ((dist_n & (dil - 1)) == 0)))
    lane = lax.broadcasted_iota(jnp.int32, (HEAD_DIM, 128), 1)
    pad_rows = jnp.zeros((128 - n_new, 128), F32)
    for h in range(N_HEADS):
        outs, lses = [], []
        for g, (c_ref, nc_ref) in enumerate(caches):
            win = c_ref.shape[-1]
            ok_c, ok_n = masks[g]
            q = q_ref[g, h] * (HEAD_DIM ** -0.5)
            kt, vt = c_ref[0, h], c_ref[1, h]
            s_c = jnp.dot(q[:, :HEAD_DIM].astype(BF16), kt.astype(BF16), preferred_element_type=F32)
            s_n = lax.dot_general(q, n_ref[g, 0, h], nt, preferred_element_type=F32)
            s_c = jnp.where(ok_c, s_c, NEG_BIG)
            s_n = jnp.where(ok_n, s_n, NEG_BIG)
            mx = jnp.maximum(jnp.max(s_c, axis=-1, keepdims=True), jnp.max(s_n, axis=-1, keepdims=True))
            p_c = jnp.exp(s_c - mx)
            p_n = jnp.exp(s_n - mx)
            den = jnp.sum(p_c, axis=-1, keepdims=True) + jnp.sum(p_n, axis=-1, keepdims=True)
            o = (lax.dot_general(p_c.astype(BF16), vt.astype(BF16), nt, preferred_element_type=F32)
                 + jnp.dot(p_n, n_ref[g, 1, h], preferred_element_type=F32)[:, :HEAD_DIM])
            outs.append(o / den)
            lses.append(mx + jnp.log(den))
            for kv, old in enumerate((kt, vt)):
                rolled = pltpu.roll(old, win - n_new, 1)
                new_t = jnp.concatenate([pad_rows, n_ref[g, kv, h]], axis=0).T[:HEAD_DIM]
                if win > 128:
                    nc_ref[kv, h, :, :win - 128] = rolled[:, :win - 128]
                nc_ref[kv, h, :, win - 128:] = jnp.where(lane < 128 - n_new, rolled[:, win - 128:], new_t)
        mx = jnp.maximum(jnp.maximum(lses[0], lses[1]), lses[2])
        es = [jnp.exp(l - mx) for l in lses]
        o_ref[h] = (es[0] * outs[0] + es[1] * outs[1] + es[2] * outs[2]) / (es[0] + es[1] + es[2])


def _attn_sample(q_p, new_p, caches_t):
    batch, _, _, n_new, _ = q_p.shape
    c_specs = [pl.BlockSpec((None, 2, N_HEADS, HEAD_DIM, c.shape[-1]), lambda b: (b, 0, 0, 0, 0))
               for c in caches_t]
    return pl.pallas_call(
        _attn_sample_kernel,
        grid=(batch,),
        in_specs=[pl.BlockSpec((None, 3, N_HEADS, n_new, 128), lambda b: (b, 0, 0, 0, 0)),
                  pl.BlockSpec((None, 3, 2, N_HEADS, n_new, 128), lambda b: (b, 0, 0, 0, 0, 0))] + c_specs,
        out_specs=[pl.BlockSpec((None, N_HEADS, n_new, HEAD_DIM), lambda b: (b, 0, 0, 0))] + c_specs,
        out_shape=[jax.ShapeDtypeStruct((batch, N_HEADS, n_new, HEAD_DIM), F32)]
        + [jax.ShapeDtypeStruct(c.shape, F32) for c in caches_t],
        compiler_params=_cparams("arbitrary"),
        name="attn_sample",
    )(q_p, new_p, *caches_t)


def _ffn_block(x, g_ref, wg_ref, wu_ref, wd_ref, chunks):
    h = _rms(x, g_ref[...]).astype(BF16)
    acc = x
    for lo, hi in chunks:
        gate = jnp.dot(h, wg_ref[:, lo:hi], preferred_element_type=F32)
        up = jnp.dot(h, wu_ref[:, lo:hi], preferred_element_type=F32)
        act = (jax.nn.silu(gate) * up).astype(BF16)
        acc = acc + jnp.dot(act, wd_ref[lo:hi, :], preferred_element_type=F32)
    return acc


def _ffn_kernel(x_ref, g_ref, wg_ref, wu_ref, wd_ref, y_ref, *, chunks):
    y_ref[...] = _ffn_block(x_ref[...], g_ref, wg_ref, wu_ref, wd_ref, chunks)


def _wo_ffn_kernel(o_ref, x_ref, wo_ref, permt_ref, g_ref, wg_ref, wu_ref, wd_ref, y_ref, *, chunks):
    blocks = []
    for i in range(o_ref.shape[1] // P16):
        o = o_ref[:, i * P16:(i + 1) * P16, :].astype(F32).reshape(ROW_TILE, ATTN_W).astype(BF16)
        blocks.append(jnp.dot(permt_ref[...], o, preferred_element_type=F32).astype(BF16))
    x = x_ref[...] + jnp.dot(jnp.concatenate(blocks, axis=0), wo_ref[...], preferred_element_type=F32)
    y_ref[...] = _ffn_block(x, g_ref, wg_ref, wu_ref, wd_ref, chunks)


def _wo_rows_ffn_kernel(o_ref, x_ref, wo_ref, g_ref, wg_ref, wu_ref, wd_ref, y_ref, *, chunks):
    x = x_ref[...] + jnp.dot(o_ref[...].astype(BF16), wo_ref[...], preferred_element_type=F32)
    y_ref[...] = _ffn_block(x, g_ref, wg_ref, wu_ref, wd_ref, chunks)


def _glu_ffn_kernel(a_ref, x_ref, wglu_ref, g_ref, wg_ref, wu_ref, wd_ref, y_ref, *, chunks):
    z = jnp.dot(a_ref[...], wglu_ref[...], preferred_element_type=F32)
    x = x_ref[...] + z[:, :D_MODEL] * jax.nn.sigmoid(z[:, D_MODEL:])
    y_ref[...] = _ffn_block(x, g_ref, wg_ref, wu_ref, wd_ref, chunks)


def _ffn(x2d, norm_g, wg_bf, wu_bf, wd_bf, tm, pre=None, seq=None):
    rows = x2d.shape[0]
    d_ff = wg_bf.shape[1]
    step = 1024
    chunks = tuple((lo, min(lo + step, d_ff)) for lo in range(0, d_ff, step))
    x_spec = pl.BlockSpec((tm, D_MODEL), lambda i: (i, 0))

    def resident(shape):
        return pl.BlockSpec(shape, lambda i: (0,) * len(shape), pipeline_mode=pl.Buffered(1))

    ffn_specs = [_const_spec((1, D_MODEL)), resident((D_MODEL, d_ff)), resident((D_MODEL, d_ff)),
                 resident((d_ff, D_MODEL))]
    ffn_args = (norm_g.reshape(1, D_MODEL), wg_bf, wu_bf, wd_bf)
    if pre is None:
        body, specs, args, name = _ffn_kernel, [x_spec], (x2d,), "ffn"
    elif pre[0] == "wo":
        n_tb = seq // tm
        o_spec = pl.BlockSpec((None, P16, tm // P16, ATTN_W), lambda i: (i // n_tb, 0, i % n_tb, 0))
        body, name = _wo_ffn_kernel, "wo_ffn"
        specs = [o_spec, x_spec, resident((ATTN_W, D_MODEL)), _const_spec((ROW_TILE, ROW_TILE))]
        args = (pre[1], x2d, pre[2], _perm_matrices()[1])
    elif pre[0] == "wo_rows":
        body, name = _wo_rows_ffn_kernel, "wo_rows_ffn"
        specs = [pl.BlockSpec((tm, ATTN_W), lambda i: (i, 0)), x_spec, resident((ATTN_W, D_MODEL))]
        args = (pre[1], x2d, pre[2])
    else:
        body, name = _glu_ffn_kernel, "glu_ffn"
        specs = [x_spec, x_spec, resident((D_MODEL, 2 * D_MODEL))]
        args = (pre[1], x2d, pre[2])
    return pl.pallas_call(
        functools.partial(body, chunks=chunks),
        grid=(rows // tm,),
        in_specs=specs + ffn_specs,
        out_specs=x_spec,
        out_shape=jax.ShapeDtypeStruct((rows, D_MODEL), F32),
        compiler_params=_cparams("arbitrary"),
        name=name,
    )(*args, *ffn_args)


def _norm_matmul_kernel(x_ref, g_ref, w_ref, y_ref):
    h = _rms(x_ref[...], g_ref[...]).astype(BF16)
    y_ref[...] = jnp.dot(h, w_ref[...], preferred_element_type=F32).astype(y_ref.dtype)


def _norm_matmul(x2d, norm_g, w_bf, tm):
    rows = x2d.shape[0]
    n = w_bf.shape[1]
    return pl.pallas_call(
        _norm_matmul_kernel,
        grid=(rows // tm,),
        in_specs=[pl.BlockSpec((tm, D_MODEL), lambda i: (i, 0)), _const_spec((1, D_MODEL)),
                  _const_spec((D_MODEL, n))],
        out_specs=pl.BlockSpec((tm, n), lambda i: (i, 0)),
        out_shape=jax.ShapeDtypeStruct((rows, n), BF16),
        compiler_params=_cparams("arbitrary"),
        name="ssm_in_proj",
    )(x2d, norm_g.reshape(1, D_MODEL), w_bf)


SSM_LANE_TILE = 128
SSM_TOK_TILE = SSM_LANE_TILE * SSM_CHUNK


def _ssm_in_kernel(x_ref, g_ref, w_ref, perm_ref, ut_ref):
    n_blk = SSM_TOK_TILE // ROW_TILE
    blocks = []
    for kb in range(n_blk):
        h = _rms(x_ref[kb * ROW_TILE:(kb + 1) * ROW_TILE, :], g_ref[...]).astype(BF16)
        blocks.append(jnp.dot(perm_ref[...], h, preferred_element_type=F32).astype(BF16))
    hp = jnp.concatenate([blk[j * P16:(j + 1) * P16] for j in range(SSM_CHUNK) for blk in blocks], axis=0)
    u = jnp.dot(hp, w_ref[...], preferred_element_type=F32)
    for j in range(SSM_CHUNK):
        ut_ref[j] = u[j * SSM_LANE_TILE:(j + 1) * SSM_LANE_TILE].T.astype(BF16)


def _ssm_in(x2d, norm_g, w_bf, perm):
    rows = x2d.shape[0]
    n_tiles = rows // SSM_TOK_TILE
    return pl.pallas_call(
        _ssm_in_kernel,
        grid=(n_tiles,),
        in_specs=[pl.BlockSpec((SSM_TOK_TILE, D_MODEL), lambda i: (i, 0)), _const_spec((1, D_MODEL)),
                  pl.BlockSpec((D_MODEL, D_MODEL), lambda i: (0, 0), pipeline_mode=pl.Buffered(1)),
                  _const_spec((ROW_TILE, ROW_TILE))],
        out_specs=pl.BlockSpec((SSM_CHUNK, D_MODEL, SSM_LANE_TILE), lambda i: (0, 0, i)),
        out_shape=jax.ShapeDtypeStruct((SSM_CHUNK, D_MODEL, n_tiles * SSM_LANE_TILE), BF16),
        compiler_params=_cparams("arbitrary"),
        name="ssm_in_t",
    )(x2d, norm_g.reshape(1, D_MODEL), w_bf, perm)


def _ssm_gate_kernel(yt_ref, permt_ref, g_ref):
    n_blk = SSM_TOK_TILE // ROW_TILE
    pieces = []
    for i in range(SSM_CHUNK):
        pieces.append(jax.nn.gelu(yt_ref[i]).T.astype(BF16))
    for kb in range(n_blk):
        blk = jnp.concatenate([p[kb * P16:(kb + 1) * P16] for p in pieces], axis=0)
        g_ref[kb * ROW_TILE:(kb + 1) * ROW_TILE, :] = jnp.dot(
            permt_ref[...], blk, preferred_element_type=F32).astype(BF16)


def _ssm_gate(yt, permt):
    n_tiles = yt.shape[2] // SSM_LANE_TILE
    return pl.pallas_call(
        _ssm_gate_kernel,
        grid=(n_tiles,),
        in_specs=[pl.BlockSpec((SSM_CHUNK, D_MODEL, SSM_LANE_TILE), lambda i: (0, 0, i)),
                  _const_spec((ROW_TILE, ROW_TILE))],
        out_specs=pl.BlockSpec((SSM_TOK_TILE, D_MODEL), lambda i: (i, 0)),
        out_shape=jax.ShapeDtypeStruct((n_tiles * SSM_TOK_TILE, D_MODEL), BF16),
        compiler_params=_cparams("arbitrary"),
        name="ssm_gate_t",
    )(yt, permt)


def _ssm_core_kernel(u_ref, mt_ref, ft_ref, et_ref, a_ref, x0_ref, y_ref, xf_ref, *, n_chunks, has_init):
    chunk, _, n_lanes = u_ref.shape
    u = u_ref[...].reshape(chunk * SSM_GROUP, n_lanes)
    y = jnp.dot(mt_ref[...], u, preferred_element_type=F32)
    s = jnp.dot(ft_ref[...], u, preferred_element_type=F32)
    sre, sim = s[:SSM_STATE], s[SSM_STATE:]
    are, aim = a_ref[0], a_ref[1]
    if n_chunks == 1:
        x0 = x0_ref[...]
        xre, xim = x0[:SSM_STATE], x0[SSM_STATE:]
        xf_ref[:SSM_STATE, :] = are * xre - aim * xim + sre
        xf_ref[SSM_STATE:, :] = are * xim + aim * xre + sim
        y = y + jnp.dot(et_ref[...], x0.astype(BF16), preferred_element_type=F32)
    else:
        assert not has_init and n_chunks % 128 == 0
        reps = n_lanes // 128
        pos = lax.broadcasted_iota(jnp.int32, (SSM_STATE, n_lanes), 1) & (n_chunks - 1)
        shift = 1
        while shift < n_chunks:
            keep = pos >= shift
            tre = jnp.where(keep, pltpu.roll(sre, shift, 1), 0.0)
            tim = jnp.where(keep, pltpu.roll(sim, shift, 1), 0.0)
            bre = jnp.concatenate([are] * reps, axis=1)
            bim = jnp.concatenate([aim] * reps, axis=1)
            sre, sim = sre + bre * tre - bim * tim, sim + bre * tim + bim * tre
            are, aim = are * are - aim * aim, 2.0 * are * aim
            shift *= 2
        for b in range(n_lanes // n_chunks):
            lo, hi = (b + 1) * n_chunks - 128, (b + 1) * n_chunks
            xf_ref[:SSM_STATE, b * 128:(b + 1) * 128] = sre[:, lo:hi]
            xf_ref[SSM_STATE:, b * 128:(b + 1) * 128] = sim[:, lo:hi]
        keep = pos >= 1
        xin = jnp.concatenate([jnp.where(keep, pltpu.roll(sre, 1, 1), 0.0),
                               jnp.where(keep, pltpu.roll(sim, 1, 1), 0.0)], axis=0)
        y = y + jnp.dot(et_ref[...], xin.astype(BF16), preferred_element_type=F32)
    y_ref[...] = y.reshape(chunk, SSM_GROUP, n_lanes)


def _ssm_core(ut, mats, x0t, n_chunks):
    mt, ft, et, a_pow = mats
    chunk, _, n_lanes = ut.shape
    groups = SSM_GROUPS
    has_init = x0t is not None
    a_lanes = n_lanes if n_chunks == 1 else 128
    a_b = jnp.broadcast_to(a_pow[:, :, :, None], a_pow.shape + (a_lanes,))
    if x0t is None:
        x0t = jnp.zeros((groups, 2 * SSM_STATE, 128), F32)
    xf_lanes = n_lanes if n_chunks == 1 else (n_lanes // n_chunks) * 128

    def gspec(shape):
        return pl.BlockSpec((None,) + tuple(shape[1:]), lambda g: (g,) + (0,) * (len(shape) - 1))

    t_spec = pl.BlockSpec((chunk, SSM_GROUP, n_lanes), lambda g: (0, g, 0))
    return pl.pallas_call(
        functools.partial(_ssm_core_kernel, n_chunks=n_chunks, has_init=has_init),
        grid=(groups,),
        in_specs=[t_spec, gspec(mt.shape), gspec(ft.shape), gspec(et.shape), gspec(a_b.shape),
                  gspec(x0t.shape)],
        out_specs=(t_spec, gspec((groups, 2 * SSM_STATE, xf_lanes))),
        out_shape=(jax.ShapeDtypeStruct((chunk, D_MODEL, n_lanes), F32),
                   jax.ShapeDtypeStruct((groups, 2 * SSM_STATE, xf_lanes), F32)),        compiler_params=_cparams("arbitrary"),
        name=f"ssm_core_{n_chunks}",
    )(ut, mt, ft, et, a_b, x0t)


def _cexp(mag_arg, ang):
    mag = jnp.exp(mag_arg)
    return mag * jnp.cos(ang), mag * jnp.sin(ang)


def _ssm_matrices_kernel(ar_row, ai_row, ar_col, ai_col, ld_ref, bre_ref, bim_ref, cre_ref, cim_ref,
                         d_ref, tile_ref, tile2_ref, mt_ref, ft_ref, et_ref, ap_ref, mth_ref, fth_ref, eth_ref,
                         *, chunk):
    hp = lax.Precision.HIGHEST
    rows = SSM_GROUP * chunk
    dt = jnp.exp(ld_ref[...])

    def powers(re, im, n):
        out = [(jnp.ones_like(re), jnp.zeros_like(im))]
        for _ in range(n):
            pr, pi = out[-1]
            out.append((pr * re - pi * im, pr * im + pi * re))
        return out

    pw = powers(*_cexp(ar_row[...] * dt, ai_row[...] * dt), chunk)
    cre = jnp.dot(cre_ref[...], tile2_ref[...], precision=hp, preferred_element_type=F32)
    cim = jnp.dot(cim_ref[...], tile2_ref[...], precision=hp, preferred_element_type=F32)
    cre, cim = jnp.concatenate([cre] * chunk, axis=0), jnp.concatenate([cim] * chunk, axis=0)

    def by_step(first):
        return tuple(jnp.concatenate([jnp.broadcast_to(pw[first + i][k], (SSM_GROUP, 128))
                                      for i in range(chunk)], axis=0) for k in range(2))

    pr, pi = by_step(0)
    r0_re, r0_im = (cre * pr - cim * pi)[:, :SSM_STATE], (cre * pi + cim * pr)[:, :SSM_STATE]
    pr, pi = by_step(1)
    lane = lax.broadcasted_iota(jnp.int32, (rows, 128), 1)
    et = jnp.where(lane < SSM_STATE, cre * pr - cim * pi, -(cre * pi + cim * pr)).astype(BF16)
    et_ref[...] = et
    eth_ref[...] = et[:rows // 2]
    for k, n in enumerate((chunk, chunk // 2)):
        ap_ref[2 * k:2 * k + 1, :] = pw[n][0]
        ap_ref[2 * k + 1:2 * k + 2, :] = pw[n][1]
    ar, ai = ar_col[...], ai_col[...]
    abr, abi = _cexp(ar * dt, ai * dt)
    inv = 1.0 / (ar * ar + ai * ai)
    zr = ((abr - 1.0) * ar + abi * ai) * inv
    zi = (abi * ar - (abr - 1.0) * ai) * inv
    bre, bim = bre_ref[...], bim_ref[...]
    bbr = jnp.dot(zr * bre - zi * bim, tile_ref[...], precision=hp, preferred_element_type=F32)
    bbi = jnp.dot(zr * bim + zi * bre, tile_ref[...], precision=hp, preferred_element_type=F32)
    pc = powers(abr, abi, chunk - 1)
    col_step = lax.broadcasted_iota(jnp.int32, (SSM_STATE, rows), 1) >> 4
    pr = jnp.zeros((SSM_STATE, rows), F32)
    pi = jnp.zeros((SSM_STATE, rows), F32)
    for j in range(chunk):
        pr = jnp.where(col_step == j, pc[chunk - 1 - j][0], pr)
        pi = jnp.where(col_step == j, pc[chunk - 1 - j][1], pi)
    ft = jnp.concatenate([pr * bbr - pi * bbi, pr * bbi + pi * bbr], axis=0).astype(BF16)
    ft_ref[...] = ft
    fth_ref[...] = ft[:, rows // 2:]
    tw = (jnp.dot(r0_re, bbr, precision=hp, preferred_element_type=F32)
          - jnp.dot(r0_im, bbi, precision=hp, preferred_element_type=F32))
    per_tile = 128 // SSM_GROUP
    row_idx = lax.broadcasted_iota(jnp.int32, (rows, 128), 0)
    col_idx = lax.broadcasted_iota(jnp.int32, (rows, 128), 1)
    row_blk, col_blk = row_idx >> 4, col_idx >> 4
    for lt in range(rows // 128):
        tw_lt = tw[:, lt * 128:(lt + 1) * 128]
        acc = jnp.zeros((rows, 128), F32)
        for jj in range(per_tile):
            j = lt * per_tile + jj
            shifted = tw_lt if j == 0 else pltpu.roll(tw_lt, SSM_GROUP * j, 0)
            acc = jnp.where(col_blk == jj, shifted, acc)
        keep = row_blk >= col_blk + lt * per_tile
        diag = row_idx == col_idx + lt * 128
        m_lt = (jnp.where(keep, acc, 0.0) + jnp.where(diag, d_ref[:, lt * 128:(lt + 1) * 128], 0.0)).astype(BF16)
        mt_ref[:, lt * 128:(lt + 1) * 128] = m_lt
        if lt == 0:
            mth_ref[...] = m_lt[:rows // 2]


def _ssm_matrices(a_re, a_im, log_dt, b_re, b_im, c_re, c_im, d_skip, chunk):
    groups, rows = SSM_GROUPS, SSM_GROUP * chunk
    half_rows = rows // 2

    def gspec(*shape):
        return pl.BlockSpec((None,) + shape, lambda g: (g,) + (0,) * len(shape))

    row2 = lambda a: jnp.tile(a, (1, 2)).reshape(groups, 1, 2 * SSM_STATE)
    col = lambda a: a.reshape(groups, SSM_STATE, 1)
    tile = jnp.asarray(np.tile(np.eye(SSM_GROUP, dtype=np.float32), (1, chunk)))
    tile2 = jnp.asarray(np.tile(np.eye(SSM_STATE, dtype=np.float32), (1, 2)))
    d_t = jnp.tile(d_skip.reshape(groups, 1, SSM_GROUP), (1, 1, chunk))
    mt, ft, et, a_pow, mt_h, ft_h, et_h = pl.pallas_call(
        functools.partial(_ssm_matrices_kernel, chunk=chunk),
        grid=(groups,),
        in_specs=[gspec(1, 128), gspec(1, 128), gspec(SSM_STATE, 1), gspec(SSM_STATE, 1), gspec(1, 1),
                  gspec(SSM_STATE, SSM_GROUP), gspec(SSM_STATE, SSM_GROUP),
                  gspec(SSM_GROUP, SSM_STATE), gspec(SSM_GROUP, SSM_STATE), gspec(1, rows),
                  _const_spec(tile.shape), _const_spec(tile2.shape)],
        out_specs=(gspec(rows, rows), gspec(2 * SSM_STATE, rows), gspec(rows, 2 * SSM_STATE), gspec(4, 128),
                   gspec(half_rows, half_rows), gspec(2 * SSM_STATE, half_rows), gspec(half_rows, 2 * SSM_STATE)),
        out_shape=(jax.ShapeDtypeStruct((groups, rows, rows), BF16),
                   jax.ShapeDtypeStruct((groups, 2 * SSM_STATE, rows), BF16),
                   jax.ShapeDtypeStruct((groups, rows, 2 * SSM_STATE), BF16),
                   jax.ShapeDtypeStruct((groups, 4, 128), F32),
                   jax.ShapeDtypeStruct((groups, half_rows, half_rows), BF16),
                   jax.ShapeDtypeStruct((groups, 2 * SSM_STATE, half_rows), BF16),
                   jax.ShapeDtypeStruct((groups, half_rows, 2 * SSM_STATE), BF16)),
        compiler_params=_cparams("arbitrary"),
        name="ssm_matrices",
    )(row2(a_re), row2(a_im), col(a_re), col(a_im), log_dt.reshape(groups, 1, 1),
      b_re, b_im, c_re, c_im, d_t, tile, tile2)
    a_pow = a_pow[:, :, :SSM_STATE]
    return (mt, ft, et, a_pow[:, 0:2]), (mt_h, ft_h, et_h, a_pow[:, 2:4])


def _gelu_kernel(y_ref, a_ref):
    a_ref[...] = jax.nn.gelu(y_ref[...]).astype(BF16)


def _gelu(y2d, tm):
    spec = pl.BlockSpec((tm, D_MODEL), lambda i: (i, 0))
    return pl.pallas_call(
        _gelu_kernel,
        grid=(y2d.shape[0] // tm,),
        in_specs=[spec],
        out_specs=spec,
        out_shape=jax.ShapeDtypeStruct(y2d.shape, BF16),
        compiler_params=_cparams("arbitrary"),
        name="ssm_gelu",
    )(y2d)


def _ssm_mixer_prompt(x2d, batch, seq, norm_g, w_in_bf, mats, perms):
    assert seq % SSM_TOK_TILE == 0
    n_chunks = seq // SSM_CHUNK
    ut = _ssm_in(x2d, norm_g, w_in_bf, perms[0])
    yt, xf = _ssm_core(ut, mats, None, n_chunks)
    xf = xf.reshape(SSM_GROUPS, 2, SSM_STATE, batch, 128)[..., 127]
    return _ssm_gate(yt, perms[1]), xf.transpose(3, 0, 2, 1)


def _ssm_mixer_sample(x2d, batch, seq, norm_g, w_in_bf, mats, state0, tm):
    u = _norm_matmul(x2d, norm_g, w_in_bf, tm)
    ut = u.reshape(batch, seq, D_MODEL).transpose(1, 2, 0)
    x0t = state0.transpose(1, 3, 2, 0).reshape(SSM_GROUPS, 2 * SSM_STATE, batch)
    yt, xf = _ssm_core(ut, mats, x0t, 1)
    y = yt.transpose(2, 0, 1).reshape(batch * seq, D_MODEL)
    state = xf.reshape(SSM_GROUPS, 2, SSM_STATE, batch).transpose(3, 0, 2, 1)
    return _gelu(y, tm), state


def kernel(x_prompt, x_sample, cache_kv_w128, cache_kv_w512, cache_kv_w2048, state_ssm, norm_mix, norm_ffn,
           w_qkv, q_norm, k_norm, w_o, ssm_w_in, ssm_a_re, ssm_a_im, ssm_log_dt, ssm_b_re, ssm_b_im,
           ssm_c_re, ssm_c_im, ssm_d, ssm_w_glu, ffn_w_gate, ffn_w_up, ffn_w_down):
    batch, seq, _ = x_prompt.shape
    dec_batch, dec_seq, _ = x_sample.shape
    caches = (cache_kv_w128, cache_kv_w512, cache_kv_w2048)
    assert seq % ROW_TILE == 0 and seq >= WINDOWS[-1] and (dec_batch * dec_seq) % ROW_TILE == 0
    assert all(c.shape[2] == w for c, w in zip(caches, WINDOWS))
    xp = x_prompt.reshape(batch * seq, D_MODEL)
    xs = x_sample.reshape(dec_batch * dec_seq, D_MODEL)

    common = _qkv_common_inputs(norm_mix[0], w_qkv[0].astype(BF16), q_norm[0], k_norm[0])
    w_o_bf = w_o[0].astype(BF16)
    q0, k0, v0, q12, k12, v12, t0, t1, t2 = _qkv_prompt(xp, batch, seq, common)
    o_prompt = _attn_prompt((q0, k0, v0), (q12, k12, v12))
    kv_prompt = [tail.transpose(0, 4, 1, 2, 3)[None] for tail in (t0, t1, t2)]

    pos_s = jnp.tile(PAST_LEN + jnp.arange(dec_seq), dec_batch)
    qs, ks, vs = _qkv_sample(xs, pos_s, common)

    def heads_padded(a):
        a = a.reshape(dec_batch, dec_seq, 3, N_HEADS, HEAD_DIM).transpose(0, 2, 3, 1, 4)
        return jnp.pad(a, ((0, 0),) * 4 + ((0, 128 - HEAD_DIM),))

    q_p = heads_padded(qs)
    new_p = jnp.stack([heads_padded(ks), heads_padded(vs)], axis=2)
    o_s, *new_caches = _attn_sample(q_p, new_p, [c[0].transpose(0, 2, 3, 4, 1) for c in caches])
    o_sample = o_s.transpose(0, 2, 1, 3).reshape(dec_batch * dec_seq, ATTN_W)
    kv_sample = [nc.transpose(0, 4, 1, 2, 3)[None] for nc in new_caches]

    ffn_w = [(ffn_w_gate[i].astype(BF16), ffn_w_up[i].astype(BF16), ffn_w_down[i].astype(BF16))
             for i in range(2)]
    xp = _ffn(xp, norm_ffn[0], *ffn_w[0], QKV_TILE, pre=("wo", o_prompt, w_o_bf), seq=seq)
    xs = _ffn(xs, norm_ffn[0], *ffn_w[0], ROW_TILE, pre=("wo_rows", o_sample, w_o_bf))

    ssm_p = (ssm_a_re[0], ssm_a_im[0], ssm_log_dt[0], ssm_b_re[0], ssm_b_im[0], ssm_c_re[0], ssm_c_im[0])
    w_in_bf = ssm_w_in[0].astype(BF16)
    w_glu_bf = ssm_w_glu[0].astype(BF16)
    assert 2 * dec_seq == SSM_CHUNK
    mats_full, mats_half = _ssm_matrices(*ssm_p, ssm_d[0], SSM_CHUNK)
    act_p, st_p = _ssm_mixer_prompt(xp, batch, seq, norm_mix[1], w_in_bf, mats_full, _perm_matrices())
    act_s, st_s = _ssm_mixer_sample(xs, dec_batch, dec_seq, norm_mix[1], w_in_bf, mats_half, state_ssm[0],
                                    ROW_TILE)
    xp = _ffn(xp, norm_ffn[1], *ffn_w[1], QKV_TILE, pre=("glu", act_p, w_glu_bf))
    xs = _ffn(xs, norm_ffn[1], *ffn_w[1], ROW_TILE, pre=("glu", act_s, w_glu_bf))

    return (xp.reshape(batch, seq, D_MODEL), xs.reshape(dec_batch, dec_seq, D_MODEL),
            kv_prompt[0], kv_prompt[1], kv_prompt[2], st_p[None],
            kv_sample[0], kv_sample[1], kv_sample[2], st_s[None])
```

```python
import functools
import math

import numpy as np
import jax
import jax.numpy as jnp
from jax import lax
from jax.experimental import pallas as pl
from jax.experimental.pallas import tpu as pltpu

F32 = jnp.float32
BF16 = jnp.bfloat16

D_MODEL = 1024
HEAD_DIM = 64
N_HEADS = 8
ATTN_W = N_HEADS * HEAD_DIM
DILATIONS = (1, 4, 16)
WINDOWS = (128, 512, 2048)
KEYS_BACK = 128
PAST_LEN = 16384
ROPE_THETA = 10000.0
RMS_EPS = 1e-6
SSM_GROUP = 16
SSM_GROUPS = D_MODEL // SSM_GROUP
SSM_STATE = 64
SSM_CHUNK = 16
ROW_TILE = 256
QKV_TILE = 512
P16 = 16
NEG_BIG = -1e30
VMEM_LIMIT = 56 * 1024 * 1024


def _cparams(*sem):
    return pltpu.CompilerParams(dimension_semantics=tuple(sem), vmem_limit_bytes=VMEM_LIMIT)


def _const_spec(shape):
    nd = len(shape)
    return pl.BlockSpec(shape, lambda *_: (0,) * nd)


def _perm_matrices():
    m = np.arange(ROW_TILE)
    to_p16 = np.zeros((ROW_TILE, ROW_TILE), np.float32)
    to_p16[m, P16 * (m % P16) + m // P16] = 1.0
    return jnp.asarray(to_p16, BF16), jnp.asarray(to_p16.T, BF16)


def _rms(x, gain):
    ms = jnp.mean(x * x, axis=-1, keepdims=True)
    return x * lax.rsqrt(ms + RMS_EPS) * gain


class _QkvSlabs:
    def __init__(self, x_ref, g_ref, w_ref, perm_ref, ones_ref, qg_ref, kg_ref, cos_ref, sin_ref, permute):
        rows = x_ref.shape[0]
        h = _rms(x_ref[...], g_ref[...]).astype(BF16)
        if permute:
            h = jnp.concatenate(
                [jnp.dot(perm_ref[...], h[i:i + ROW_TILE], preferred_element_type=F32).astype(BF16)
                 for i in range(0, rows, ROW_TILE)], axis=0)
        self.h, self.w_ref, self.ones = h, w_ref, ones_ref[...]
        self.gains = (qg_ref, kg_ref)
        self.cos = jnp.concatenate([cos_ref[...]] * 4, axis=1)
        self.sin = jnp.concatenate([sin_ref[...]] * 4, axis=1)
        lane = lax.broadcasted_iota(jnp.int32, (rows, ATTN_W), 1)
        self.first_half = (lane & (HEAD_DIM - 1)) < (HEAD_DIM // 2)

    def slab(self, s, g):
        col = (s * 3 + g) * ATTN_W
        y = jnp.dot(self.h, self.w_ref[:, col:col + ATTN_W], preferred_element_type=F32)
        if s == 2:
            return y
        yy = (y * y).astype(BF16)
        ss = jnp.concatenate([jnp.dot(yy[:, :256], self.ones, preferred_element_type=F32),
                              jnp.dot(yy[:, 256:], self.ones, preferred_element_type=F32)], axis=1)
        yn = y * lax.rsqrt(ss * (1.0 / HEAD_DIM) + RMS_EPS) * self.gains[s][...]
        swapped = jnp.where(self.first_half,
                            pltpu.roll(yn, ATTN_W - HEAD_DIM // 2, 1),
                            pltpu.roll(yn, HEAD_DIM // 2, 1))
        return yn * self.cos + swapped * self.sin


def _split3_bf16(x):
    hi = x.astype(BF16)
    r1 = x - hi.astype(F32)
    mid = r1.astype(BF16)
    lo = (r1 - mid.astype(F32)).astype(BF16)
    return hi, mid, lo


def _qkv_prompt_kernel(x_ref, g_ref, w_ref, perm_ref, permt_ref, ones_ref, qg_ref, kg_ref, cos_ref, sin_ref,
                       q0_ref, k0_ref, v0_ref, q_ref, k_ref, v_ref, t0_ref, t1_ref, t2_ref, *, tail_first):
    slabs = _QkvSlabs(x_ref, g_ref, w_ref, perm_ref, ones_ref, qg_ref, kg_ref, cos_ref, sin_ref, True)
    rows = x_ref.shape[0]
    tails = (t0_ref, t1_ref, t2_ref)
    t = pl.program_id(1)
    for g in range(3):
        y = [slabs.slab(s, g) for s in range(3)]
        for s, (ref0, ref) in enumerate(((q0_ref, q_ref), (k0_ref, k_ref), (v0_ref, v_ref))):
            for i in range(rows // ROW_TILE):
                blk = y[s][i * ROW_TILE:(i + 1) * ROW_TILE].reshape(P16, P16, ATTN_W)
                if g == 0:
                    ref0[:, i * P16:(i + 1) * P16, :] = blk
                else:
                    ref[:, i * P16:(i + 1) * P16, (g - 1) * ATTN_W:g * ATTN_W] = blk.astype(BF16)

        @pl.when(t >= tail_first[g])
        def _(g=g, y=y):
            keep = tails[g].shape[-1]
            for i in range((rows - keep) // ROW_TILE, rows // ROW_TILE):
                kv = jnp.concatenate([y[1][i * ROW_TILE:(i + 1) * ROW_TILE],
                                      y[2][i * ROW_TILE:(i + 1) * ROW_TILE]], axis=1)
                nat = sum(jnp.dot(permt_ref[...], part, preferred_element_type=F32)
                          for part in _split3_bf16(kv))
                lo = max(i * ROW_TILE, rows - keep)
                width = (i + 1) * ROW_TILE - lo
                tails[g][:, :, :, lo - (rows - keep):lo - (rows - keep) + width] = (
                    nat[ROW_TILE - width:].T.reshape(2, N_HEADS, HEAD_DIM, width))


def _qkv_sample_kernel(x_ref, g_ref, w_ref, perm_ref, ones_ref, qg_ref, kg_ref, cos_ref, sin_ref,
                       q_ref, k_ref, v_ref):
    slabs = _QkvSlabs(x_ref, g_ref, w_ref, perm_ref, ones_ref, qg_ref, kg_ref, cos_ref, sin_ref, False)
    for g in range(3):
        lo, hi = g * ATTN_W, (g + 1) * ATTN_W
        for s, ref in enumerate((q_ref, k_ref, v_ref)):
            ref[:, lo:hi] = slabs.slab(s, g)


def _qkv_common_inputs(norm_g, w_qkv_bf, q_gain, k_gain):
    ones = np.kron(np.eye(4, dtype=np.float32), np.ones((HEAD_DIM, HEAD_DIM), np.float32))
    return (norm_g.reshape(1, D_MODEL), w_qkv_bf, _perm_matrices()[0], jnp.asarray(ones, BF16),
            jnp.tile(q_gain, N_HEADS).reshape(1, ATTN_W), jnp.tile(k_gain, N_HEADS).reshape(1, ATTN_W))


def _rope_tables(pos):
    half = HEAD_DIM // 2
    inv = ROPE_THETA ** (-jnp.arange(half, dtype=F32) / half)
    ang = pos.astype(F32)[:, None] * inv[None, :]
    cos, sin = jnp.cos(ang), jnp.sin(ang)
    cos = jnp.concatenate([cos, cos, cos, cos], axis=1)
    sin = jnp.concatenate([-sin, sin, -sin, sin], axis=1)
    return cos, sin


def _p16_positions(seq):
    t = np.arange(seq).reshape(seq // ROW_TILE, P16, P16)
    return jnp.asarray(t.transpose(0, 2, 1).reshape(seq))


def _qkv_prompt(x2d, batch, seq, common):
    n_tb = seq // QKV_TILE
    n_a = seq // P16
    a_blk = QKV_TILE // P16
    cos, sin = _rope_tables(_p16_positions(seq))
    out_shape = ((jax.ShapeDtypeStruct((batch, P16, n_a, ATTN_W), F32),) * 3
                 + (jax.ShapeDtypeStruct((batch, P16, n_a, 2 * ATTN_W), BF16),) * 3
                 + tuple(jax.ShapeDtypeStruct((batch, 2, N_HEADS, HEAD_DIM, w), F32) for w in WINDOWS))
    g0_spec = pl.BlockSpec((None, P16, a_blk, ATTN_W), lambda b, t: (b, 0, t, 0))
    g12_spec = pl.BlockSpec((None, P16, a_blk, 2 * ATTN_W), lambda b, t: (b, 0, t, 0))
    tail_blk = tuple(min(w, QKV_TILE) for w in WINDOWS)
    tail_first = tuple(n_tb - w // blk for w, blk in zip(WINDOWS, tail_blk))

    def tail_spec(blk, first):
        return pl.BlockSpec((None, 2, N_HEADS, HEAD_DIM, blk),
                            lambda b, t: (b, 0, 0, 0, jnp.maximum(t - first, 0)))

    norm_g, w_bf, perm, ones, qg, kg = common
    return pl.pallas_call(
        functools.partial(_qkv_prompt_kernel, tail_first=tail_first),
        grid=(batch, n_tb),
        in_specs=[pl.BlockSpec((QKV_TILE, D_MODEL), lambda b, t: (b * n_tb + t, 0)),
                  _const_spec((1, D_MODEL)),
                  pl.BlockSpec((D_MODEL, 9 * ATTN_W), lambda b, t: (0, 0), pipeline_mode=pl.Buffered(1)),
                  _const_spec((ROW_TILE, ROW_TILE)), _const_spec((ROW_TILE, ROW_TILE)), _const_spec((256, 256)),
                  _const_spec((1, ATTN_W)), _const_spec((1, ATTN_W)),
                  pl.BlockSpec((QKV_TILE, 128), lambda b, t: (t, 0)),
                  pl.BlockSpec((QKV_TILE, 128), lambda b, t: (t, 0))],
        out_specs=((g0_spec,) * 3 + (g12_spec,) * 3
                   + tuple(tail_spec(blk, first) for blk, first in zip(tail_blk, tail_first))),
        out_shape=out_shape,
        compiler_params=_cparams("arbitrary", "arbitrary"),
        name="qkv_prompt",
    )(x2d, norm_g, w_bf, perm, _perm_matrices()[1], ones, qg, kg, cos, sin)


def _qkv_sample(x2d, pos, common):
    rows = x2d.shape[0]
    cos, sin = _rope_tables(pos)
    spec = pl.BlockSpec((ROW_TILE, 3 * ATTN_W), lambda i: (i, 0))
    return pl.pallas_call(
        _qkv_sample_kernel,
        grid=(rows // ROW_TILE,),
        in_specs=[pl.BlockSpec((ROW_TILE, D_MODEL), lambda i: (i, 0)),
                  _const_spec((1, D_MODEL)), _const_spec((D_MODEL, 9 * ATTN_W)),
                  _const_spec((ROW_TILE, ROW_TILE)), _const_spec((256, 256)),
                  _const_spec((1, ATTN_W)), _const_spec((1, ATTN_W)),
                  pl.BlockSpec((ROW_TILE, 128), lambda i: (i, 0)),
                  pl.BlockSpec((ROW_TILE, 128), lambda i: (i, 0))],
        out_specs=(spec,) * 3,
        out_shape=(jax.ShapeDtypeStruct((rows, 3 * ATTN_W), F32),) * 3,
        compiler_params=_cparams("arbitrary"),
        name="qkv_sample",
    )(x2d, *common, cos, sin)


ATTN_Q = 128


def _attn_geometry(dil):
    nres = P16 // dil
    qa = ATTN_Q // nres
    return nres, qa


def _attn_bias_tables(dil):
    nres, qa = _attn_geometry(dil)
    m = np.arange(ATTN_Q)
    n = np.arange(2 * ATTN_Q)
    qoff = nres * (m % qa) + m // qa
    koff = nres * (n % (2 * qa)) + n // (2 * qa)
    tables = []
    for delta in (0, qa):
        dist = qoff[:, None] - koff[None, :] + nres * delta
        bias = np.where((dist >= 0) & (dist <= KEYS_BACK), 0.0, NEG_BIG).astype(np.float32)
        tables.append(np.concatenate([bias, bias], axis=0))
    return jnp.asarray(np.stack(tables))


def _attn_group(bias_ref, q_ref, k_ref, v_ref, o_ref, l_ref, m_ref, dil, n_a):
    nres, qa = _attn_geometry(dil)
    n_atiles = n_a // qa
    head1 = lax.broadcasted_iota(jnp.int32, (ATTN_Q, 128), 1) >= HEAD_DIM
    zeros = jnp.zeros((ATTN_Q, 128), BF16)
    ones = jnp.ones((2 * ATTN_Q, 128), BF16)

    def tile(t, carry):
        c = t // n_atiles
        at = t % n_atiles
        a0 = pl.multiple_of(at * qa, qa)
        k0 = pl.multiple_of(jnp.maximum(at * qa - qa, 0), qa)
        rows = [c + dil * b for b in range(nres)]
        q = jnp.concatenate([q_ref[r, pl.ds(a0, qa), :] for r in rows], axis=0).astype(BF16)
        k = jnp.concatenate([k_ref[r, pl.ds(k0, 2 * qa), :] for r in rows], axis=0).astype(BF16)
        v = jnp.concatenate([v_ref[r, pl.ds(k0, 2 * qa), :] for r in rows], axis=0).astype(BF16)
        q = q * jnp.asarray(HEAD_DIM ** -0.5, BF16)
        q2 = jnp.concatenate([jnp.where(head1, zeros, q), jnp.where(head1, q, zeros)], axis=0)
        s = lax.dot_general(q2, k, (((1,), (1,)), ((), ())), preferred_element_type=F32)
        s = s + bias_ref[jnp.minimum(at, 1)]
        mx = jnp.max(s, axis=-1, keepdims=True)
        p = jnp.exp(s - mx).astype(BF16)
        ov = jnp.dot(p, jnp.concatenate([v, ones], axis=1), preferred_element_type=F32)
        o = jnp.where(head1, ov[ATTN_Q:, :128], ov[:ATTN_Q, :128])
        den = jnp.where(head1, ov[ATTN_Q:, 128:], ov[:ATTN_Q, 128:])
        m = jnp.where(head1, mx[ATTN_Q:], mx[:ATTN_Q])
        for b, r in enumerate(rows):
            o_ref[r, pl.ds(a0, qa), :] = o[b * qa:(b + 1) * qa]
            l_ref[r, pl.ds(a0, qa), :] = den[b * qa:(b + 1) * qa]
            m_ref[r, pl.ds(a0, qa), :] = m[b * qa:(b + 1) * qa]
        return carry

    lax.fori_loop(0, dil * n_atiles, tile, 0, unroll=32)


def _attn_prompt_kernel(b0, b1, b2, q0, k0, v0, q1, k1, v1, q2, k2, v2, o_ref, og_ref, lg_ref, mg_ref, *, n_a):
    groups = ((b0, q0, k0, v0), (b1, q1, k1, v1), (b2, q2, k2, v2))
    for g, (bias_ref, q_ref, k_ref, v_ref) in enumerate(groups):
        _attn_group(bias_ref, q_ref, k_ref, v_ref, og_ref.at[g], lg_ref.at[g], mg_ref.at[g], DILATIONS[g], n_a)

    def merge(r, carry):
        ms = [mg_ref[g, r] for g in range(3)]
        mx = jnp.maximum(jnp.maximum(ms[0], ms[1]), ms[2])
        es = [jnp.exp(m - mx) for m in ms]
        num = es[0] * og_ref[0, r] + es[1] * og_ref[1, r] + es[2] * og_ref[2, r]
        den = es[0] * lg_ref[0, r] + es[1] * lg_ref[1, r] + es[2] * lg_ref[2, r]
        o_ref[r] = (num / den).astype(BF16)
        return carry

    lax.fori_loop(0, P16, merge, 0)


def _attn_prompt(qkv0, qkv12):
    batch, _, n_a, _ = qkv0[0].shape
    biases = [_attn_bias_tables(dil) for dil in DILATIONS]
    in_specs = [_const_spec(b.shape) for b in biases]
    operands = list(biases)
    for g in range(3):
        in_specs += [pl.BlockSpec((None, P16, n_a, 128), lambda b, h, g=g: (b, 0, 0, max(g - 1, 0) * 4 + h))] * 3
        operands += list(qkv0 if g == 0 else qkv12)
    return pl.pallas_call(
        functools.partial(_attn_prompt_kernel, n_a=n_a),
        grid=(batch, 4),
        in_specs=in_specs,
        out_specs=pl.BlockSpec((None, P16, n_a, 128), lambda b, h: (b, 0, 0, h)),
        out_shape=jax.ShapeDtypeStruct((batch, P16, n_a, ATTN_W), BF16),
        scratch_shapes=[pltpu.VMEM((3, P16, n_a, 128), F32)] * 3,
        compiler_params=_cparams("arbitrary", "arbitrary"),
        name="attn_prompt",
    )(*operands)


def _attn_sample_kernel(q_ref, n_ref, c0_ref, c1_ref, c2_ref, o_ref, nc0_ref, nc1_ref, nc2_ref):
    n_new = q_ref.shape[2]
    nt = (((1,), (1,)), ((), ()))
    caches = ((c0_ref, nc0_ref), (c1_ref, nc1_ref), (c2_ref, nc2_ref))
    masks = []
    for (c_ref, _), dil in zip(caches, DILATIONS):
        win = c_ref.shape[-1]
        dist_c = (win + lax.broadcasted_iota(jnp.int32, (n_new, win), 0)
                  - lax.broadcasted_iota(jnp.int32, (n_new, win), 1))
        dist_n = (lax.broadcasted_iota(jnp.int32, (n_new, n_new), 0)
                  - lax.broadcasted_iota(jnp.int32, (n_new, n_new), 1))
        masks.append(((dist_c <= win) & ((dist_c & (dil - 1)) == 0),
                      (dist_n >= 0) & ((dist_n & (dil - 1)) == 0)))
    lane = lax.broadcasted_iota(jnp.int32, (HEAD_DIM, 128), 1)
    pad_rows = jnp.zeros((128 - n_new, 128), F32)
    for h in range(N_HEADS):
        outs, lses = [], []
        for g, (c_ref, nc_ref) in enumerate(caches):
            win = c_ref.shape[-1]
            ok_c, ok_n = masks[g]
            q = q_ref[g, h] * (HEAD_DIM ** -0.5)
            kt, vt = c_ref[0, h], c_ref[1, h]
            s_c = jnp.dot(q[:, :HEAD_DIM].astype(BF16), kt.astype(BF16), preferred_element_type=F32)
            s_n = lax.dot_general(q, n_ref[g, 0, h], nt, preferred_element_type=F32)
            s_c = jnp.where(ok_c, s_c, NEG_BIG)
            s_n = jnp.where(ok_n, s_n, NEG_BIG)
            mx = jnp.maximum(jnp.max(s_c, axis=-1, keepdims=True), jnp.max(s_n, axis=-1, keepdims=True))
            p_c = jnp.exp(s_c - mx)
            p_n = jnp.exp(s_n - mx)
            den = jnp.sum(p_c, axis=-1, keepdims=True) + jnp.sum(p_n, axis=-1, keepdims=True)
            o = (lax.dot_general(p_c.astype(BF16), vt.astype(BF16), nt, preferred_element_type=F32)
                 + jnp.dot(p_n, n_ref[g, 1, h], preferred_element_type=F32)[:, :HEAD_DIM])
            outs.append(o / den)
            lses.append(mx + jnp.log(den))
            for kv, old in enumerate((kt, vt)):
                rolled = pltpu.roll(old, win - n_new, 1)
                new_t = jnp.concatenate([pad_rows, n_ref[g, kv, h]], axis=0).T[:HEAD_DIM]
                if win > 128:
                    nc_ref[kv, h, :, :win - 128] = rolled[:, :win - 128]
                nc_ref[kv, h, :, win - 128:] = jnp.where(lane < 128 - n_new, rolled[:, win - 128:], new_t)
        mx = jnp.maximum(jnp.maximum(lses[0], lses[1]), lses[2])
        es = [jnp.exp(l - mx) for l in lses]
        o_ref[h] = (es[0] * outs[0] + es[1] * outs[1] + es[2] * outs[2]) / (es[0] + es[1] + es[2])


def _attn_sample(q_p, new_p, caches_t):
    batch, _, _, n_new, _ = q_p.shape
    c_specs = [pl.BlockSpec((None, 2, N_HEADS, HEAD_DIM, c.shape[-1]), lambda b: (b, 0, 0, 0, 0))
               for c in caches_t]
    return pl.pallas_call(
        _attn_sample_kernel,
        grid=(batch,),
        in_specs=[pl.BlockSpec((None, 3, N_HEADS, n_new, 128), lambda b: (b, 0, 0, 0, 0)),
                  pl.BlockSpec((None, 3, 2, N_HEADS, n_new, 128), lambda b: (b, 0, 0, 0, 0, 0))] + c_specs,
        out_specs=[pl.BlockSpec((None, N_HEADS, n_new, HEAD_DIM), lambda b: (b, 0, 0, 0))] + c_specs,
        out_shape=[jax.ShapeDtypeStruct((batch, N_HEADS, n_new, HEAD_DIM), F32)]
        + [jax.ShapeDtypeStruct(c.shape, F32) for c in caches_t],
        compiler_params=_cparams("arbitrary"),
        name="attn_sample",
    )(q_p, new_p, *caches_t)


def _ffn_block(x, g_ref, wg_ref, wu_ref, wd_ref, chunks):
    h = _rms(x, g_ref[...]).astype(BF16)
    acc = x
    for lo, hi in chunks:
        gate = jnp.dot(h, wg_ref[:, lo:hi], preferred_element_type=F32)
        up = jnp.dot(h, wu_ref[:, lo:hi], preferred_element_type=F32)
        act = (jax.nn.silu(gate) * up).astype(BF16)
        acc = acc + jnp.dot(act, wd_ref[lo:hi, :], preferred_element_type=F32)
    return acc


def _ffn_kernel(x_ref, g_ref, wg_ref, wu_ref, wd_ref, y_ref, *, chunks):
    y_ref[...] = _ffn_block(x_ref[...], g_ref, wg_ref, wu_ref, wd_ref, chunks)


def _wo_ffn_kernel(o_ref, x_ref, wo_ref, permt_ref, g_ref, wg_ref, wu_ref, wd_ref, y_ref, *, chunks):
    blocks = []
    for i in range(o_ref.shape[1] // P16):
        o = o_ref[:, i * P16:(i + 1) * P16, :].astype(F32).reshape(ROW_TILE, ATTN_W).astype(BF16)
        blocks.append(jnp.dot(permt_ref[...], o, preferred_element_type=F32).astype(BF16))
    x = x_ref[...] + jnp.dot(jnp.concatenate(blocks, axis=0), wo_ref[...], preferred_element_type=F32)
    y_ref[...] = _ffn_block(x, g_ref, wg_ref, wu_ref, wd_ref, chunks)


def _wo_rows_ffn_kernel(o_ref, x_ref, wo_ref, g_ref, wg_ref, wu_ref, wd_ref, y_ref, *, chunks):
    x = x_ref[...] + jnp.dot(o_ref[...].astype(BF16), wo_ref[...], preferred_element_type=F32)
    y_ref[...] = _ffn_block(x, g_ref, wg_ref, wu_ref, wd_ref, chunks)


def _glu_ffn_kernel(a_ref, x_ref, wglu_ref, g_ref, wg_ref, wu_ref, wd_ref, y_ref, *, chunks):
    z = jnp.dot(a_ref[...], wglu_ref[...], preferred_element_type=F32)
    x = x_ref[...] + z[:, :D_MODEL] * jax.nn.sigmoid(z[:, D_MODEL:])
    y_ref[...] = _ffn_block(x, g_ref, wg_ref, wu_ref, wd_ref, chunks)


def _ffn(x2d, norm_g, wg_bf, wu_bf, wd_bf, layer, tm, pre=None, seq=None):
    rows = x2d.shape[0]
    d_ff = wg_bf.shape[2]
    step = 1024
    chunks = tuple((lo, min(lo + step, d_ff)) for lo in range(0, d_ff, step))
    x_spec = pl.BlockSpec((tm, D_MODEL), lambda i: (i, 0))

    def resident(shape):
        return pl.BlockSpec(shape, lambda i: (0,) * len(shape), pipeline_mode=pl.Buffered(1))

    def of_layer(shape):
        return pl.BlockSpec((None,) + shape, lambda i: (layer, 0, 0), pipeline_mode=pl.Buffered(1))

    ffn_specs = [_const_spec((1, D_MODEL)), of_layer((D_MODEL, d_ff)), of_layer((D_MODEL, d_ff)),
                 of_layer((d_ff, D_MODEL))]
    ffn_args = (norm_g.reshape(1, D_MODEL), wg_bf, wu_bf, wd_bf)
    if pre is None:
        body, specs, args, name = _ffn_kernel, [x_spec], (x2d,), "ffn"
    elif pre[0] == "wo":
        n_tb = seq // tm
        o_spec = pl.BlockSpec((None, P16, tm // P16, ATTN_W), lambda i: (i // n_tb, 0, i % n_tb, 0))
        body, name = _wo_ffn_kernel, "wo_ffn"
        specs = [o_spec, x_spec, resident((ATTN_W, D_MODEL)), _const_spec((ROW_TILE, ROW_TILE))]
        args = (pre[1], x2d, pre[2], _perm_matrices()[1])
    elif pre[0] == "wo_rows":
        body, name = _wo_rows_ffn_kernel, "wo_rows_ffn"
        specs = [pl.BlockSpec((tm, ATTN_W), lambda i: (i, 0)), x_spec, resident((ATTN_W, D_MODEL))]
        args = (pre[1], x2d, pre[2])
    else:
        body, name = _glu_ffn_kernel, "glu_ffn"
        specs = [x_spec, x_spec, resident((D_MODEL, 2 * D_MODEL))]
        args = (pre[1], x2d, pre[2])
    return pl.pallas_call(
        functools.partial(body, chunks=chunks),
        grid=(rows // tm,),
        in_specs=specs + ffn_specs,
        out_specs=x_spec,
        out_shape=jax.ShapeDtypeStruct((rows, D_MODEL), F32),
        compiler_params=_cparams("arbitrary"),
        name=name,
    )(*args, *ffn_args)


def _norm_matmul_kernel(x_ref, g_ref, w_ref, y_ref):
    h = _rms(x_ref[...], g_ref[...]).astype(BF16)
    y_ref[...] = jnp.dot(h, w_ref[...], preferred_element_type=F32).astype(y_ref.dtype)


def _norm_matmul(x2d, norm_g, w_bf, tm):
    rows = x2d.shape[0]
    n = w_bf.shape[1]
    return pl.pallas_call(
        _norm_matmul_kernel,
        grid=(rows // tm,),
        in_specs=[pl.BlockSpec((tm, D_MODEL), lambda i: (i, 0)), _const_spec((1, D_MODEL)),
                  _const_spec((D_MODEL, n))],
        out_specs=pl.BlockSpec((tm, n), lambda i: (i, 0)),
        out_shape=jax.ShapeDtypeStruct((rows, n), BF16),
        compiler_params=_cparams("arbitrary"),
        name="ssm_in_proj",
    )(x2d, norm_g.reshape(1, D_MODEL), w_bf)


SSM_LANE_TILE = 128
SSM_TOK_TILE = SSM_LANE_TILE * SSM_CHUNK


def _ssm_in_kernel(x_ref, g_ref, w_ref, perm_ref, ut_ref):
    n_blk = SSM_TOK_TILE // ROW_TILE
    blocks = []
    for kb in range(n_blk):
        h = _rms(x_ref[kb * ROW_TILE:(kb + 1) * ROW_TILE, :], g_ref[...]).astype(BF16)
        blocks.append(jnp.dot(perm_ref[...], h, preferred_element_type=F32).astype(BF16))
    hp = jnp.concatenate([blk[j * P16:(j + 1) * P16] for j in range(SSM_CHUNK) for blk in blocks], axis=0)
    u = jnp.dot(hp, w_ref[...], preferred_element_type=F32)
    for j in range(SSM_CHUNK):
        ut_ref[j] = u[j * SSM_LANE_TILE:(j + 1) * SSM_LANE_TILE].T.astype(BF16)


def _ssm_in(x2d, norm_g, w_bf, perm):
    rows = x2d.shape[0]
    n_tiles = rows // SSM_TOK_TILE
    return pl.pallas_call(
        _ssm_in_kernel,
        grid=(n_tiles,),
        in_specs=[pl.BlockSpec((SSM_TOK_TILE, D_MODEL), lambda i: (i, 0)), _const_spec((1, D_MODEL)),
                  pl.BlockSpec((D_MODEL, D_MODEL), lambda i: (0, 0), pipeline_mode=pl.Buffered(1)),
                  _const_spec((ROW_TILE, ROW_TILE))],
        out_specs=pl.BlockSpec((SSM_CHUNK, D_MODEL, SSM_LANE_TILE), lambda i: (0, 0, i)),
        out_shape=jax.ShapeDtypeStruct((SSM_CHUNK, D_MODEL, n_tiles * SSM_LANE_TILE), BF16),
        compiler_params=_cparams("arbitrary"),
        name="ssm_in_t",
    )(x2d, norm_g.reshape(1, D_MODEL), w_bf, perm)


def _ssm_gate_kernel(yt_ref, permt_ref, g_ref):
    n_blk = SSM_TOK_TILE // ROW_TILE
    pieces = []
    for i in range(SSM_CHUNK):
        pieces.append(jax.nn.gelu(yt_ref[i]).T.astype(BF16))
    for kb in range(n_blk):
        blk = jnp.concatenate([p[kb * P16:(kb + 1) * P16] for p in pieces], axis=0)
        g_ref[kb * ROW_TILE:(kb + 1) * ROW_TILE, :] = jnp.dot(
            permt_ref[...], blk, preferred_element_type=F32).astype(BF16)


def _ssm_gate(yt, permt):
    n_tiles = yt.shape[2] // SSM_LANE_TILE
    return pl.pallas_call(
        _ssm_gate_kernel,
        grid=(n_tiles,),
        in_specs=[pl.BlockSpec((SSM_CHUNK, D_MODEL, SSM_LANE_TILE), lambda i: (0, 0, i)),
                  _const_spec((ROW_TILE, ROW_TILE))],
        out_specs=pl.BlockSpec((SSM_TOK_TILE, D_MODEL), lambda i: (i, 0)),
        out_shape=jax.ShapeDtypeStruct((n_tiles * SSM_TOK_TILE, D_MODEL), BF16),
        compiler_params=_cparams("arbitrary"),
        name="ssm_gate_t",
    )(yt, permt)


def _ssm_core_kernel(u_ref, mt_ref, ft_ref, et_ref, a_ref, x0_ref, y_ref, xf_ref, *, n_chunks, has_init):
    chunk, _, n_lanes = u_ref.shape
    u = u_ref[...].reshape(chunk * SSM_GROUP, n_lanes)
    y = jnp.dot(mt_ref[...], u, preferred_element_type=F32)
    s = jnp.dot(ft_ref[...], u, preferred_element_type=F32)
    sre, sim = s[:SSM_STATE], s[SSM_STATE:]
    are, aim = a_ref[0], a_ref[1]
    if n_chunks == 1:
        x0 = x0_ref[...]
        xre, xim = x0[:SSM_STATE], x0[SSM_STATE:]
        xf_ref[:SSM_STATE, :] = are * xre - aim * xim + sre
        xf_ref[SSM_STATE:, :] = are * xim + aim * xre + sim
        y = y + jnp.dot(et_ref[...], x0.astype(BF16), preferred_element_type=F32)
    else:
        assert not has_init and n_chunks % 128 == 0
        reps = n_lanes // 128
        pos = lax.broadcasted_iota(jnp.int32, (SSM_STATE, n_lanes), 1) & (n_chunks - 1)
        shift = 1
        while shift < n_chunks:
            keep = pos >= shift
            tre = jnp.where(keep, pltpu.roll(sre, shift, 1), 0.0)
            tim = jnp.where(keep, pltpu.roll(sim, shift, 1), 0.0)
            bre = jnp.concatenate([are] * reps, axis=1)
            bim = jnp.concatenate([aim] * reps, axis=1)
            sre, sim = sre + bre * tre - bim * tim, sim + bre * tim + bim * tre
            are, aim = are * are - aim * aim, 2.0 * are * aim
            shift *= 2
        lane128 = lax.broadcasted_iota(jnp.int32, (SSM_STATE, 128), 1)
        fre = jnp.zeros((SSM_STATE, 128), F32)
        fim = jnp.zeros((SSM_STATE, 128), F32)
        for b in range(n_lanes // n_chunks):
            lo, hi = (b + 1) * n_chunks - 128, (b + 1) * n_chunks
            last = lane128 == 127
            fre = jnp.where(lane128 == b, jnp.sum(jnp.where(last, sre[:, lo:hi], 0.0), axis=1, keepdims=True), fre)
            fim = jnp.where(lane128 == b, jnp.sum(jnp.where(last, sim[:, lo:hi], 0.0), axis=1, keepdims=True), fim)
        xf_ref[:SSM_STATE, :] = fre
        xf_ref[SSM_STATE:, :] = fim
        keep = pos >= 1
        xin = jnp.concatenate([jnp.where(keep, pltpu.roll(sre, 1, 1), 0.0),
                               jnp.where(keep, pltpu.roll(sim, 1, 1), 0.0)], axis=0)
        y = y + jnp.dot(et_ref[...], xin.astype(BF16), preferred_element_type=F32)
    y_ref[...] = y.reshape(chunk, SSM_GROUP, n_lanes)


def _ssm_core(ut, mats, x0t, n_chunks):
    mt, ft, et, a_pow = mats
    chunk, _, n_lanes = ut.shape
    groups = SSM_GROUPS
    has_init = x0t is not None
    a_lanes = n_lanes if n_chunks == 1 else 128
    a_b = jnp.broadcast_to(a_pow[:, :, :, None], a_pow.shape + (a_lanes,))
    if x0t is None:
        x0t = jnp.zeros((groups, 2 * SSM_STATE, 128), F32)
    xf_lanes = n_lanes if n_chunks == 1 else 128
    assert n_chunks == 1 or n_lanes // n_chunks <= 128

    def gspec(shape):
        return pl.BlockSpec((None,) + tuple(shape[1:]), lambda g: (g,) + (0,) * (len(shape) - 1))

    t_spec = pl.BlockSpec((chunk, SSM_GROUP, n_lanes), lambda g: (0, g, 0))
    return pl.pallas_call(
        functools.partial(_ssm_core_kernel, n_chunks=n_chunks, has_init=has_init),
        grid=(groups,),
        in_specs=[t_spec, gspec(mt.shape), gspec(ft.shape), gspec(et.shape), gspec(a_b.shape),
                  gspec(x0t.shape)],
        out_specs=(t_spec, gspec((groups, 2 * SSM_STATE, xf_lanes))),
        out_shape=(jax.ShapeDtypeStruct((chunk, D_MODEL, n_lanes), F32),
                   jax.ShapeDtypeStruct((groups, 2 * SSM_STATE, xf_lanes), F32)),
        compiler_params=_cparams("arbitrary"),
        name=f"ssm_core_{n_chunks}",
    )(ut, mt, ft, et, a_b, x0t)


def _cexp(mag_arg, ang):
    mag = jnp.exp(mag_arg)
    return mag * jnp.cos(ang), mag * jnp.sin(ang)


def _ssm_matrices_kernel(ar_row, ai_row, ar_col, ai_col, ld_ref, bre_ref, bim_ref, cre_ref, cim_ref,
                         d_ref, tile_ref, tile2_ref, mt_ref, ft_ref, et_ref, ap_ref, mth_ref, fth_ref, eth_ref,
                         *, chunk):
    hp = lax.Precision.HIGHEST
    rows = SSM_GROUP * chunk
    dt = jnp.exp(ld_ref[...])

    def powers(re, im, n):
        out = [(jnp.ones_like(re), jnp.zeros_like(im))]
        for _ in range(n):
            pr, pi = out[-1]
            out.append((pr * re - pi * im, pr * im + pi * re))
        return out

    pw = powers(*_cexp(ar_row[...] * dt, ai_row[...] * dt), chunk)
    cre = jnp.dot(cre_ref[...], tile2_ref[...], precision=hp, preferred_element_type=F32)
    cim = jnp.dot(cim_ref[...], tile2_ref[...], precision=hp, preferred_element_type=F32)
    cre, cim = jnp.concatenate([cre] * chunk, axis=0), jnp.concatenate([cim] * chunk, axis=0)

    def by_step(first):
        return tuple(jnp.concatenate([jnp.broadcast_to(pw[first + i][k], (SSM_GROUP, 128))
                                      for i in range(chunk)], axis=0) for k in range(2))

    pr, pi = by_step(0)
    r0_re, r0_im = (cre * pr - cim * pi)[:, :SSM_STATE], (cre * pi + cim * pr)[:, :SSM_STATE]
    pr, pi = by_step(1)
    lane = lax.broadcasted_iota(jnp.int32, (rows, 128), 1)
    et = jnp.where(lane < SSM_STATE, cre * pr - cim * pi, -(cre * pi + cim * pr)).astype(BF16)
    et_ref[...] = et
    eth_ref[...] = et[:rows // 2]
    for k, n in enumerate((chunk, chunk // 2)):
        ap_ref[2 * k:2 * k + 1, :] = pw[n][0]
        ap_ref[2 * k + 1:2 * k + 2, :] = pw[n][1]
    ar, ai = ar_col[...], ai_col[...]
    abr, abi = _cexp(ar * dt, ai * dt)
    inv = 1.0 / (ar * ar + ai * ai)
    zr = ((abr - 1.0) * ar + abi * ai) * inv
    zi = (abi * ar - (abr - 1.0) * ai) * inv
    bre, bim = bre_ref[...], bim_ref[...]
    bbr = jnp.dot(zr * bre - zi * bim, tile_ref[...], precision=hp, preferred_element_type=F32)
    bbi = jnp.dot(zr * bim + zi * bre, tile_ref[...], precision=hp, preferred_element_type=F32)
    pc = powers(abr, abi, chunk - 1)
    col_step = lax.broadcasted_iota(jnp.int32, (SSM_STATE, rows), 1) >> 4
    pr = jnp.zeros((SSM_STATE, rows), F32)
    pi = jnp.zeros((SSM_STATE, rows), F32)
    for j in range(chunk):
        pr = jnp.where(col_step == j, pc[chunk - 1 - j][0], pr)
        pi = jnp.where(col_step == j, pc[chunk - 1 - j][1], pi)
    ft = jnp.concatenate([pr * bbr - pi * bbi, pr * bbi + pi * bbr], axis=0).astype(BF16)
    ft_ref[...] = ft
    fth_ref[...] = ft[:, rows // 2:]
    tw = (jnp.dot(r0_re, bbr, precision=hp, preferred_element_type=F32)
          - jnp.dot(r0_im, bbi, precision=hp, preferred_element_type=F32))
    per_tile = 128 // SSM_GROUP
    row_idx = lax.broadcasted_iota(jnp.int32, (rows, 128), 0)
    col_idx = lax.broadcasted_iota(jnp.int32, (rows, 128), 1)
    row_blk, col_blk = row_idx >> 4, col_idx >> 4
    for lt in range(rows // 128):
        tw_lt = tw[:, lt * 128:(lt + 1) * 128]
        acc = jnp.zeros((rows, 128), F32)
        for jj in range(per_tile):
            j = lt * per_tile + jj
            shifted = tw_lt if j == 0 else pltpu.roll(tw_lt, SSM_GROUP * j, 0)
            acc = jnp.where(col_blk == jj, shifted, acc)
        keep = row_blk >= col_blk + lt * per_tile
        diag = row_idx == col_idx + lt * 128
        m_lt = (jnp.where(keep, acc, 0.0) + jnp.where(diag, d_ref[:, lt * 128:(lt + 1) * 128], 0.0)).astype(BF16)
        mt_ref[:, lt * 128:(lt + 1) * 128] = m_lt
        if lt == 0:
            mth_ref[...] = m_lt[:rows // 2]


def _ssm_matrices(a_re, a_im, log_dt, b_re, b_im, c_re, c_im, d_skip, chunk):
    groups, rows = SSM_GROUPS, SSM_GROUP * chunk
    half_rows = rows // 2

    def gspec(*shape):
        return pl.BlockSpec((None,) + shape, lambda g: (g,) + (0,) * len(shape))

    row2 = lambda a: jnp.tile(a, (1, 2)).reshape(groups, 1, 2 * SSM_STATE)
    col = lambda a: a.reshape(groups, SSM_STATE, 1)
    tile = jnp.asarray(np.tile(np.eye(SSM_GROUP, dtype=np.float32), (1, chunk)))
    tile2 = jnp.asarray(np.tile(np.eye(SSM_STATE, dtype=np.float32), (1, 2)))
    d_t = jnp.tile(d_skip.reshape(groups, 1, SSM_GROUP), (1, 1, chunk))
    mt, ft, et, a_pow, mt_h, ft_h, et_h = pl.pallas_call(
        functools.partial(_ssm_matrices_kernel, chunk=chunk),
        grid=(groups,),
        in_specs=[gspec(1, 128), gspec(1, 128), gspec(SSM_STATE, 1), gspec(SSM_STATE, 1), gspec(1, 1),
                  gspec(SSM_STATE, SSM_GROUP), gspec(SSM_STATE, SSM_GROUP),
                  gspec(SSM_GROUP, SSM_STATE), gspec(SSM_GROUP, SSM_STATE), gspec(1, rows),
                  _const_spec(tile.shape), _const_spec(tile2.shape)],
        out_specs=(gspec(rows, rows), gspec(2 * SSM_STATE, rows), gspec(rows, 2 * SSM_STATE), gspec(4, 128),
                   gspec(half_rows, half_rows), gspec(2 * SSM_STATE, half_rows), gspec(half_rows, 2 * SSM_STATE)),
        out_shape=(jax.ShapeDtypeStruct((groups, rows, rows), BF16),
                   jax.ShapeDtypeStruct((groups, 2 * SSM_STATE, rows), BF16),
                   jax.ShapeDtypeStruct((groups, rows, 2 * SSM_STATE), BF16),
                   jax.ShapeDtypeStruct((groups, 4, 128), F32),
                   jax.ShapeDtypeStruct((groups, half_rows, half_rows), BF16),
                   jax.ShapeDtypeStruct((groups, 2 * SSM_STATE, half_rows), BF16),
                   jax.ShapeDtypeStruct((groups, half_rows, 2 * SSM_STATE), BF16)),
        compiler_params=_cparams("arbitrary"),
        name="ssm_matrices",
    )(row2(a_re), row2(a_im), col(a_re), col(a_im), log_dt.reshape(groups, 1, 1),
      b_re, b_im, c_re, c_im, d_t, tile, tile2)
    a_pow = a_pow[:, :, :SSM_STATE]
    return (mt, ft, et, a_pow[:, 0:2]), (mt_h, ft_h, et_h, a_pow[:, 2:4])


def _gelu_kernel(y_ref, a_ref):
    a_ref[...] = jax.nn.gelu(y_ref[...]).astype(BF16)


def _gelu(y2d, tm):
    spec = pl.BlockSpec((tm, D_MODEL), lambda i: (i, 0))
    return pl.pallas_call(
        _gelu_kernel,
        grid=(y2d.shape[0] // tm,),
        in_specs=[spec],
        out_specs=spec,
        out_shape=jax.ShapeDtypeStruct(y2d.shape, BF16),
        compiler_params=_cparams("arbitrary"),
        name="ssm_gelu",
    )(y2d)


def _ssm_mixer_prompt(x2d, batch, seq, norm_g, w_in_bf, mats, perms):
    assert seq % SSM_TOK_TILE == 0
    n_chunks = seq // SSM_CHUNK
    ut = _ssm_in(x2d, norm_g, w_in_bf, perms[0])
    yt, xf = _ssm_core(ut, mats, None, n_chunks)
    xf = xf[:, :, :batch].reshape(SSM_GROUPS, 2, SSM_STATE, batch)
    return _ssm_gate(yt, perms[1]), xf.transpose(3, 0, 2, 1)


def _ssm_mixer_sample(x2d, batch, seq, norm_g, w_in_bf, mats, state0, tm):
    u = _norm_matmul(x2d, norm_g, w_in_bf, tm)
    ut = u.reshape(batch, seq, D_MODEL).transpose(1, 2, 0)
    x0t = state0.transpose(1, 3, 2, 0).reshape(SSM_GROUPS, 2 * SSM_STATE, batch)
    yt, xf = _ssm_core(ut, mats, x0t, 1)
    y = yt.transpose(2, 0, 1).reshape(batch * seq, D_MODEL)
    state = xf.reshape(SSM_GROUPS, 2, SSM_STATE, batch).transpose(3, 0, 2, 1)
    return _gelu(y, tm), state


def kernel(x_prompt, x_sample, cache_kv_w128, cache_kv_w512, cache_kv_w2048, state_ssm, norm_mix, norm_ffn,
           w_qkv, q_norm, k_norm, w_o, ssm_w_in, ssm_a_re, ssm_a_im, ssm_log_dt, ssm_b_re, ssm_b_im,
           ssm_c_re, ssm_c_im, ssm_d, ssm_w_glu, ffn_w_gate, ffn_w_up, ffn_w_down):
    batch, seq, _ = x_prompt.shape
    dec_batch, dec_seq, _ = x_sample.shape
    caches = (cache_kv_w128, cache_kv_w512, cache_kv_w2048)
    assert seq % ROW_TILE == 0 and seq >= WINDOWS[-1] and (dec_batch * dec_seq) % ROW_TILE == 0
    assert all(c.shape[2] == w for c, w in zip(caches, WINDOWS))
    xp = x_prompt.reshape(batch * seq, D_MODEL)
    xs = x_sample.reshape(dec_batch * dec_seq, D_MODEL)

    common = _qkv_common_inputs(norm_mix[0], w_qkv[0].astype(BF16), q_norm[0], k_norm[0])
    w_o_bf = w_o[0].astype(BF16)
    q0, k0, v0, q12, k12, v12, t0, t1, t2 = _qkv_prompt(xp, batch, seq, common)
    o_prompt = _attn_prompt((q0, k0, v0), (q12, k12, v12))
    kv_prompt = [tail.transpose(0, 4, 1, 2, 3)[None] for tail in (t0, t1, t2)]

    pos_s = jnp.tile(PAST_LEN + jnp.arange(dec_seq), dec_batch)
    qs, ks, vs = _qkv_sample(xs, pos_s, common)

    def heads_padded(a):
        a = a.reshape(dec_batch, dec_seq, 3, N_HEADS, HEAD_DIM).transpose(0, 2, 3, 1, 4)
        return jnp.pad(a, ((0, 0),) * 4 + ((0, 128 - HEAD_DIM),))

    q_p = heads_padded(qs)
    new_p = jnp.stack([heads_padded(ks), heads_padded(vs)], axis=2)
    o_s, *new_caches = _attn_sample(q_p, new_p, [c[0].transpose(0, 2, 3, 4, 1) for c in caches])
    o_sample = o_s.transpose(0, 2, 1, 3).reshape(dec_batch * dec_seq, ATTN_W)
    kv_sample = [nc.transpose(0, 4, 1, 2, 3)[None] for nc in new_caches]

    ffn_w = (ffn_w_gate.astype(BF16), ffn_w_up.astype(BF16), ffn_w_down.astype(BF16))
    xp = _ffn(xp, norm_ffn[0], *ffn_w, 0, QKV_TILE, pre=("wo", o_prompt, w_o_bf), seq=seq)
    xs = _ffn(xs, norm_ffn[0], *ffn_w, 0, ROW_TILE, pre=("wo_rows", o_sample, w_o_bf))

    ssm_p = (ssm_a_re[0], ssm_a_im[0], ssm_log_dt[0], ssm_b_re[0], ssm_b_im[0], ssm_c_re[0], ssm_c_im[0])
    w_in_bf = ssm_w_in[0].astype(BF16)
    w_glu_bf = ssm_w_glu[0].astype(BF16)
    assert 2 * dec_seq == SSM_CHUNK
    mats_full, mats_half = _ssm_matrices(*ssm_p, ssm_d[0], SSM_CHUNK)
    act_p, st_p = _ssm_mixer_prompt(xp, batch, seq, norm_mix[1], w_in_bf, mats_full, _perm_matrices())
    act_s, st_s = _ssm_mixer_sample(xs, dec_batch, dec_seq, norm_mix[1], w_in_bf, mats_half, state_ssm[0],
                                    ROW_TILE)
    xp = _ffn(xp, norm_ffn[1], *ffn_w, 1, QKV_TILE, pre=("glu", act_p, w_glu_bf))
    xs = _ffn(xs, norm_ffn[1], *ffn_w, 1, ROW_TILE, pre=("glu", act_s, w_glu_bf))

    return (xp.reshape(batch, seq, D_MODEL), xs.reshape(dec_batch, dec_seq, D_MODEL),
            kv_prompt[0], kv_prompt[1], kv_prompt[2], st_p[None],
            kv_sample[0], kv_sample[1], kv_sample[2], st_s[None])
```

```python
import functools
import math

import numpy as np
import jax
import jax.numpy as jnp
from jax import lax
from jax.experimental import pallas as pl
from jax.experimental.pallas import tpu as pltpu

F32 = jnp.float32
BF16 = jnp.bfloat16

D_MODEL = 1024
HEAD_DIM = 64
N_HEADS = 8
ATTN_W = N_HEADS * HEAD_DIM
DILATIONS = (1, 4, 16)
WINDOWS = (128, 512, 2048)
KEYS_BACK = 128
PAST_LEN = 16384
ROPE_THETA = 10000.0
RMS_EPS = 1e-6
SSM_GROUP = 16
SSM_GROUPS = D_MODEL // SSM_GROUP
SSM_STATE = 64
SSM_CHUNK = 16
ROW_TILE = 256
QKV_TILE = 512
P16 = 16
NEG_BIG = -1e30
VMEM_LIMIT = 56 * 1024 * 1024


def _cparams(*sem):
    return pltpu.CompilerParams(dimension_semantics=tuple(sem), vmem_limit_bytes=VMEM_LIMIT)


def _const_spec(shape):
    nd = len(shape)
    return pl.BlockSpec(shape, lambda *_: (0,) * nd)


def _perm_matrices():
    m = np.arange(ROW_TILE)
    to_p16 = np.zeros((ROW_TILE, ROW_TILE), np.float32)
    to_p16[m, P16 * (m % P16) + m // P16] = 1.0
    return jnp.asarray(to_p16, BF16), jnp.asarray(to_p16.T, BF16)


def _rms(x, gain):
    ms = jnp.mean(x * x, axis=-1, keepdims=True)
    return x * lax.rsqrt(ms + RMS_EPS) * gain


class _QkvSlabs:
    def __init__(self, x_ref, g_ref, w_ref, perm_ref, ones_ref, qg_ref, kg_ref, cos_ref, sin_ref, permute):
        rows = x_ref.shape[0]
        h = _rms(x_ref[...], g_ref[...]).astype(BF16)
        if permute:
            h = jnp.concatenate(
                [jnp.dot(perm_ref[...], h[i:i + ROW_TILE], preferred_element_type=F32).astype(BF16)
                 for i in range(0, rows, ROW_TILE)], axis=0)
        self.h, self.w_ref, self.ones = h, w_ref, ones_ref[...]
        self.gains = (qg_ref, kg_ref)
        self.cos = jnp.concatenate([cos_ref[...]] * 4, axis=1)
        self.sin = jnp.concatenate([sin_ref[...]] * 4, axis=1)
        lane = lax.broadcasted_iota(jnp.int32, (rows, ATTN_W), 1)
        self.first_half = (lane & (HEAD_DIM - 1)) < (HEAD_DIM // 2)

    def slab(self, s, g):
        col = (s * 3 + g) * ATTN_W
        y = jnp.dot(self.h, self.w_ref[:, col:col + ATTN_W], preferred_element_type=F32)
        if s == 2:
            return y
        yy = (y * y).astype(BF16)
        ss = jnp.concatenate([jnp.dot(yy[:, :256], self.ones, preferred_element_type=F32),
                              jnp.dot(yy[:, 256:], self.ones, preferred_element_type=F32)], axis=1)
        yn = y * lax.rsqrt(ss * (1.0 / HEAD_DIM) + RMS_EPS) * self.gains[s][...]
        swapped = jnp.where(self.first_half,
                            pltpu.roll(yn, ATTN_W - HEAD_DIM // 2, 1),
                            pltpu.roll(yn, HEAD_DIM // 2, 1))
        return yn * self.cos + swapped * self.sin


def _split3_bf16(x):
    hi = x.astype(BF16)
    r1 = x - hi.astype(F32)
    mid = r1.astype(BF16)
    lo = (r1 - mid.astype(F32)).astype(BF16)
    return hi, mid, lo


def _qkv_prompt_kernel(x_ref, g_ref, w_ref, perm_ref, permt_ref, ones_ref, qg_ref, kg_ref, cos_ref, sin_ref,
                       q0_ref, k0_ref, v0_ref, q_ref, k_ref, v_ref, t0_ref, t1_ref, t2_ref, *, tail_first):
    slabs = _QkvSlabs(x_ref, g_ref, w_ref, perm_ref, ones_ref, qg_ref, kg_ref, cos_ref, sin_ref, True)
    rows = x_ref.shape[0]
    tails = (t0_ref, t1_ref, t2_ref)
    t = pl.program_id(1)
    for g in range(3):
        y = [slabs.slab(s, g) for s in range(3)]
        for s, (ref0, ref) in enumerate(((q0_ref, q_ref), (k0_ref, k_ref), (v0_ref, v_ref))):
            for i in range(rows // ROW_TILE):
                blk = y[s][i * ROW_TILE:(i + 1) * ROW_TILE].reshape(P16, P16, ATTN_W)
                if g == 0:
                    ref0[:, i * P16:(i + 1) * P16, :] = blk
                else:
                    ref[:, i * P16:(i + 1) * P16, (g - 1) * ATTN_W:g * ATTN_W] = blk.astype(BF16)

        @pl.when(t >= tail_first[g])
        def _(g=g, y=y):
            keep = tails[g].shape[-1]
            for i in range((rows - keep) // ROW_TILE, rows // ROW_TILE):
                kv = jnp.concatenate([y[1][i * ROW_TILE:(i + 1) * ROW_TILE],
                                      y[2][i * ROW_TILE:(i + 1) * ROW_TILE]], axis=1)
                nat = sum(jnp.dot(permt_ref[...], part, preferred_element_type=F32)
                          for part in _split3_bf16(kv))
                lo = max(i * ROW_TILE, rows - keep)
                width = (i + 1) * ROW_TILE - lo
                tails[g][:, :, :, lo - (rows - keep):lo - (rows - keep) + width] = (
                    nat[ROW_TILE - width:].T.reshape(2, N_HEADS, HEAD_DIM, width))


def _qkv_sample_kernel(x_ref, g_ref, w_ref, perm_ref, ones_ref, qg_ref, kg_ref, cos_ref, sin_ref,
                       q_ref, k_ref, v_ref):
    slabs = _QkvSlabs(x_ref, g_ref, w_ref, perm_ref, ones_ref, qg_ref, kg_ref, cos_ref, sin_ref, False)
    for g in range(3):
        lo, hi = g * ATTN_W, (g + 1) * ATTN_W
        for s, ref in enumerate((q_ref, k_ref, v_ref)):
            ref[:, lo:hi] = slabs.slab(s, g)


def _qkv_common_inputs(norm_g, w_qkv_bf, q_gain, k_gain):
    ones = np.kron(np.eye(4, dtype=np.float32), np.ones((HEAD_DIM, HEAD_DIM), np.float32))
    return (norm_g.reshape(1, D_MODEL), w_qkv_bf, _perm_matrices()[0], jnp.asarray(ones, BF16),
            jnp.tile(q_gain, N_HEADS).reshape(1, ATTN_W), jnp.tile(k_gain, N_HEADS).reshape(1, ATTN_W))


def _rope_tables(pos):
    half = HEAD_DIM // 2
    inv = ROPE_THETA ** (-jnp.arange(half, dtype=F32) / half)
    ang = pos.astype(F32)[:, None] * inv[None, :]
    cos, sin = jnp.cos(ang), jnp.sin(ang)
    cos = jnp.concatenate([cos, cos, cos, cos], axis=1)
    sin = jnp.concatenate([-sin, sin, -sin, sin], axis=1)
    return cos, sin


def _p16_positions(seq):
    t = np.arange(seq).reshape(seq // ROW_TILE, P16, P16)
    return jnp.asarray(t.transpose(0, 2, 1).reshape(seq))


def _qkv_prompt(x2d, batch, seq, common):
    n_tb = seq // QKV_TILE
    n_a = seq // P16
    a_blk = QKV_TILE // P16
    cos, sin = _rope_tables(_p16_positions(seq))
    out_shape = ((jax.ShapeDtypeStruct((batch, P16, n_a, ATTN_W), F32),) * 3
                 + (jax.ShapeDtypeStruct((batch, P16, n_a, 2 * ATTN_W), BF16),) * 3
                 + tuple(jax.ShapeDtypeStruct((batch, 2, N_HEADS, HEAD_DIM, w), F32) for w in WINDOWS))
    g0_spec = pl.BlockSpec((None, P16, a_blk, ATTN_W), lambda b, t: (b, 0, t, 0))
    g12_spec = pl.BlockSpec((None, P16, a_blk, 2 * ATTN_W), lambda b, t: (b, 0, t, 0))
    tail_blk = tuple(min(w, QKV_TILE) for w in WINDOWS)
    tail_first = tuple(n_tb - w // blk for w, blk in zip(WINDOWS, tail_blk))

    def tail_spec(blk, first):
        return pl.BlockSpec((None, 2, N_HEADS, HEAD_DIM, blk),
                            lambda b, t: (b, 0, 0, 0, jnp.maximum(t - first, 0)))

    norm_g, w_bf, perm, ones, qg, kg = common
    return pl.pallas_call(
        functools.partial(_qkv_prompt_kernel, tail_first=tail_first),
        grid=(batch, n_tb),
        in_specs=[pl.BlockSpec((QKV_TILE, D_MODEL), lambda b, t: (b * n_tb + t, 0)),
                  _const_spec((1, D_MODEL)),
                  pl.BlockSpec((D_MODEL, 9 * ATTN_W), lambda b, t: (0, 0), pipeline_mode=pl.Buffered(1)),
                  _const_spec((ROW_TILE, ROW_TILE)), _const_spec((ROW_TILE, ROW_TILE)), _const_spec((256, 256)),
                  _const_spec((1, ATTN_W)), _const_spec((1, ATTN_W)),
                  pl.BlockSpec((QKV_TILE, 128), lambda b, t: (t, 0)),
                  pl.BlockSpec((QKV_TILE, 128), lambda b, t: (t, 0))],
        out_specs=((g0_spec,) * 3 + (g12_spec,) * 3
                   + tuple(tail_spec(blk, first) for blk, first in zip(tail_blk, tail_first))),
        out_shape=out_shape,
        compiler_params=_cparams("arbitrary", "arbitrary"),
        name="qkv_prompt",
    )(x2d, norm_g, w_bf, perm, _perm_matrices()[1], ones, qg, kg, cos, sin)


def _qkv_sample(x2d, pos, common):
    rows = x2d.shape[0]
    cos, sin = _rope_tables(pos)
    spec = pl.BlockSpec((ROW_TILE, 3 * ATTN_W), lambda i: (i, 0))
    return pl.pallas_call(
        _qkv_sample_kernel,
        grid=(rows // ROW_TILE,),
        in_specs=[pl.BlockSpec((ROW_TILE, D_MODEL), lambda i: (i, 0)),
                  _const_spec((1, D_MODEL)), _const_spec((D_MODEL, 9 * ATTN_W)),
                  _const_spec((ROW_TILE, ROW_TILE)), _const_spec((256, 256)),
                  _const_spec((1, ATTN_W)), _const_spec((1, ATTN_W)),
                  pl.BlockSpec((ROW_TILE, 128), lambda i: (i, 0)),
                  pl.BlockSpec((ROW_TILE, 128), lambda i: (i, 0))],
        out_specs=(spec,) * 3,
        out_shape=(jax.ShapeDtypeStruct((rows, 3 * ATTN_W), F32),) * 3,
        compiler_params=_cparams("arbitrary"),
        name="qkv_sample",
    )(x2d, *common, cos, sin)


ATTN_Q = 128


def _attn_geometry(dil):
    nres = P16 // dil
    qa = ATTN_Q // nres
    return nres, qa


def _attn_bias_tables(dil):
    nres, qa = _attn_geometry(dil)
    m = np.arange(ATTN_Q)
    n = np.arange(2 * ATTN_Q)
    qoff = nres * (m % qa) + m // qa
    koff = nres * (n % (2 * qa)) + n // (2 * qa)
    tables = []
    for delta in (0, qa):
        dist = qoff[:, None] - koff[None, :] + nres * delta
        bias = np.where((dist >= 0) & (dist <= KEYS_BACK), 0.0, NEG_BIG).astype(np.float32)
        tables.append(np.concatenate([bias, bias], axis=0))
    return jnp.asarray(np.stack(tables))


def _attn_group(bias_ref, q_ref, k_ref, v_ref, o_ref, l_ref, m_ref, dil, n_a):
    nres, qa = _attn_geometry(dil)
    n_atiles = n_a // qa
    head1 = lax.broadcasted_iota(jnp.int32, (ATTN_Q, 128), 1) >= HEAD_DIM
    zeros = jnp.zeros((ATTN_Q, 128), BF16)
    ones = jnp.ones((2 * ATTN_Q, 128), BF16)

    def tile(t, carry):
        c = t // n_atiles
        at = t % n_atiles
        a0 = pl.multiple_of(at * qa, qa)
        k0 = pl.multiple_of(jnp.maximum(at * qa - qa, 0), qa)
        rows = [c + dil * b for b in range(nres)]
        q = jnp.concatenate([q_ref[r, pl.ds(a0, qa), :] for r in rows], axis=0).astype(BF16)
        k = jnp.concatenate([k_ref[r, pl.ds(k0, 2 * qa), :] for r in rows], axis=0).astype(BF16)
        v = jnp.concatenate([v_ref[r, pl.ds(k0, 2 * qa), :] for r in rows], axis=0).astype(BF16)
        q = q * jnp.asarray(HEAD_DIM ** -0.5, BF16)
        q2 = jnp.concatenate([jnp.where(head1, zeros, q), jnp.where(head1, q, zeros)], axis=0)
        s = lax.dot_general(q2, k, (((1,), (1,)), ((), ())), preferred_element_type=F32)
        s = s + bias_ref[jnp.minimum(at, 1)]
        mx = jnp.max(s, axis=-1, keepdims=True)
        p = jnp.exp(s - mx).astype(BF16)
        ov = jnp.dot(p, jnp.concatenate([v, ones], axis=1), preferred_element_type=F32)
        o = jnp.where(head1, ov[ATTN_Q:, :128], ov[:ATTN_Q, :128])
        den = jnp.where(head1, ov[ATTN_Q:, 128:], ov[:ATTN_Q, 128:])
        m = jnp.where(head1, mx[ATTN_Q:], mx[:ATTN_Q])
        for b, r in enumerate(rows):
            o_ref[r, pl.ds(a0, qa), :] = o[b * qa:(b + 1) * qa]
            l_ref[r, pl.ds(a0, qa), :] = den[b * qa:(b + 1) * qa]
            m_ref[r, pl.ds(a0, qa), :] = m[b * qa:(b + 1) * qa]
        return carry

    lax.fori_loop(0, dil * n_atiles, tile, 0, unroll=32)


def _attn_prompt_kernel(b0, b1, b2, q0, k0, v0, q1, k1, v1, q2, k2, v2, o_ref, og_ref, lg_ref, mg_ref, *, n_a):
    groups = ((b0, q0, k0, v0), (b1, q1, k1, v1), (b2, q2, k2, v2))
    for g, (bias_ref, q_ref, k_ref, v_ref) in enumerate(groups):
        _attn_group(bias_ref, q_ref, k_ref, v_ref, og_ref.at[g], lg_ref.at[g], mg_ref.at[g], DILATIONS[g], n_a)

    def merge(r, carry):
        ms = [mg_ref[g, r] for g in range(3)]
        mx = jnp.maximum(jnp.maximum(ms[0], ms[1]), ms[2])
        es = [jnp.exp(m - mx) for m in ms]
        num = es[0] * og_ref[0, r] + es[1] * og_ref[1, r] + es[2] * og_ref[2, r]
        den = es[0] * lg_ref[0, r] + es[1] * lg_ref[1, r] + es[2] * lg_ref[2, r]
        o_ref[r] = (num / den).astype(BF16)
        return carry

    lax.fori_loop(0, P16, merge, 0)


def _attn_prompt(qkv0, qkv12):
    batch, _, n_a, _ = qkv0[0].shape
    biases = [_attn_bias_tables(dil) for dil in DILATIONS]
    in_specs = [_const_spec(b.shape) for b in biases]
    operands = list(biases)
    for g in range(3):
        in_specs += [pl.BlockSpec((None, P16, n_a, 128), lambda b, h, g=g: (b, 0, 0, max(g - 1, 0) * 4 + h))] * 3
        operands += list(qkv0 if g == 0 else qkv12)
    return pl.pallas_call(
        functools.partial(_attn_prompt_kernel, n_a=n_a),
        grid=(batch, 4),
        in_specs=in_specs,
        out_specs=pl.BlockSpec((None, P16, n_a, 128), lambda b, h: (b, 0, 0, h)),
        out_shape=jax.ShapeDtypeStruct((batch, P16, n_a, ATTN_W), BF16),
        scratch_shapes=[pltpu.VMEM((3, P16, n_a, 128), F32)] * 3,
        compiler_params=_cparams("arbitrary", "arbitrary"),
        name="attn_prompt",
    )(*operands)


def _attn_sample_item(q_ref, n_ref, c0_ref, c1_ref, c2_ref, o_ref, nc0_ref, nc1_ref, nc2_ref):
    n_new = q_ref.shape[2]
    nt = (((1,), (1,)), ((), ()))
    caches = ((c0_ref, nc0_ref), (c1_ref, nc1_ref), (c2_ref, nc2_ref))
    masks = []
    for (c_ref, _), dil in zip(caches, DILATIONS):
        win = c_ref.shape[-1]
        dist_c = (win + lax.broadcasted_iota(jnp.int32, (n_new, win), 0)
                  - lax.broadcasted_iota(jnp.int32, (n_new, win), 1))
        dist_n = (lax.broadcasted_iota(jnp.int32, (n_new, n_new), 0)
                  - lax.broadcasted_iota(jnp.int32, (n_new, n_new), 1))
        masks.append(((dist_c <= win) & ((dist_c & (dil - 1)) == 0),
                      (dist_n >= 0) & ((dist_n & (dil - 1)) == 0)))
    lane = lax.broadcasted_iota(jnp.int32, (HEAD_DIM, 128), 1)
    pad_rows = jnp.zeros((128 - n_new, 128), F32)
    for h in range(q_ref.shape[1]):
        outs, lses = [], []
        for g, (c_ref, nc_ref) in enumerate(caches):
            win = c_ref.shape[-1]
            ok_c, ok_n = masks[g]
            q = q_ref[g, h] * (HEAD_DIM ** -0.5)
            kt, vt = c_ref[0, h], c_ref[1, h]
            s_c = jnp.dot(q[:, :HEAD_DIM].astype(BF16), kt.astype(BF16), preferred_element_type=F32)
            s_n = lax.dot_general(q, n_ref[g, 0, h], nt, preferred_element_type=F32)
            s_c = jnp.where(ok_c, s_c, NEG_BIG)
            s_n = jnp.where(ok_n, s_n, NEG_BIG)
            mx = jnp.maximum(jnp.max(s_c, axis=-1, keepdims=True), jnp.max(s_n, axis=-1, keepdims=True))
            p_c = jnp.exp(s_c - mx)
            p_n = jnp.exp(s_n - mx)
            den = jnp.sum(p_c, axis=-1, keepdims=True) + jnp.sum(p_n, axis=-1, keepdims=True)
            o = (lax.dot_general(p_c.astype(BF16), vt.astype(BF16), nt, preferred_element_type=F32)
                 + jnp.dot(p_n, n_ref[g, 1, h], preferred_element_type=F32)[:, :HEAD_DIM])
            outs.append(o / den)
            lses.append(mx + jnp.log(den))
            for kv, old in enumerate((kt, vt)):
                rolled = pltpu.roll(old, win - n_new, 1)
                new_t = jnp.concatenate([pad_rows, n_ref[g, kv, h]], axis=0).T[:HEAD_DIM]
                if win > 128:
                    nc_ref[kv, h, :, :win - 128] = rolled[:, :win - 128]
                nc_ref[kv, h, :, win - 128:] = jnp.where(lane < 128 - n_new, rolled[:, win - 128:], new_t)
        mx = jnp.maximum(jnp.maximum(lses[0], lses[1]), lses[2])
        es = [jnp.exp(l - mx) for l in lses]
        o_ref[h] = (es[0] * outs[0] + es[1] * outs[1] + es[2] * outs[2]) / (es[0] + es[1] + es[2])


def _ffn_block(x, g_ref, wg_ref, wu_ref, wd_ref, chunks):
    h = _rms(x, g_ref[...]).astype(BF16)
    acc = x
    for lo, hi in chunks:
        gate = jnp.dot(h, wg_ref[:, lo:hi], preferred_element_type=F32)
        up = jnp.dot(h, wu_ref[:, lo:hi], preferred_element_type=F32)
        act = (jax.nn.silu(gate) * up).astype(BF16)
        acc = acc + jnp.dot(act, wd_ref[lo:hi, :], preferred_element_type=F32)
    return acc


def _mixer_residual(mode, refs):
    if mode == "wo":
        o_ref, x_ref, wo_ref, permt_ref = refs
        blocks = []
        for i in range(o_ref.shape[1] // P16):
            o = o_ref[:, i * P16:(i + 1) * P16, :].astype(F32).reshape(ROW_TILE, ATTN_W).astype(BF16)
            blocks.append(jnp.dot(permt_ref[...], o, preferred_element_type=F32).astype(BF16))
        return x_ref[...] + jnp.dot(jnp.concatenate(blocks, axis=0), wo_ref[...], preferred_element_type=F32)
    if mode == "wo_rows":
        o_ref, x_ref, wo_ref = refs
        return x_ref[...] + jnp.dot(o_ref[...].astype(BF16), wo_ref[...], preferred_element_type=F32)
    a_ref, x_ref, wglu_ref = refs
    z = jnp.dot(a_ref[...], wglu_ref[...], preferred_element_type=F32)
    return x_ref[...] + z[:, :D_MODEL] * jax.nn.sigmoid(z[:, D_MODEL:])


_MIXER_INPUTS = {"wo": 4, "wo_rows": 3, "glu": 3}
_RIDER_INPUTS = 5


def _mixer_ffn_kernel(*refs, mode, chunks, rider_inputs):
    n_pre = _MIXER_INPUTS[mode]
    g_ref, wg_ref, wu_ref, wd_ref = refs[n_pre:n_pre + 4]
    rest = refs[n_pre + 4:]
    y_ref = rest[rider_inputs]
    if rider_inputs:
        _attn_sample_item(*rest[:_RIDER_INPUTS], *rest[rider_inputs + 1:])
    y_ref[...] = _ffn_block(_mixer_residual(mode, refs[:n_pre]), g_ref, wg_ref, wu_ref, wd_ref, chunks)


def _ffn(x2d, norm_g, wg_bf, wu_bf, wd_bf, layer, tm, pre, seq=None, rider=None):
    rows = x2d.shape[0]
    d_ff = wg_bf.shape[2]
    step = 1024
    chunks = tuple((lo, min(lo + step, d_ff)) for lo in range(0, d_ff, step))
    x_spec = pl.BlockSpec((tm, D_MODEL), lambda i: (i, 0))

    def resident(shape):
        return pl.BlockSpec(shape, lambda i: (0,) * len(shape), pipeline_mode=pl.Buffered(1))

    def of_layer(shape):
        return pl.BlockSpec((None,) + shape, lambda i: (layer, 0, 0), pipeline_mode=pl.Buffered(1))

    ffn_specs = [_const_spec((1, D_MODEL)), of_layer((D_MODEL, d_ff)), of_layer((D_MODEL, d_ff)),
                 of_layer((d_ff, D_MODEL))]
    ffn_args = (norm_g.reshape(1, D_MODEL), wg_bf, wu_bf, wd_bf)
    mode = pre[0]
    if mode == "wo":
        n_tb = seq // tm
        o_spec = pl.BlockSpec((None, P16, tm // P16, ATTN_W), lambda i: (i // n_tb, 0, i % n_tb, 0))
        specs = [o_spec, x_spec, resident((ATTN_W, D_MODEL)), _const_spec((ROW_TILE, ROW_TILE))]
        args = (pre[1], x2d, pre[2], _perm_matrices()[1])
    elif mode == "wo_rows":
        specs = [pl.BlockSpec((tm, ATTN_W), lambda i: (i, 0)), x_spec, resident((ATTN_W, D_MODEL))]
        args = (pre[1], x2d, pre[2])
    else:
        specs = [x_spec, x_spec, resident((D_MODEL, 2 * D_MODEL))]
        args = (pre[1], x2d, pre[2])
    out_specs = [x_spec]
    out_shape = [jax.ShapeDtypeStruct((rows, D_MODEL), F32)]
    rider_specs, rider_args, aliases = [], (), {}
    if rider is not None:
        q_p, new_p, caches_t, hp_base, prev = rider
        batch, _, _, n_new, _ = q_p.shape
        assert rows // tm == 2 * batch
        item = lambda i: (i // 2, hp_base + i % 2)
        c_specs = [pl.BlockSpec((None, 2, 2, HEAD_DIM, c.shape[-1]), lambda i: (item(i)[0], 0, item(i)[1], 0, 0))
                   for c in caches_t]
        rider_specs = [pl.BlockSpec((None, 3, 2, n_new, 128), lambda i: (item(i)[0], 0, item(i)[1], 0, 0)),
                       pl.BlockSpec((None, 3, 2, 2, n_new, 128), lambda i: (item(i)[0], 0, 0, item(i)[1], 0, 0)),
                       *c_specs]
        rider_args = (q_p, new_p, *caches_t)
        out_specs += [pl.BlockSpec((None, 2, n_new, HEAD_DIM), lambda i: (item(i)[0], item(i)[1], 0, 0)), *c_specs]
        out_shape += [jax.ShapeDtypeStruct((batch, N_HEADS, n_new, HEAD_DIM), F32)]
        out_shape += [jax.ShapeDtypeStruct(c.shape, F32) for c in caches_t]
        if prev is not None:
            first = len(specs) + len(ffn_specs) + len(rider_specs)
            aliases = {first + k: 1 + k for k in range(len(prev))}
            rider_specs += [pl.BlockSpec(memory_space=pl.ANY)] * len(prev)
            rider_args += tuple(prev)
    outs = pl.pallas_call(
        functools.partial(_mixer_ffn_kernel, mode=mode, chunks=chunks, rider_inputs=len(rider_specs)),
        grid=(rows // tm,),
        in_specs=specs + ffn_specs + rider_specs,
        out_specs=out_specs,
        out_shape=out_shape,
        input_output_aliases=aliases,
        compiler_params=_cparams("arbitrary"),
        name=mode + "_ffn",
    )(*args, *ffn_args, *rider_args)
    return outs[0] if rider is None else outs


def _norm_matmul_kernel(x_ref, g_ref, w_ref, y_ref):
    h = _rms(x_ref[...], g_ref[...]).astype(BF16)
    y_ref[...] = jnp.dot(h, w_ref[...], preferred_element_type=F32).astype(y_ref.dtype)


def _norm_matmul(x2d, norm_g, w_bf, tm):
    rows = x2d.shape[0]
    n = w_bf.shape[1]
    return pl.pallas_call(
        _norm_matmul_kernel,
        grid=(rows // tm,),
        in_specs=[pl.BlockSpec((tm, D_MODEL), lambda i: (i, 0)), _const_spec((1, D_MODEL)),
                  _const_spec((D_MODEL, n))],
        out_specs=pl.BlockSpec((tm, n), lambda i: (i, 0)),
        out_shape=jax.ShapeDtypeStruct((rows, n), BF16),
        compiler_params=_cparams("arbitrary"),
        name="ssm_in_proj",
    )(x2d, norm_g.reshape(1, D_MODEL), w_bf)


SSM_LANE_TILE = 128
SSM_TOK_TILE = SSM_LANE_TILE * SSM_CHUNK


def _ssm_in_kernel(x_ref, g_ref, w_ref, perm_ref, ut_ref):
    n_blk = SSM_TOK_TILE // ROW_TILE
    blocks = []
    for kb in range(n_blk):
        h = _rms(x_ref[kb * ROW_TILE:(kb + 1) * ROW_TILE, :], g_ref[...]).astype(BF16)
        blocks.append(jnp.dot(perm_ref[...], h, preferred_element_type=F32).astype(BF16))
    hp = jnp.concatenate([blk[j * P16:(j + 1) * P16] for j in range(SSM_CHUNK) for blk in blocks], axis=0)
    u = jnp.dot(hp, w_ref[...], preferred_element_type=F32)
    for j in range(SSM_CHUNK):
        ut_ref[j] = u[j * SSM_LANE_TILE:(j + 1) * SSM_LANE_TILE].T.astype(BF16)


def _ssm_in(x2d, norm_g, w_bf, perm):
    rows = x2d.shape[0]
    n_tiles = rows // SSM_TOK_TILE
    return pl.pallas_call(
        _ssm_in_kernel,
        grid=(n_tiles,),
        in_specs=[pl.BlockSpec((SSM_TOK_TILE, D_MODEL), lambda i: (i, 0)), _const_spec((1, D_MODEL)),
                  pl.BlockSpec((D_MODEL, D_MODEL), lambda i: (0, 0), pipeline_mode=pl.Buffered(1)),
                  _const_spec((ROW_TILE, ROW_TILE))],
        out_specs=pl.BlockSpec((SSM_CHUNK, D_MODEL, SSM_LANE_TILE), lambda i: (0, 0, i)),
        out_shape=jax.ShapeDtypeStruct((SSM_CHUNK, D_MODEL, n_tiles * SSM_LANE_TILE), BF16),
        compiler_params=_cparams("arbitrary"),
        name="ssm_in_t",
    )(x2d, norm_g.reshape(1, D_MODEL), w_bf, perm)


def _ssm_gate_kernel(yt_ref, permt_ref, g_ref):
    n_blk = SSM_TOK_TILE // ROW_TILE
    pieces = []
    for i in range(SSM_CHUNK):
        pieces.append(jax.nn.gelu(yt_ref[i]).T.astype(BF16))
    for kb in range(n_blk):
        blk = jnp.concatenate([p[kb * P16:(kb + 1) * P16] for p in pieces], axis=0)
        g_ref[kb * ROW_TILE:(kb + 1) * ROW_TILE, :] = jnp.dot(
            permt_ref[...], blk, preferred_element_type=F32).astype(BF16)


def _ssm_gate(yt, permt):
    n_tiles = yt.shape[2] // SSM_LANE_TILE
    return pl.pallas_call(
        _ssm_gate_kernel,
        grid=(n_tiles,),
        in_specs=[pl.BlockSpec((SSM_CHUNK, D_MODEL, SSM_LANE_TILE), lambda i: (0, 0, i)),
                  _const_spec((ROW_TILE, ROW_TILE))],
        out_specs=pl.BlockSpec((SSM_TOK_TILE, D_MODEL), lambda i: (i, 0)),
        out_shape=jax.ShapeDtypeStruct((n_tiles * SSM_TOK_TILE, D_MODEL), BF16),
        compiler_params=_cparams("arbitrary"),
        name="ssm_gate_t",
    )(yt, permt)


def _ssm_core_kernel(u_ref, mt_ref, ft_ref, et_ref, a_ref, x0_ref, y_ref, xf_ref, *, n_chunks, has_init):
    chunk, _, n_lanes = u_ref.shape
    u = u_ref[...].reshape(chunk * SSM_GROUP, n_lanes)
    y = jnp.dot(mt_ref[...], u, preferred_element_type=F32)
    s = jnp.dot(ft_ref[...], u, preferred_element_type=F32)
    sre, sim = s[:SSM_STATE], s[SSM_STATE:]
    are, aim = a_ref[0], a_ref[1]
    if n_chunks == 1:
        x0 = x0_ref[...]
        xre, xim = x0[:SSM_STATE], x0[SSM_STATE:]
        xf_ref[:SSM_STATE, :] = are * xre - aim * xim + sre
        xf_ref[SSM_STATE:, :] = are * xim + aim * xre + sim
        y = y + jnp.dot(et_ref[...], x0.astype(BF16), preferred_element_type=F32)
    else:
        assert not has_init and n_chunks % 128 == 0
        reps = n_lanes // 128
        pos = lax.broadcasted_iota(jnp.int32, (SSM_STATE, n_lanes), 1) & (n_chunks - 1)
        shift = 1
        while shift < n_chunks:
            keep = pos >= shift
            tre = jnp.where(keep, pltpu.roll(sre, shift, 1), 0.0)
            tim = jnp.where(keep, pltpu.roll(sim, shift, 1), 0.0)
            bre = jnp.concatenate([are] * reps, axis=1)
            bim = jnp.concatenate([aim] * reps, axis=1)
            sre, sim = sre + bre * tre - bim * tim, sim + bre * tim + bim * tre
            are, aim = are * are - aim * aim, 2.0 * are * aim
            shift *= 2
        lane128 = lax.broadcasted_iota(jnp.int32, (SSM_STATE, 128), 1)
        fre = jnp.zeros((SSM_STATE, 128), F32)
        fim = jnp.zeros((SSM_STATE, 128), F32)
        for b in range(n_lanes // n_chunks):
            lo, hi = (b + 1) * n_chunks - 128, (b + 1) * n_chunks
            last = lane128 == 127
            fre = jnp.where(lane128 == b, jnp.sum(jnp.where(last, sre[:, lo:hi], 0.0), axis=1, keepdims=True), fre)
            fim = jnp.where(lane128 == b, jnp.sum(jnp.where(last, sim[:, lo:hi], 0.0), axis=1, keepdims=True), fim)
        xf_ref[:SSM_STATE, :] = fre
        xf_ref[SSM_STATE:, :] = fim
        keep = pos >= 1
        xin = jnp.concatenate([jnp.where(keep, pltpu.roll(sre, 1, 1), 0.0),
                               jnp.where(keep, pltpu.roll(sim, 1, 1), 0.0)], axis=0)
        y = y + jnp.dot(et_ref[...], xin.astype(BF16), preferred_element_type=F32)
    y_ref[...] = y.reshape(chunk, SSM_GROUP, n_lanes)


def _ssm_core(ut, mats, x0t, n_chunks):
    mt, ft, et, a_pow = mats
    chunk, _, n_lanes = ut.shape
    groups = SSM_GROUPS
    has_init = x0t is not None
    a_lanes = n_lanes if n_chunks == 1 else 128
    a_b = jnp.broadcast_to(a_pow[:, :, :, None], a_pow.shape + (a_lanes,))
    if x0t is None:
        x0t = jnp.zeros((groups, 2 * SSM_STATE, 128), F32)
    xf_lanes = n_lanes if n_chunks == 1 else 128
    assert n_chunks == 1 or n_lanes // n_chunks <= 128

    def gspec(shape):
        return pl.BlockSpec((None,) + tuple(shape[1:]), lambda g: (g,) + (0,) * (len(shape) - 1))

    t_spec = pl.BlockSpec((chunk, SSM_GROUP, n_lanes), lambda g: (0, g, 0))
    return pl.pallas_call(
        functools.partial(_ssm_core_kernel, n_chunks=n_chunks, has_init=has_init),
        grid=(groups,),
        in_specs=[t_spec, gspec(mt.shape), gspec(ft.shape), gspec(et.shape), gspec(a_b.shape),
                  gspec(x0t.shape)],
        out_specs=(t_spec, gspec((groups, 2 * SSM_STATE, xf_lanes))),
        out_shape=(jax.ShapeDtypeStruct((chunk, D_MODEL, n_lanes), F32),
                   jax.ShapeDtypeStruct((groups, 2 * SSM_STATE, xf_lanes), F32)),
        compiler_params=_cparams("arbitrary"),
        name=f"ssm_core_{n_chunks}",
    )(ut, mt, ft, et, a_b, x0t)


def _cexp(mag_arg, ang):
    mag = jnp.exp(mag_arg)
    return mag * jnp.cos(ang), mag * jnp.sin(ang)


def _ssm_matrices_kernel(ar_row, ai_row, ar_col, ai_col, ld_ref, bre_ref, bim_ref, cre_ref, cim_ref,
                         d_ref, tile_ref, tile2_ref, mt_ref, ft_ref, et_ref, ap_ref, mth_ref, fth_ref, eth_ref,
                         *, chunk):
    hp = lax.Precision.HIGHEST
    rows = SSM_GROUP * chunk
    dt = jnp.exp(ld_ref[...])

    def powers(re, im, n):
        out = [(jnp.ones_like(re), jnp.zeros_like(im))]
        for _ in range(n):
            pr, pi = out[-1]
            out.append((pr * re - pi * im, pr * im + pi * re))
        return out

    pw = powers(*_cexp(ar_row[...] * dt, ai_row[...] * dt), chunk)
    cre = jnp.dot(cre_ref[...], tile2_ref[...], precision=hp, preferred_element_type=F32)
    cim = jnp.dot(cim_ref[...], tile2_ref[...], precision=hp, preferred_element_type=F32)
    cre, cim = jnp.concatenate([cre] * chunk, axis=0), jnp.concatenate([cim] * chunk, axis=0)

    def by_step(first):
        return tuple(jnp.concatenate([jnp.broadcast_to(pw[first + i][k], (SSM_GROUP, 128))
                                      for i in range(chunk)], axis=0) for k in range(2))

    pr, pi = by_step(0)
    r0_re, r0_im = (cre * pr - cim * pi)[:, :SSM_STATE], (cre * pi + cim * pr)[:, :SSM_STATE]
    pr, pi = by_step(1)
    lane = lax.broadcasted_iota(jnp.int32, (rows, 128), 1)
    et = jnp.where(lane < SSM_STATE, cre * pr - cim * pi, -(cre * pi + cim * pr)).astype(BF16)
    et_ref[...] = et
    eth_ref[...] = et[:rows // 2]
    for k, n in enumerate((chunk, chunk // 2)):
        ap_ref[2 * k:2 * k + 1, :] = pw[n][0]
        ap_ref[2 * k + 1:2 * k + 2, :] = pw[n][1]
    ar, ai = ar_col[...], ai_col[...]
    abr, abi = _cexp(ar * dt, ai * dt)
    inv = 1.0 / (ar * ar + ai * ai)
    zr = ((abr - 1.0) * ar + abi * ai) * inv
    zi = (abi * ar - (abr - 1.0) * ai) * inv
    bre, bim = bre_ref[...], bim_ref[...]
    bbr = jnp.dot(zr * bre - zi * bim, tile_ref[...], precision=hp, preferred_element_type=F32)
    bbi = jnp.dot(zr * bim + zi * bre, tile_ref[...], precision=hp, preferred_element_type=F32)
    pc = powers(abr, abi, chunk - 1)
    col_step = lax.broadcasted_iota(jnp.int32, (SSM_STATE, rows), 1) >> 4
    pr = jnp.zeros((SSM_STATE, rows), F32)
    pi = jnp.zeros((SSM_STATE, rows), F32)
    for j in range(chunk):
        pr = jnp.where(col_step == j, pc[chunk - 1 - j][0], pr)
        pi = jnp.where(col_step == j, pc[chunk - 1 - j][1], pi)
    ft = jnp.concatenate([pr * bbr - pi * bbi, pr * bbi + pi * bbr], axis=0).astype(BF16)
    ft_ref[...] = ft
    fth_ref[...] = ft[:, rows // 2:]
    tw = (jnp.dot(r0_re, bbr, precision=hp, preferred_element_type=F32)
          - jnp.dot(r0_im, bbi, precision=hp, preferred_element_type=F32))
    per_tile = 128 // SSM_GROUP
    row_idx = lax.broadcasted_iota(jnp.int32, (rows, 128), 0)
    col_idx = lax.broadcasted_iota(jnp.int32, (rows, 128), 1)
    row_blk, col_blk = row_idx >> 4, col_idx >> 4
    for lt in range(rows // 128):
        tw_lt = tw[:, lt * 128:(lt + 1) * 128]
        acc = jnp.zeros((rows, 128), F32)
        for jj in range(per_tile):
            j = lt * per_tile + jj
            shifted = tw_lt if j == 0 else pltpu.roll(tw_lt, SSM_GROUP * j, 0)
            acc = jnp.where(col_blk == jj, shifted, acc)
        keep = row_blk >= col_blk + lt * per_tile
        diag = row_idx == col_idx + lt * 128
        m_lt = (jnp.where(keep, acc, 0.0) + jnp.where(diag, d_ref[:, lt * 128:(lt + 1) * 128], 0.0)).astype(BF16)
        mt_ref[:, lt * 128:(lt + 1) * 128] = m_lt
        if lt == 0:
            mth_ref[...] = m_lt[:rows // 2]


def _ssm_matrices(a_re, a_im, log_dt, b_re, b_im, c_re, c_im, d_skip, chunk):
    groups, rows = SSM_GROUPS, SSM_GROUP * chunk
    half_rows = rows // 2

    def gspec(*shape):
        return pl.BlockSpec((None,) + shape, lambda g: (g,) + (0,) * len(shape))

    row2 = lambda a: jnp.tile(a, (1, 2)).reshape(groups, 1, 2 * SSM_STATE)
    col = lambda a: a.reshape(groups, SSM_STATE, 1)
    tile = jnp.asarray(np.tile(np.eye(SSM_GROUP, dtype=np.float32), (1, chunk)))
    tile2 = jnp.asarray(np.tile(np.eye(SSM_STATE, dtype=np.float32), (1, 2)))
    d_t = jnp.tile(d_skip.reshape(groups, 1, SSM_GROUP), (1, 1, chunk))
    mt, ft, et, a_pow, mt_h, ft_h, et_h = pl.pallas_call(
        functools.partial(_ssm_matrices_kernel, chunk=chunk),
        grid=(groups,),
        in_specs=[gspec(1, 128), gspec(1, 128), gspec(SSM_STATE, 1), gspec(SSM_STATE, 1), gspec(1, 1),
                  gspec(SSM_STATE, SSM_GROUP), gspec(SSM_STATE, SSM_GROUP),
                  gspec(SSM_GROUP, SSM_STATE), gspec(SSM_GROUP, SSM_STATE), gspec(1, rows),
                  _const_spec(tile.shape), _const_spec(tile2.shape)],
        out_specs=(gspec(rows, rows), gspec(2 * SSM_STATE, rows), gspec(rows, 2 * SSM_STATE), gspec(4, 128),
                   gspec(half_rows, half_rows), gspec(2 * SSM_STATE, half_rows), gspec(half_rows, 2 * SSM_STATE)),
        out_shape=(jax.ShapeDtypeStruct((groups, rows, rows), BF16),
                   jax.ShapeDtypeStruct((groups, 2 * SSM_STATE, rows), BF16),
                   jax.ShapeDtypeStruct((groups, rows, 2 * SSM_STATE), BF16),
                   jax.ShapeDtypeStruct((groups, 4, 128), F32),
                   jax.ShapeDtypeStruct((groups, half_rows, half_rows), BF16),
                   jax.ShapeDtypeStruct((groups, 2 * SSM_STATE, half_rows), BF16),
                   jax.ShapeDtypeStruct((groups, half_rows, 2 * SSM_STATE), BF16)),
        compiler_params=_cparams("arbitrary"),
        name="ssm_matrices",
    )(row2(a_re), row2(a_im), col(a_re), col(a_im), log_dt.reshape(groups, 1, 1),
      b_re, b_im, c_re, c_im, d_t, tile, tile2)
    a_pow = a_pow[:, :, :SSM_STATE]
    return (mt, ft, et, a_pow[:, 0:2]), (mt_h, ft_h, et_h, a_pow[:, 2:4])


def _gelu_kernel(y_ref, a_ref):
    a_ref[...] = jax.nn.gelu(y_ref[...]).astype(BF16)


def _gelu(y2d, tm):
    spec = pl.BlockSpec((tm, D_MODEL), lambda i: (i, 0))
    return pl.pallas_call(
        _gelu_kernel,
        grid=(y2d.shape[0] // tm,),
        in_specs=[spec],
        out_specs=spec,
        out_shape=jax.ShapeDtypeStruct(y2d.shape, BF16),
        compiler_params=_cparams("arbitrary"),
        name="ssm_gelu",
    )(y2d)


def _ssm_mixer_prompt(x2d, batch, seq, norm_g, w_in_bf, mats, perms):
    assert seq % SSM_TOK_TILE == 0
    n_chunks = seq // SSM_CHUNK
    ut = _ssm_in(x2d, norm_g, w_in_bf, perms[0])
    yt, xf = _ssm_core(ut, mats, None, n_chunks)
    xf = xf[:, :, :batch].reshape(SSM_GROUPS, 2, SSM_STATE, batch)
    return _ssm_gate(yt, perms[1]), xf.transpose(3, 0, 2, 1)


def _ssm_mixer_sample(x2d, batch, seq, norm_g, w_in_bf, mats, state0, tm):
    u = _norm_matmul(x2d, norm_g, w_in_bf, tm)
    ut = u.reshape(batch, seq, D_MODEL).transpose(1, 2, 0)
    x0t = state0.transpose(1, 3, 2, 0).reshape(SSM_GROUPS, 2 * SSM_STATE, batch)
    yt, xf = _ssm_core(ut, mats, x0t, 1)
    y = yt.transpose(2, 0, 1).reshape(batch * seq, D_MODEL)
    state = xf.reshape(SSM_GROUPS, 2, SSM_STATE, batch).transpose(3, 0, 2, 1)
    return _gelu(y, tm), state


def kernel(x_prompt, x_sample, cache_kv_w128, cache_kv_w512, cache_kv_w2048, state_ssm, norm_mix, norm_ffn,
           w_qkv, q_norm, k_norm, w_o, ssm_w_in, ssm_a_re, ssm_a_im, ssm_log_dt, ssm_b_re, ssm_b_im,
           ssm_c_re, ssm_c_im, ssm_d, ssm_w_glu, ffn_w_gate, ffn_w_up, ffn_w_down):
    batch, seq, _ = x_prompt.shape
    dec_batch, dec_seq, _ = x_sample.shape
    caches = (cache_kv_w128, cache_kv_w512, cache_kv_w2048)
    assert seq % ROW_TILE == 0 and seq >= WINDOWS[-1] and (dec_batch * dec_seq) % ROW_TILE == 0
    assert all(c.shape[2] == w for c, w in zip(caches, WINDOWS))
    xp = x_prompt.reshape(batch * seq, D_MODEL)
    xs = x_sample.reshape(dec_batch * dec_seq, D_MODEL)

    common = _qkv_common_inputs(norm_mix[0], w_qkv[0].astype(BF16), q_norm[0], k_norm[0])
    w_o_bf = w_o[0].astype(BF16)
    q0, k0, v0, q12, k12, v12, t0, t1, t2 = _qkv_prompt(xp, batch, seq, common)
    o_prompt = _attn_prompt((q0, k0, v0), (q12, k12, v12))
    kv_prompt = [tail.transpose(0, 4, 1, 2, 3)[None] for tail in (t0, t1, t2)]

    pos_s = jnp.tile(PAST_LEN + jnp.arange(dec_seq), dec_batch)
    qs, ks, vs = _qkv_sample(xs, pos_s, common)

    def heads_padded(a):
        a = a.reshape(dec_batch, dec_seq, 3, N_HEADS, HEAD_DIM).transpose(0, 2, 3, 1, 4)
        return jnp.pad(a, ((0, 0),) * 4 + ((0, 128 - HEAD_DIM),))

    q_p = heads_padded(qs)
    new_p = jnp.stack([heads_padded(ks), heads_padded(vs)], axis=2)
    caches_t = [c[0].transpose(0, 2, 3, 4, 1) for c in caches]

    ffn_w = (ffn_w_gate.astype(BF16), ffn_w_up.astype(BF16), ffn_w_down.astype(BF16))
    xp, *sample_attn = _ffn(xp, norm_ffn[0], *ffn_w, 0, QKV_TILE, pre=("wo", o_prompt, w_o_bf), seq=seq,
                            rider=(q_p, new_p, caches_t, 0, None))

    ssm_p = (ssm_a_re[0], ssm_a_im[0], ssm_log_dt[0], ssm_b_re[0], ssm_b_im[0], ssm_c_re[0], ssm_c_im[0])
    w_in_bf = ssm_w_in[0].astype(BF16)
    w_glu_bf = ssm_w_glu[0].astype(BF16)
    assert 2 * dec_seq == SSM_CHUNK
    mats_full, mats_half = _ssm_matrices(*ssm_p, ssm_d[0], SSM_CHUNK)
    act_p, st_p = _ssm_mixer_prompt(xp, batch, seq, norm_mix[1], w_in_bf, mats_full, _perm_matrices())
    xp, o_s, *new_caches = _ffn(xp, norm_ffn[1], *ffn_w, 1, QKV_TILE, pre=("glu", act_p, w_glu_bf),
                                rider=(q_p, new_p, caches_t, 2, sample_attn))
    kv_sample = [nc.transpose(0, 4, 1, 2, 3)[None] for nc in new_caches]

    o_sample = o_s.transpose(0, 2, 1, 3).reshape(dec_batch * dec_seq, ATTN_W)
    xs = _ffn(xs, norm_ffn[0], *ffn_w, 0, ROW_TILE, pre=("wo_rows", o_sample, w_o_bf))
    act_s, st_s = _ssm_mixer_sample(xs, dec_batch, dec_seq, norm_mix[1], w_in_bf, mats_half, state_ssm[0],
                                    ROW_TILE)
    xs = _ffn(xs, norm_ffn[1], *ffn_w, 1, ROW_TILE, pre=("glu", act_s, w_glu_bf))

    return (xp.reshape(batch, seq, D_MODEL), xs.reshape(dec_batch, dec_seq, D_MODEL),
            kv_prompt[0], kv_prompt[1], kv_prompt[2], st_p[None],
            kv_sample[0], kv_sample[1], kv_sample[2], st_s[None])
```

```python
import functools
import math

import numpy as np
import jax
import jax.numpy as jnp
from jax import lax
from jax.experimental import pallas as pl
from jax.experimental.pallas import tpu as pltpu

F32 = jnp.float32
BF16 = jnp.bfloat16

D_MODEL = 1024
HEAD_DIM = 64
N_HEADS = 8
ATTN_W = N_HEADS * HEAD_DIM
DILATIONS = (1, 4, 16)
WINDOWS = (128, 512, 2048)
KEYS_BACK = 128
PAST_LEN = 16384
ROPE_THETA = 10000.0
RMS_EPS = 1e-6
SSM_GROUP = 16
SSM_GROUPS = D_MODEL // SSM_GROUP
SSM_STATE = 64
SSM_CHUNK = 16
ROW_TILE = 256
QKV_TILE = 512
P16 = 16
NEG_BIG = -1e30
VMEM_LIMIT = 56 * 1024 * 1024


def _cparams(*sem):
    return pltpu.CompilerParams(dimension_semantics=tuple(sem), vmem_limit_bytes=VMEM_LIMIT)


def _const_spec(shape):
    nd = len(shape)
    return pl.BlockSpec(shape, lambda *_: (0,) * nd)


def _perm_matrices():
    m = np.arange(ROW_TILE)
    to_p16 = np.zeros((ROW_TILE, ROW_TILE), np.float32)
    to_p16[m, P16 * (m % P16) + m // P16] = 1.0
    return jnp.asarray(to_p16, BF16), jnp.asarray(to_p16.T, BF16)


def _rms(x, gain):
    ms = jnp.mean(x * x, axis=-1, keepdims=True)
    return x * lax.rsqrt(ms + RMS_EPS) * gain


class _QkvSlabs:
    def __init__(self, x_ref, g_ref, w_ref, perm_ref, ones_ref, qg_ref, kg_ref, cos_ref, sin_ref, permute):
        rows = x_ref.shape[0]
        h = _rms(x_ref[...], g_ref[...]).astype(BF16)
        if permute:
            h = jnp.concatenate(
                [jnp.dot(perm_ref[...], h[i:i + ROW_TILE], preferred_element_type=F32).astype(BF16)
                 for i in range(0, rows, ROW_TILE)], axis=0)
        self.h, self.w_ref, self.ones = h, w_ref, ones_ref[...]
        self.gains = (qg_ref, kg_ref)
        self.cos = jnp.concatenate([cos_ref[...]] * 4, axis=1)
        self.sin = jnp.concatenate([sin_ref[...]] * 4, axis=1)
        lane = lax.broadcasted_iota(jnp.int32, (rows, ATTN_W), 1)
        self.first_half = (lane & (HEAD_DIM - 1)) < (HEAD_DIM // 2)

    def slab(self, s, g):
        col = (s * 3 + g) * ATTN_W
        y = jnp.dot(self.h, self.w_ref[:, col:col + ATTN_W], preferred_element_type=F32)
        if s == 2:
            return y
        yy = (y * y).astype(BF16)
        ss = jnp.concatenate([jnp.dot(yy[:, :256], self.ones, preferred_element_type=F32),
                              jnp.dot(yy[:, 256:], self.ones, preferred_element_type=F32)], axis=1)
        yn = y * lax.rsqrt(ss * (1.0 / HEAD_DIM) + RMS_EPS) * self.gains[s][...]
        swapped = jnp.where(self.first_half,
                            pltpu.roll(yn, ATTN_W - HEAD_DIM // 2, 1),
                            pltpu.roll(yn, HEAD_DIM // 2, 1))
        return yn * self.cos + swapped * self.sin


def _split3_bf16(x):
    hi = x.astype(BF16)
    r1 = x - hi.astype(F32)
    mid = r1.astype(BF16)
    lo = (r1 - mid.astype(F32)).astype(BF16)
    return hi, mid, lo


def _qkv_prompt_kernel(x_ref, g_ref, w_ref, perm_ref, permt_ref, ones_ref, qg_ref, kg_ref, cos_ref, sin_ref,
                       q0_ref, k0_ref, v0_ref, q_ref, k_ref, v_ref, t0_ref, t1_ref, t2_ref, *, tail_first):
    slabs = _QkvSlabs(x_ref, g_ref, w_ref, perm_ref, ones_ref, qg_ref, kg_ref, cos_ref, sin_ref, True)
    rows = x_ref.shape[0]
    tails = (t0_ref, t1_ref, t2_ref)
    t = pl.program_id(1)
    for g in range(3):
        y = [slabs.slab(s, g) for s in range(3)]
        for s, (ref0, ref) in enumerate(((q0_ref, q_ref), (k0_ref, k_ref), (v0_ref, v_ref))):
            for i in range(rows // ROW_TILE):
                blk = y[s][i * ROW_TILE:(i + 1) * ROW_TILE].reshape(P16, P16, ATTN_W)
                if g == 0:
                    ref0[:, i * P16:(i + 1) * P16, :] = blk
                else:
                    ref[:, i * P16:(i + 1) * P16, (g - 1) * ATTN_W:g * ATTN_W] = blk.astype(BF16)

        @pl.when(t >= tail_first[g])
        def _(g=g, y=y):
            keep = tails[g].shape[-1]
            for i in range((rows - keep) // ROW_TILE, rows // ROW_TILE):
                kv = jnp.concatenate([y[1][i * ROW_TILE:(i + 1) * ROW_TILE],
                                      y[2][i * ROW_TILE:(i + 1) * ROW_TILE]], axis=1)
                nat = sum(jnp.dot(permt_ref[...], part, preferred_element_type=F32)
                          for part in _split3_bf16(kv))
                lo = max(i * ROW_TILE, rows - keep)
                width = (i + 1) * ROW_TILE - lo
                tails[g][:, :, :, lo - (rows - keep):lo - (rows - keep) + width] = (
                    nat[ROW_TILE - width:].T.reshape(2, N_HEADS, HEAD_DIM, width))


def _qkv_sample_kernel(x_ref, g_ref, w_ref, perm_ref, ones_ref, qg_ref, kg_ref, cos_ref, sin_ref,
                       q_ref, k_ref, v_ref):
    slabs = _QkvSlabs(x_ref, g_ref, w_ref, perm_ref, ones_ref, qg_ref, kg_ref, cos_ref, sin_ref, False)
    for g in range(3):
        lo, hi = g * ATTN_W, (g + 1) * ATTN_W
        for s, ref in enumerate((q_ref, k_ref, v_ref)):
            ref[:, lo:hi] = slabs.slab(s, g)


def _qkv_common_inputs(norm_g, w_qkv_bf, q_gain, k_gain):
    ones = np.kron(np.eye(4, dtype=np.float32), np.ones((HEAD_DIM, HEAD_DIM), np.float32))
    return (norm_g.reshape(1, D_MODEL), w_qkv_bf, _perm_matrices()[0], jnp.asarray(ones, BF16),
            jnp.tile(q_gain, N_HEADS).reshape(1, ATTN_W), jnp.tile(k_gain, N_HEADS).reshape(1, ATTN_W))


def _rope_tables(pos):
    half = HEAD_DIM // 2
    inv = ROPE_THETA ** (-jnp.arange(half, dtype=F32) / half)
    ang = pos.astype(F32)[:, None] * inv[None, :]
    cos, sin = jnp.cos(ang), jnp.sin(ang)
    cos = jnp.concatenate([cos, cos, cos, cos], axis=1)
    sin = jnp.concatenate([-sin, sin, -sin, sin], axis=1)
    return cos, sin


def _p16_positions(seq):
    t = np.arange(seq).reshape(seq // ROW_TILE, P16, P16)
    return jnp.asarray(t.transpose(0, 2, 1).reshape(seq))


def _qkv_prompt(x2d, batch, seq, common):
    n_tb = seq // QKV_TILE
    n_a = seq // P16
    a_blk = QKV_TILE // P16
    cos, sin = _rope_tables(_p16_positions(seq))
    out_shape = ((jax.ShapeDtypeStruct((batch, P16, n_a, ATTN_W), F32),) * 3
                 + (jax.ShapeDtypeStruct((batch, P16, n_a, 2 * ATTN_W), BF16),) * 3
                 + tuple(jax.ShapeDtypeStruct((batch, 2, N_HEADS, HEAD_DIM, w), F32) for w in WINDOWS))
    g0_spec = pl.BlockSpec((None, P16, a_blk, ATTN_W), lambda b, t: (b, 0, t, 0))
    g12_spec = pl.BlockSpec((None, P16, a_blk, 2 * ATTN_W), lambda b, t: (b, 0, t, 0))
    tail_blk = tuple(min(w, QKV_TILE) for w in WINDOWS)
    tail_first = tuple(n_tb - w // blk for w, blk in zip(WINDOWS, tail_blk))

    def tail_spec(blk, first):
        return pl.BlockSpec((None, 2, N_HEADS, HEAD_DIM, blk),
                            lambda b, t: (b, 0, 0, 0, jnp.maximum(t - first, 0)))

    norm_g, w_bf, perm, ones, qg, kg = common
    return pl.pallas_call(
        functools.partial(_qkv_prompt_kernel, tail_first=tail_first),
        grid=(batch, n_tb),
        in_specs=[pl.BlockSpec((QKV_TILE, D_MODEL), lambda b, t: (b * n_tb + t, 0)),
                  _const_spec((1, D_MODEL)),
                  pl.BlockSpec((D_MODEL, 9 * ATTN_W), lambda b, t: (0, 0), pipeline_mode=pl.Buffered(1)),
                  _const_spec((ROW_TILE, ROW_TILE)), _const_spec((ROW_TILE, ROW_TILE)), _const_spec((256, 256)),
                  _const_spec((1, ATTN_W)), _const_spec((1, ATTN_W)),
                  pl.BlockSpec((QKV_TILE, 128), lambda b, t: (t, 0)),
                  pl.BlockSpec((QKV_TILE, 128), lambda b, t: (t, 0))],
        out_specs=((g0_spec,) * 3 + (g12_spec,) * 3
                   + tuple(tail_spec(blk, first) for blk, first in zip(tail_blk, tail_first))),
        out_shape=out_shape,
        compiler_params=_cparams("arbitrary", "arbitrary"),
        name="qkv_prompt",
    )(x2d, norm_g, w_bf, perm, _perm_matrices()[1], ones, qg, kg, cos, sin)


def _qkv_sample(x2d, pos, common):
    rows = x2d.shape[0]
    cos, sin = _rope_tables(pos)
    spec = pl.BlockSpec((ROW_TILE, 3 * ATTN_W), lambda i: (i, 0))
    return pl.pallas_call(
        _qkv_sample_kernel,
        grid=(rows // ROW_TILE,),
        in_specs=[pl.BlockSpec((ROW_TILE, D_MODEL), lambda i: (i, 0)),
                  _const_spec((1, D_MODEL)), _const_spec((D_MODEL, 9 * ATTN_W)),
                  _const_spec((ROW_TILE, ROW_TILE)), _const_spec((256, 256)),
                  _const_spec((1, ATTN_W)), _const_spec((1, ATTN_W)),
                  pl.BlockSpec((ROW_TILE, 128), lambda i: (i, 0)),
                  pl.BlockSpec((ROW_TILE, 128), lambda i: (i, 0))],
        out_specs=(spec,) * 3,
        out_shape=(jax.ShapeDtypeStruct((rows, 3 * ATTN_W), F32),) * 3,
        compiler_params=_cparams("arbitrary"),
        name="qkv_sample",
    )(x2d, *common, cos, sin)


ATTN_Q = 128


def _attn_geometry(dil):
    nres = P16 // dil
    qa = ATTN_Q // nres
    return nres, qa


def _attn_bias_tables(dil):
    nres, qa = _attn_geometry(dil)
    m = np.arange(ATTN_Q)
    n = np.arange(2 * ATTN_Q)
    qoff = nres * (m % qa) + m // qa
    koff = nres * (n % (2 * qa)) + n // (2 * qa)
    tables = []
    for delta in (0, qa):
        dist = qoff[:, None] - koff[None, :] + nres * delta
        bias = np.where((dist >= 0) & (dist <= KEYS_BACK), 0.0, NEG_BIG).astype(np.float32)
        tables.append(np.concatenate([bias, bias], axis=0))
    return jnp.asarray(np.stack(tables))


def _attn_group(bias_ref, q_ref, k_ref, v_ref, o_ref, l_ref, m_ref, dil, n_a):
    nres, qa = _attn_geometry(dil)
    n_atiles = n_a // qa
    head1 = lax.broadcasted_iota(jnp.int32, (ATTN_Q, 128), 1) >= HEAD_DIM
    zeros = jnp.zeros((ATTN_Q, 128), BF16)
    ones = jnp.ones((2 * ATTN_Q, 128), BF16)

    def tile(t, carry):
        c = t // n_atiles
        at = t % n_atiles
        a0 = pl.multiple_of(at * qa, qa)
        k0 = pl.multiple_of(jnp.maximum(at * qa - qa, 0), qa)
        rows = [c + dil * b for b in range(nres)]
        q = jnp.concatenate([q_ref[r, pl.ds(a0, qa), :] for r in rows], axis=0).astype(BF16)
        k = jnp.concatenate([k_ref[r, pl.ds(k0, 2 * qa), :] for r in rows], axis=0).astype(BF16)
        v = jnp.concatenate([v_ref[r, pl.ds(k0, 2 * qa), :] for r in rows], axis=0).astype(BF16)
        q = q * jnp.asarray(HEAD_DIM ** -0.5, BF16)
        q2 = jnp.concatenate([jnp.where(head1, zeros, q), jnp.where(head1, q, zeros)], axis=0)
        s = lax.dot_general(q2, k, (((1,), (1,)), ((), ())), preferred_element_type=F32)
        s = s + bias_ref[jnp.minimum(at, 1)]
        mx = jnp.max(s, axis=-1, keepdims=True)
        p = jnp.exp(s - mx).astype(BF16)
        ov = jnp.dot(p, jnp.concatenate([v, ones], axis=1), preferred_element_type=F32)
        o = jnp.where(head1, ov[ATTN_Q:, :128], ov[:ATTN_Q, :128])
        den = jnp.where(head1, ov[ATTN_Q:, 128:], ov[:ATTN_Q, 128:])
        m = jnp.where(head1, mx[ATTN_Q:], mx[:ATTN_Q])
        for b, r in enumerate(rows):
            o_ref[r, pl.ds(a0, qa), :] = o[b * qa:(b + 1) * qa]
            l_ref[r, pl.ds(a0, qa), :] = den[b * qa:(b + 1) * qa]
            m_ref[r, pl.ds(a0, qa), :] = m[b * qa:(b + 1) * qa]
        return carry

    lax.fori_loop(0, dil * n_atiles, tile, 0, unroll=32)


def _attn_prompt_kernel(b0, b1, b2, q0, k0, v0, q1, k1, v1, q2, k2, v2, o_ref, og_ref, lg_ref, mg_ref, *, n_a):
    groups = ((b0, q0, k0, v0), (b1, q1, k1, v1), (b2, q2, k2, v2))
    for g, (bias_ref, q_ref, k_ref, v_ref) in enumerate(groups):
        _attn_group(bias_ref, q_ref, k_ref, v_ref, og_ref.at[g], lg_ref.at[g], mg_ref.at[g], DILATIONS[g], n_a)

    def merge(r, carry):
        ms = [mg_ref[g, r] for g in range(3)]
        mx = jnp.maximum(jnp.maximum(ms[0], ms[1]), ms[2])
        es = [jnp.exp(m - mx) for m in ms]
        num = es[0] * og_ref[0, r] + es[1] * og_ref[1, r] + es[2] * og_ref[2, r]
        den = es[0] * lg_ref[0, r] + es[1] * lg_ref[1, r] + es[2] * lg_ref[2, r]
        o_ref[r] = (num / den).astype(BF16)
        return carry

    lax.fori_loop(0, P16, merge, 0)


def _attn_prompt(qkv0, qkv12):
    batch, _, n_a, _ = qkv0[0].shape
    biases = [_attn_bias_tables(dil) for dil in DILATIONS]
    in_specs = [_const_spec(b.shape) for b in biases]
    operands = list(biases)
    for g in range(3):
        in_specs += [pl.BlockSpec((None, P16, n_a, 128), lambda b, h, g=g: (b, 0, 0, max(g - 1, 0) * 4 + h))] * 3
        operands += list(qkv0 if g == 0 else qkv12)
    return pl.pallas_call(
        functools.partial(_attn_prompt_kernel, n_a=n_a),
        grid=(batch, 4),
        in_specs=in_specs,
        out_specs=pl.BlockSpec((None, P16, n_a, 128), lambda b, h: (b, 0, 0, h)),
        out_shape=jax.ShapeDtypeStruct((batch, P16, n_a, ATTN_W), BF16),
        scratch_shapes=[pltpu.VMEM((3, P16, n_a, 128), F32)] * 3,
        compiler_params=_cparams("arbitrary", "arbitrary"),
        name="attn_prompt",
    )(*operands)


def _attn_sample_item(q_ref, n_ref, c0_ref, c1_ref, c2_ref, o_ref, nc0_ref, nc1_ref, nc2_ref):
    n_new = q_ref.shape[2]
    assert q_ref.shape[1] == 2
    rows = 2 * n_new
    nt = (((1,), (1,)), ((), ()))
    caches = ((c0_ref, nc0_ref), (c1_ref, nc1_ref), (c2_ref, nc2_ref))
    lane = lax.broadcasted_iota(jnp.int32, (HEAD_DIM, 128), 1)
    pad_rows = jnp.zeros((128 - n_new, 128), F32)
    row_r = lax.broadcasted_iota(jnp.int32, (rows, 128), 0)
    lane_r = lax.broadcasted_iota(jnp.int32, (rows, 128), 1)
    own = (lane_r // HEAD_DIM) == (row_r // n_new)
    row_n = lax.broadcasted_iota(jnp.int32, (rows, rows), 0)
    col_n = lax.broadcasted_iota(jnp.int32, (rows, rows), 1)
    dist_n = (row_n & (n_new - 1)) - (col_n & (n_new - 1))
    same_head = (row_n // n_new) == (col_n // n_new)

    def stacked(a0, a1):
        return jnp.concatenate([a0, pltpu.roll(a1, HEAD_DIM, 1)], axis=0)

    outs, lses = [], []
    for g, ((c_ref, nc_ref), dil) in enumerate(zip(caches, DILATIONS)):
        win = c_ref.shape[-1]
        dist_c = (win + (lax.broadcasted_iota(jnp.int32, (rows, win), 0) & (n_new - 1))
                  - lax.broadcasted_iota(jnp.int32, (rows, win), 1))
        ok_c = (dist_c <= win) & ((dist_c & (dil - 1)) == 0)
        ok_n = same_head & (dist_n >= 0) & ((dist_n & (dil - 1)) == 0)
        q2 = stacked(q_ref[g, 0], q_ref[g, 1]) * (HEAD_DIM ** -0.5)
        kn2 = stacked(n_ref[g, 0, 0], n_ref[g, 0, 1])
        vn2 = stacked(n_ref[g, 1, 0], n_ref[g, 1, 1])
        kt2 = c_ref[0].reshape(2 * HEAD_DIM, win)
        vt2 = c_ref[1].reshape(2 * HEAD_DIM, win)
        s_c = jnp.dot(q2.astype(BF16), kt2.astype(BF16), preferred_element_type=F32)
        s_n = lax.dot_general(q2, kn2, nt, preferred_element_type=F32)
        s_c = jnp.where(ok_c, s_c, NEG_BIG)
        s_n = jnp.where(ok_n, s_n, NEG_BIG)
        mx = jnp.maximum(jnp.max(s_c, axis=-1, keepdims=True), jnp.max(s_n, axis=-1, keepdims=True))
        p_c = jnp.exp(s_c - mx)
        p_n = jnp.exp(s_n - mx)
        den = jnp.sum(p_c, axis=-1, keepdims=True) + jnp.sum(p_n, axis=-1, keepdims=True)
        o = (lax.dot_general(p_c.astype(BF16), vt2.astype(BF16), nt, preferred_element_type=F32)
             + jnp.dot(p_n, vn2, preferred_element_type=F32))
        outs.append(jnp.where(own, o, 0.0) / den)
        lses.append(mx + jnp.log(den))
        for kv in range(2):
            for h in range(2):
                rolled = pltpu.roll(c_ref[kv, h], win - n_new, 1)
                new_t = jnp.concatenate([pad_rows, n_ref[g, kv, h]], axis=0).T[:HEAD_DIM]
                if win > 128:
                    nc_ref[kv, h, :, :win - 128] = rolled[:, :win - 128]
                nc_ref[kv, h, :, win - 128:] = jnp.where(lane < 128 - n_new, rolled[:, win - 128:], new_t)
    mx = jnp.maximum(jnp.maximum(lses[0], lses[1]), lses[2])
    es = [jnp.exp(l - mx) for l in lses]
    o = (es[0] * outs[0] + es[1] * outs[1] + es[2] * outs[2]) / (es[0] + es[1] + es[2])
    o_ref[0] = o[:n_new, :HEAD_DIM]
    o_ref[1] = pltpu.roll(o[n_new:], HEAD_DIM, 1)[:, :HEAD_DIM]


def _ffn_block(x, g_ref, wg_ref, wu_ref, wd_ref, chunks):
    h = _rms(x, g_ref[...]).astype(BF16)
    acc = x
    for lo, hi in chunks:
        gate = jnp.dot(h, wg_ref[:, lo:hi], preferred_element_type=F32)
        up = jnp.dot(h, wu_ref[:, lo:hi], preferred_element_type=F32)
        act = (jax.nn.silu(gate) * up).astype(BF16)
        acc = acc + jnp.dot(act, wd_ref[lo:hi, :], preferred_element_type=F32)
    return acc


def _mixer_residual(mode, refs):
    if mode == "wo":
        o_ref, x_ref, wo_ref, permt_ref = refs
        blocks = []
        for i in range(o_ref.shape[1] // P16):
            o = o_ref[:, i * P16:(i + 1) * P16, :].astype(F32).reshape(ROW_TILE, ATTN_W).astype(BF16)
            blocks.append(jnp.dot(permt_ref[...], o, preferred_element_type=F32).astype(BF16))
        return x_ref[...] + jnp.dot(jnp.concatenate(blocks, axis=0), wo_ref[...], preferred_element_type=F32)
    if mode == "wo_rows":
        o_ref, x_ref, wo_ref = refs
        return x_ref[...] + jnp.dot(o_ref[...].astype(BF16), wo_ref[...], preferred_element_type=F32)
    a_ref, x_ref, wglu_ref = refs
    z = jnp.dot(a_ref[...], wglu_ref[...], preferred_element_type=F32)
    return x_ref[...] + z[:, :D_MODEL] * jax.nn.sigmoid(z[:, D_MODEL:])


_MIXER_INPUTS = {"wo": 4, "wo_rows": 3, "glu": 3}
_RIDER_INPUTS = 5


def _mixer_ffn_kernel(*refs, mode, chunks, rider_inputs):
    n_pre = _MIXER_INPUTS[mode]
    g_ref, wg_ref, wu_ref, wd_ref = refs[n_pre:n_pre + 4]
    rest = refs[n_pre + 4:]
    y_ref = rest[rider_inputs]
    if rider_inputs:
        _attn_sample_item(*rest[:_RIDER_INPUTS], *rest[rider_inputs + 1:])
    y_ref[...] = _ffn_block(_mixer_residual(mode, refs[:n_pre]), g_ref, wg_ref, wu_ref, wd_ref, chunks)


def _ffn(x2d, norm_g, wg_bf, wu_bf, wd_bf, layer, tm, pre, seq=None, rider=None):
    rows = x2d.shape[0]
    d_ff = wg_bf.shape[2]
    step = 1024
    chunks = tuple((lo, min(lo + step, d_ff)) for lo in range(0, d_ff, step))
    x_spec = pl.BlockSpec((tm, D_MODEL), lambda i: (i, 0))

    def resident(shape):
        return pl.BlockSpec(shape, lambda i: (0,) * len(shape), pipeline_mode=pl.Buffered(1))

    def of_layer(shape):
        return pl.BlockSpec((None,) + shape, lambda i: (layer, 0, 0), pipeline_mode=pl.Buffered(1))

    ffn_specs = [_const_spec((1, D_MODEL)), of_layer((D_MODEL, d_ff)), of_layer((D_MODEL, d_ff)),
                 of_layer((d_ff, D_MODEL))]
    ffn_args = (norm_g.reshape(1, D_MODEL), wg_bf, wu_bf, wd_bf)
    mode = pre[0]
    if mode == "wo":
        n_tb = seq // tm
        o_spec = pl.BlockSpec((None, P16, tm // P16, ATTN_W), lambda i: (i // n_tb, 0, i % n_tb, 0))
        specs = [o_spec, x_spec, resident((ATTN_W, D_MODEL)), _const_spec((ROW_TILE, ROW_TILE))]
        args = (pre[1], x2d, pre[2], _perm_matrices()[1])
    elif mode == "wo_rows":
        specs = [pl.BlockSpec((tm, ATTN_W), lambda i: (i, 0)), x_spec, resident((ATTN_W, D_MODEL))]
        args = (pre[1], x2d, pre[2])
    else:
        specs = [x_spec, x_spec, resident((D_MODEL, 2 * D_MODEL))]
        args = (pre[1], x2d, pre[2])
    out_specs = [x_spec]
    out_shape = [jax.ShapeDtypeStruct((rows, D_MODEL), F32)]
    rider_specs, rider_args, aliases = [], (), {}
    if rider is not None:
        q_p, new_p, caches_t, hp_base, prev = rider
        batch, _, _, n_new, _ = q_p.shape
        assert rows // tm == 2 * batch
        item = lambda i: (i // 2, hp_base + i % 2)
        c_specs = [pl.BlockSpec((None, 2, 2, HEAD_DIM, c.shape[-1]), lambda i: (item(i)[0], 0, item(i)[1], 0, 0))
                   for c in caches_t]
        rider_specs = [pl.BlockSpec((None, 3, 2, n_new, 128), lambda i: (item(i)[0], 0, item(i)[1], 0, 0)),
                       pl.BlockSpec((None, 3, 2, 2, n_new, 128), lambda i: (item(i)[0], 0, 0, item(i)[1], 0, 0)),
                       *c_specs]
        rider_args = (q_p, new_p, *caches_t)
        out_specs += [pl.BlockSpec((None, 2, n_new, HEAD_DIM), lambda i: (item(i)[0], item(i)[1], 0, 0)), *c_specs]
        out_shape += [jax.ShapeDtypeStruct((batch, N_HEADS, n_new, HEAD_DIM), F32)]
        out_shape += [jax.ShapeDtypeStruct(c.shape, F32) for c in caches_t]
        if prev is not None:
            first = len(specs) + len(ffn_specs) + len(rider_specs)
            aliases = {first + k: 1 + k for k in range(len(prev))}
            rider_specs += [pl.BlockSpec(memory_space=pl.ANY)] * len(prev)
            rider_args += tuple(prev)
    outs = pl.pallas_call(
        functools.partial(_mixer_ffn_kernel, mode=mode, chunks=chunks, rider_inputs=len(rider_specs)),
        grid=(rows // tm,),
        in_specs=specs + ffn_specs + rider_specs,
        out_specs=out_specs,
        out_shape=out_shape,
        input_output_aliases=aliases,
        compiler_params=_cparams("arbitrary"),
        name=mode + "_ffn",
    )(*args, *ffn_args, *rider_args)
    return outs[0] if rider is None else outs


def _norm_matmul_kernel(x_ref, g_ref, w_ref, y_ref):
    h = _rms(x_ref[...], g_ref[...]).astype(BF16)
    y_ref[...] = jnp.dot(h, w_ref[...], preferred_element_type=F32).astype(y_ref.dtype)


def _norm_matmul(x2d, norm_g, w_bf, tm):
    rows = x2d.shape[0]
    n = w_bf.shape[1]
    return pl.pallas_call(
        _norm_matmul_kernel,
        grid=(rows // tm,),
        in_specs=[pl.BlockSpec((tm, D_MODEL), lambda i: (i, 0)), _const_spec((1, D_MODEL)),
                  _const_spec((D_MODEL, n))],
        out_specs=pl.BlockSpec((tm, n), lambda i: (i, 0)),
        out_shape=jax.ShapeDtypeStruct((rows, n), BF16),
        compiler_params=_cparams("arbitrary"),
        name="ssm_in_proj",
    )(x2d, norm_g.reshape(1, D_MODEL), w_bf)


SSM_LANE_TILE = 128
SSM_TOK_TILE = SSM_LANE_TILE * SSM_CHUNK


def _ssm_in_kernel(x_ref, g_ref, w_ref, perm_ref, ut_ref):
    n_blk = SSM_TOK_TILE // ROW_TILE
    blocks = []
    for kb in range(n_blk):
        h = _rms(x_ref[kb * ROW_TILE:(kb + 1) * ROW_TILE, :], g_ref[...]).astype(BF16)
        blocks.append(jnp.dot(perm_ref[...], h, preferred_element_type=F32).astype(BF16))
    hp = jnp.concatenate([blk[j * P16:(j + 1) * P16] for j in range(SSM_CHUNK) for blk in blocks], axis=0)
    u = jnp.dot(hp, w_ref[...], preferred_element_type=F32)
    for j in range(SSM_CHUNK):
        ut_ref[j] = u[j * SSM_LANE_TILE:(j + 1) * SSM_LANE_TILE].T.astype(BF16)


def _ssm_in(x2d, norm_g, w_bf, perm):
    rows = x2d.shape[0]
    n_tiles = rows // SSM_TOK_TILE
    return pl.pallas_call(
        _ssm_in_kernel,
        grid=(n_tiles,),
        in_specs=[pl.BlockSpec((SSM_TOK_TILE, D_MODEL), lambda i: (i, 0)), _const_spec((1, D_MODEL)),
                  pl.BlockSpec((D_MODEL, D_MODEL), lambda i: (0, 0), pipeline_mode=pl.Buffered(1)),
                  _const_spec((ROW_TILE, ROW_TILE))],
        out_specs=pl.BlockSpec((SSM_CHUNK, D_MODEL, SSM_LANE_TILE), lambda i: (0, 0, i)),
        out_shape=jax.ShapeDtypeStruct((SSM_CHUNK, D_MODEL, n_tiles * SSM_LANE_TILE), BF16),
        compiler_params=_cparams("arbitrary"),
        name="ssm_in_t",
    )(x2d, norm_g.reshape(1, D_MODEL), w_bf, perm)


def _ssm_gate_kernel(yt_ref, permt_ref, g_ref):
    n_blk = SSM_TOK_TILE // ROW_TILE
    pieces = []
    for i in range(SSM_CHUNK):
        pieces.append(jax.nn.gelu(yt_ref[i]).T.astype(BF16))
    for kb in range(n_blk):
        blk = jnp.concatenate([p[kb * P16:(kb + 1) * P16] for p in pieces], axis=0)
        g_ref[kb * ROW_TILE:(kb + 1) * ROW_TILE, :] = jnp.dot(
            permt_ref[...], blk, preferred_element_type=F32).astype(BF16)


def _ssm_gate(yt, permt):
    n_tiles = yt.shape[2] // SSM_LANE_TILE
    return pl.pallas_call(
        _ssm_gate_kernel,
        grid=(n_tiles,),
        in_specs=[pl.BlockSpec((SSM_CHUNK, D_MODEL, SSM_LANE_TILE), lambda i: (0, 0, i)),
                  _const_spec((ROW_TILE, ROW_TILE))],
        out_specs=pl.BlockSpec((SSM_TOK_TILE, D_MODEL), lambda i: (i, 0)),
        out_shape=jax.ShapeDtypeStruct((n_tiles * SSM_TOK_TILE, D_MODEL), BF16),
        compiler_params=_cparams("arbitrary"),
        name="ssm_gate_t",
    )(yt, permt)


def _ssm_core_kernel(u_ref, mt_ref, ft_ref, et_ref, a_ref, x0_ref, y_ref, xf_ref, *, n_chunks, has_init):
    chunk, _, n_lanes = u_ref.shape
    if n_chunks == 1:
        for gi in range(mt_ref.shape[0]):
            rows = pl.ds(gi * SSM_GROUP, SSM_GROUP)
            u = u_ref[:, rows, :].reshape(chunk * SSM_GROUP, n_lanes)
            x0 = x0_ref[gi]
            xre, xim = x0[:SSM_STATE], x0[SSM_STATE:]
            s = jnp.dot(ft_ref[gi], u, preferred_element_type=F32)
            are, aim = a_ref[gi, 0], a_ref[gi, 1]
            xf_ref[gi, :SSM_STATE, :] = are * xre - aim * xim + s[:SSM_STATE]
            xf_ref[gi, SSM_STATE:, :] = are * xim + aim * xre + s[SSM_STATE:]
            y = (jnp.dot(mt_ref[gi], u, preferred_element_type=F32)
                 + jnp.dot(et_ref[gi], x0.astype(BF16), preferred_element_type=F32))
            y_ref[:, rows, :] = y.reshape(chunk, SSM_GROUP, n_lanes)
    else:
        assert not has_init and n_chunks % 128 == 0 and mt_ref.shape[0] == 1
        u = u_ref[...].reshape(chunk * SSM_GROUP, n_lanes)
        y = jnp.dot(mt_ref[0], u, preferred_element_type=F32)
        s = jnp.dot(ft_ref[0], u, preferred_element_type=F32)
        sre, sim = s[:SSM_STATE], s[SSM_STATE:]
        are, aim = a_ref[0, 0], a_ref[0, 1]
        reps = n_lanes // 128
        pos = lax.broadcasted_iota(jnp.int32, (SSM_STATE, n_lanes), 1) & (n_chunks - 1)
        shift = 1
        while shift < n_chunks:
            keep = pos >= shift
            tre = jnp.where(keep, pltpu.roll(sre, shift, 1), 0.0)
            tim = jnp.where(keep, pltpu.roll(sim, shift, 1), 0.0)
            bre = jnp.concatenate([are] * reps, axis=1)
            bim = jnp.concatenate([aim] * reps, axis=1)
            sre, sim = sre + bre * tre - bim * tim, sim + bre * tim + bim * tre
            are, aim = are * are - aim * aim, 2.0 * are * aim
            shift *= 2
        lane128 = lax.broadcasted_iota(jnp.int32, (SSM_STATE, 128), 1)
        fre = jnp.zeros((SSM_STATE, 128), F32)
        fim = jnp.zeros((SSM_STATE, 128), F32)
        for b in range(n_lanes // n_chunks):
            lo, hi = (b + 1) * n_chunks - 128, (b + 1) * n_chunks
            last = lane128 == 127
            fre = jnp.where(lane128 == b, jnp.sum(jnp.where(last, sre[:, lo:hi], 0.0), axis=1, keepdims=True), fre)
            fim = jnp.where(lane128 == b, jnp.sum(jnp.where(last, sim[:, lo:hi], 0.0), axis=1, keepdims=True), fim)
        xf_ref[0, :SSM_STATE, :] = fre
        xf_ref[0, SSM_STATE:, :] = fim
        keep = pos >= 1
        xin = jnp.concatenate([jnp.where(keep, pltpu.roll(sre, 1, 1), 0.0),
                               jnp.where(keep, pltpu.roll(sim, 1, 1), 0.0)], axis=0)
        y = y + jnp.dot(et_ref[0], xin.astype(BF16), preferred_element_type=F32)
        y_ref[...] = y.reshape(chunk, SSM_GROUP, n_lanes)


def _ssm_core(ut, mats, x0t, n_chunks):
    mt, ft, et, a_pow = mats
    chunk, _, n_lanes = ut.shape
    groups = SSM_GROUPS
    has_init = x0t is not None
    a_lanes = n_lanes if n_chunks == 1 else 128
    a_b = jnp.broadcast_to(a_pow[:, :, :, None], a_pow.shape + (a_lanes,))
    if x0t is None:
        x0t = jnp.zeros((groups, 2 * SSM_STATE, 128), F32)
    xf_lanes = n_lanes if n_chunks == 1 else 128
    assert n_chunks == 1 or n_lanes // n_chunks <= 128
    gb = 8 if n_chunks == 1 else 1

    def gspec(shape):
        return pl.BlockSpec((gb,) + tuple(shape[1:]), lambda g: (g,) + (0,) * (len(shape) - 1))

    t_spec = pl.BlockSpec((chunk, gb * SSM_GROUP, n_lanes), lambda g: (0, g, 0))
    return pl.pallas_call(
        functools.partial(_ssm_core_kernel, n_chunks=n_chunks, has_init=has_init),
        grid=(groups // gb,),
        in_specs=[t_spec, gspec(mt.shape), gspec(ft.shape), gspec(et.shape), gspec(a_b.shape),
                  gspec(x0t.shape)],
        out_specs=(t_spec, gspec((groups, 2 * SSM_STATE, xf_lanes))),
        out_shape=(jax.ShapeDtypeStruct((chunk, D_MODEL, n_lanes), F32),
                   jax.ShapeDtypeStruct((groups, 2 * SSM_STATE, xf_lanes), F32)),
        compiler_params=_cparams("arbitrary"),
        name=f"ssm_core_{n_chunks}",
    )(ut, mt, ft, et, a_b, x0t)


def _cexp(mag_arg, ang):
    mag = jnp.exp(mag_arg)
    return mag * jnp.cos(ang), mag * jnp.sin(ang)


def _ssm_matrices_kernel(ar_row, ai_row, ar_col, ai_col, ld_ref, bre_ref, bim_ref, cre_ref, cim_ref,
                         d_ref, tile_ref, tile2_ref, mt_ref, ft_ref, et_ref, ap_ref, mth_ref, fth_ref, eth_ref,
                         *, chunk):
    hp = lax.Precision.HIGHEST
    rows = SSM_GROUP * chunk
    dt = jnp.exp(ld_ref[...])

    def powers(re, im, n):
        out = [(jnp.ones_like(re), jnp.zeros_like(im))]
        for _ in range(n):
            pr, pi = out[-1]
            out.append((pr * re - pi * im, pr * im + pi * re))
        return out

    pw = powers(*_cexp(ar_row[...] * dt, ai_row[...] * dt), chunk)
    cre = jnp.dot(cre_ref[...], tile2_ref[...], precision=hp, preferred_element_type=F32)
    cim = jnp.dot(cim_ref[...], tile2_ref[...], precision=hp, preferred_element_type=F32)
    cre, cim = jnp.concatenate([cre] * chunk, axis=0), jnp.concatenate([cim] * chunk, axis=0)

    def by_step(first):
        return tuple(jnp.concatenate([jnp.broadcast_to(pw[first + i][k], (SSM_GROUP, 128))
                                      for i in range(chunk)], axis=0) for k in range(2))

    pr, pi = by_step(0)
    r0_re, r0_im = (cre * pr - cim * pi)[:, :SSM_STATE], (cre * pi + cim * pr)[:, :SSM_STATE]
    pr, pi = by_step(1)
    lane = lax.broadcasted_iota(jnp.int32, (rows, 128), 1)
    et = jnp.where(lane < SSM_STATE, cre * pr - cim * pi, -(cre * pi + cim * pr)).astype(BF16)
    et_ref[...] = et
    eth_ref[...] = et[:rows // 2]
    for k, n in enumerate((chunk, chunk // 2)):
        ap_ref[2 * k:2 * k + 1, :] = pw[n][0]
        ap_ref[2 * k + 1:2 * k + 2, :] = pw[n][1]
    ar, ai = ar_col[...], ai_col[...]
    abr, abi = _cexp(ar * dt, ai * dt)
    inv = 1.0 / (ar * ar + ai * ai)
    zr = ((abr - 1.0) * ar + abi * ai) * inv
    zi = (abi * ar - (abr - 1.0) * ai) * inv
    bre, bim = bre_ref[...], bim_ref[...]
    bbr = jnp.dot(zr * bre - zi * bim, tile_ref[...], precision=hp, preferred_element_type=F32)
    bbi = jnp.dot(zr * bim + zi * bre, tile_ref[...], precision=hp, preferred_element_type=F32)
    pc = powers(abr, abi, chunk - 1)
    col_step = lax.broadcasted_iota(jnp.int32, (SSM_STATE, rows), 1) >> 4
    pr = jnp.zeros((SSM_STATE, rows), F32)
    pi = jnp.zeros((SSM_STATE, rows), F32)
    for j in range(chunk):
        pr = jnp.where(col_step == j, pc[chunk - 1 - j][0], pr)
        pi = jnp.where(col_step == j, pc[chunk - 1 - j][1], pi)
    ft = jnp.concatenate([pr * bbr - pi * bbi, pr * bbi + pi * bbr], axis=0).astype(BF16)
    ft_ref[...] = ft
    fth_ref[...] = ft[:, rows // 2:]
    tw = (jnp.dot(r0_re, bbr, precision=hp, preferred_element_type=F32)
          - jnp.dot(r0_im, bbi, precision=hp, preferred_element_type=F32))
    per_tile = 128 // SSM_GROUP
    row_idx = lax.broadcasted_iota(jnp.int32, (rows, 128), 0)
    col_idx = lax.broadcasted_iota(jnp.int32, (rows, 128), 1)
    row_blk, col_blk = row_idx >> 4, col_idx >> 4
    for lt in range(rows // 128):
        tw_lt = tw[:, lt * 128:(lt + 1) * 128]
        acc = jnp.zeros((rows, 128), F32)
        for jj in range(per_tile):
            j = lt * per_tile + jj
            shifted = tw_lt if j == 0 else pltpu.roll(tw_lt, SSM_GROUP * j, 0)
            acc = jnp.where(col_blk == jj, shifted, acc)
        keep = row_blk >= col_blk + lt * per_tile
        diag = row_idx == col_idx + lt * 128
        m_lt = (jnp.where(keep, acc, 0.0) + jnp.where(diag, d_ref[:, lt * 128:(lt + 1) * 128], 0.0)).astype(BF16)
        mt_ref[:, lt * 128:(lt + 1) * 128] = m_lt
        if lt == 0:
            mth_ref[...] = m_lt[:rows // 2]


def _ssm_matrices(a_re, a_im, log_dt, b_re, b_im, c_re, c_im, d_skip, chunk):
    groups, rows = SSM_GROUPS, SSM_GROUP * chunk
    half_rows = rows // 2

    def gspec(*shape):
        return pl.BlockSpec((None,) + shape, lambda g: (g,) + (0,) * len(shape))

    row2 = lambda a: jnp.tile(a, (1, 2)).reshape(groups, 1, 2 * SSM_STATE)
    col = lambda a: a.reshape(groups, SSM_STATE, 1)
    tile = jnp.asarray(np.tile(np.eye(SSM_GROUP, dtype=np.float32), (1, chunk)))
    tile2 = jnp.asarray(np.tile(np.eye(SSM_STATE, dtype=np.float32), (1, 2)))
    d_t = jnp.tile(d_skip.reshape(groups, 1, SSM_GROUP), (1, 1, chunk))
    mt, ft, et, a_pow, mt_h, ft_h, et_h = pl.pallas_call(
        functools.partial(_ssm_matrices_kernel, chunk=chunk),
        grid=(groups,),
        in_specs=[gspec(1, 128), gspec(1, 128), gspec(SSM_STATE, 1), gspec(SSM_STATE, 1), gspec(1, 1),
                  gspec(SSM_STATE, SSM_GROUP), gspec(SSM_STATE, SSM_GROUP),
                  gspec(SSM_GROUP, SSM_STATE), gspec(SSM_GROUP, SSM_STATE), gspec(1, rows),
                  _const_spec(tile.shape), _const_spec(tile2.shape)],
        out_specs=(gspec(rows, rows), gspec(2 * SSM_STATE, rows), gspec(rows, 2 * SSM_STATE), gspec(4, 128),
                   gspec(half_rows, half_rows), gspec(2 * SSM_STATE, half_rows), gspec(half_rows, 2 * SSM_STATE)),
        out_shape=(jax.ShapeDtypeStruct((groups, rows, rows), BF16),
                   jax.ShapeDtypeStruct((groups, 2 * SSM_STATE, rows), BF16),
                   jax.ShapeDtypeStruct((groups, rows, 2 * SSM_STATE), BF16),
                   jax.ShapeDtypeStruct((groups, 4, 128), F32),
                   jax.ShapeDtypeStruct((groups, half_rows, half_rows), BF16),
                   jax.ShapeDtypeStruct((groups, 2 * SSM_STATE, half_rows), BF16),
                   jax.ShapeDtypeStruct((groups, half_rows, 2 * SSM_STATE), BF16)),
        compiler_params=_cparams("arbitrary"),
        name="ssm_matrices",
    )(row2(a_re), row2(a_im), col(a_re), col(a_im), log_dt.reshape(groups, 1, 1),
      b_re, b_im, c_re, c_im, d_t, tile, tile2)
    a_pow = a_pow[:, :, :SSM_STATE]
    return (mt, ft, et, a_pow[:, 0:2]), (mt_h, ft_h, et_h, a_pow[:, 2:4])


def _gelu_kernel(y_ref, a_ref):
    a_ref[...] = jax.nn.gelu(y_ref[...]).astype(BF16)


def _gelu(y2d, tm):
    spec = pl.BlockSpec((tm, D_MODEL), lambda i: (i, 0))
    return pl.pallas_call(
        _gelu_kernel,
        grid=(y2d.shape[0] // tm,),
        in_specs=[spec],
        out_specs=spec,
        out_shape=jax.ShapeDtypeStruct(y2d.shape, BF16),
        compiler_params=_cparams("arbitrary"),
        name="ssm_gelu",
    )(y2d)


def _ssm_mixer_prompt(x2d, batch, seq, norm_g, w_in_bf, mats, perms):
    assert seq % SSM_TOK_TILE == 0
    n_chunks = seq // SSM_CHUNK
    ut = _ssm_in(x2d, norm_g, w_in_bf, perms[0])
    yt, xf = _ssm_core(ut, mats, None, n_chunks)
    xf = xf[:, :, :batch].reshape(SSM_GROUPS, 2, SSM_STATE, batch)
    return _ssm_gate(yt, perms[1]), xf.transpose(3, 0, 2, 1)


def _ssm_mixer_sample(x2d, batch, seq, norm_g, w_in_bf, mats, state0, tm):
    u = _norm_matmul(x2d, norm_g, w_in_bf, tm)
    ut = u.reshape(batch, seq, D_MODEL).transpose(1, 2, 0)
    x0t = state0.transpose(1, 3, 2, 0).reshape(SSM_GROUPS, 2 * SSM_STATE, batch)
    yt, xf = _ssm_core(ut, mats, x0t, 1)
    y = yt.transpose(2, 0, 1).reshape(batch * seq, D_MODEL)
    state = xf.reshape(SSM_GROUPS, 2, SSM_STATE, batch).transpose(3, 0, 2, 1)
    return _gelu(y, tm), state


def kernel(x_prompt, x_sample, cache_kv_w128, cache_kv_w512, cache_kv_w2048, state_ssm, norm_mix, norm_ffn,
           w_qkv, q_norm, k_norm, w_o, ssm_w_in, ssm_a_re, ssm_a_im, ssm_log_dt, ssm_b_re, ssm_b_im,
           ssm_c_re, ssm_c_im, ssm_d, ssm_w_glu, ffn_w_gate, ffn_w_up, ffn_w_down):
    batch, seq, _ = x_prompt.shape
    dec_batch, dec_seq, _ = x_sample.shape
    caches = (cache_kv_w128, cache_kv_w512, cache_kv_w2048)
    assert seq % ROW_TILE == 0 and seq >= WINDOWS[-1] and (dec_batch * dec_seq) % ROW_TILE == 0
    assert all(c.shape[2] == w for c, w in zip(caches, WINDOWS))
    xp = x_prompt.reshape(batch * seq, D_MODEL)
    xs = x_sample.reshape(dec_batch * dec_seq, D_MODEL)

    common = _qkv_common_inputs(norm_mix[0], w_qkv[0].astype(BF16), q_norm[0], k_norm[0])
    w_o_bf = w_o[0].astype(BF16)
    q0, k0, v0, q12, k12, v12, t0, t1, t2 = _qkv_prompt(xp, batch, seq, common)
    o_prompt = _attn_prompt((q0, k0, v0), (q12, k12, v12))
    kv_prompt = [tail.transpose(0, 4, 1, 2, 3)[None] for tail in (t0, t1, t2)]

    pos_s = jnp.tile(PAST_LEN + jnp.arange(dec_seq), dec_batch)
    qs, ks, vs = _qkv_sample(xs, pos_s, common)

    def heads_padded(a):
        a = a.reshape(dec_batch, dec_seq, 3, N_HEADS, HEAD_DIM).transpose(0, 2, 3, 1, 4)
        return jnp.pad(a, ((0, 0),) * 4 + ((0, 128 - HEAD_DIM),))

    q_p = heads_padded(qs)
    new_p = jnp.stack([heads_padded(ks), heads_padded(vs)], axis=2)
    caches_t = [c[0].transpose(0, 2, 3, 4, 1) for c in caches]

    ffn_w = (ffn_w_gate.astype(BF16), ffn_w_up.astype(BF16), ffn_w_down.astype(BF16))
    xp, *sample_attn = _ffn(xp, norm_ffn[0], *ffn_w, 0, QKV_TILE, pre=("wo", o_prompt, w_o_bf), seq=seq,
                            rider=(q_p, new_p, caches_t, 0, None))

    ssm_p = (ssm_a_re[0], ssm_a_im[0], ssm_log_dt[0], ssm_b_re[0], ssm_b_im[0], ssm_c_re[0], ssm_c_im[0])
    w_in_bf = ssm_w_in[0].astype(BF16)
    w_glu_bf = ssm_w_glu[0].astype(BF16)
    assert 2 * dec_seq == SSM_CHUNK
    mats_full, mats_half = _ssm_matrices(*ssm_p, ssm_d[0], SSM_CHUNK)
    act_p, st_p = _ssm_mixer_prompt(xp, batch, seq, norm_mix[1], w_in_bf, mats_full, _perm_matrices())
    xp, o_s, *new_caches = _ffn(xp, norm_ffn[1], *ffn_w, 1, QKV_TILE, pre=("glu", act_p, w_glu_bf),
                                rider=(q_p, new_p, caches_t, 2, sample_attn))
    kv_sample = [nc.transpose(0, 4, 1, 2, 3)[None] for nc in new_caches]

    o_sample = o_s.transpose(0, 2, 1, 3).reshape(dec_batch * dec_seq, ATTN_W)
    xs = _ffn(xs, norm_ffn[0], *ffn_w, 0, ROW_TILE, pre=("wo_rows", o_sample, w_o_bf))
    act_s, st_s = _ssm_mixer_sample(xs, dec_batch, dec_seq, norm_mix[1], w_in_bf, mats_half, state_ssm[0],
                                    ROW_TILE)
    xs = _ffn(xs, norm_ffn[1], *ffn_w, 1, ROW_TILE, pre=("glu", act_s, w_glu_bf))

    return (xp.reshape(batch, seq, D_MODEL), xs.reshape(dec_batch, dec_seq, D_MODEL),
            kv_prompt[0], kv_prompt[1], kv_prompt[2], st_p[None],
            kv_sample[0], kv_sample[1], kv_sample[2], st_s[None])
```

```python
import functools
import math

import numpy as np
import jax
import jax.numpy as jnp
from jax import lax
from jax.experimental import pallas as pl
from jax.experimental.pallas import tpu as pltpu

F32 = jnp.float32
BF16 = jnp.bfloat16

D_MODEL = 1024
HEAD_DIM = 64
N_HEADS = 8
ATTN_W = N_HEADS * HEAD_DIM
DILATIONS = (1, 4, 16)
WINDOWS = (128, 512, 2048)
KEYS_BACK = 128
PAST_LEN = 16384
ROPE_THETA = 10000.0
RMS_EPS = 1e-6
SSM_GROUP = 16
SSM_GROUPS = D_MODEL // SSM_GROUP
SSM_STATE = 64
SSM_CHUNK = 16
ROW_TILE = 256
QKV_TILE = 512
P16 = 16
NEG_BIG = -1e30
VMEM_LIMIT = 56 * 1024 * 1024


def _cparams(*sem):
    return pltpu.CompilerParams(dimension_semantics=tuple(sem), vmem_limit_bytes=VMEM_LIMIT)


def _const_spec(shape):
    nd = len(shape)
    return pl.BlockSpec(shape, lambda *_: (0,) * nd)


def _perm_matrices():
    m = np.arange(ROW_TILE)
    to_p16 = np.zeros((ROW_TILE, ROW_TILE), np.float32)
    to_p16[m, P16 * (m % P16) + m // P16] = 1.0
    return jnp.asarray(to_p16, BF16), jnp.asarray(to_p16.T, BF16)


def _rms(x, gain):
    ms = jnp.mean(x * x, axis=-1, keepdims=True)
    return x * lax.rsqrt(ms + RMS_EPS) * gain


class _QkvSlabs:
    def __init__(self, x_ref, g_ref, w_ref, perm_ref, ones_ref, qg_ref, kg_ref, cos_ref, sin_ref, permute):
        rows = x_ref.shape[0]
        h = _rms(x_ref[...], g_ref[...]).astype(BF16)
        if permute:
            h = jnp.concatenate(
                [jnp.dot(perm_ref[...], h[i:i + ROW_TILE], preferred_element_type=F32).astype(BF16)
                 for i in range(0, rows, ROW_TILE)], axis=0)
        self.h, self.w_ref, self.ones = h, w_ref, ones_ref[...]
        self.gains = (qg_ref, kg_ref)
        self.cos = jnp.concatenate([cos_ref[...]] * 4, axis=1)
        self.sin = jnp.concatenate([sin_ref[...]] * 4, axis=1)
        lane = lax.broadcasted_iota(jnp.int32, (rows, ATTN_W), 1)
        self.first_half = (lane & (HEAD_DIM - 1)) < (HEAD_DIM // 2)

    def slab(self, s, g):
        col = (s * 3 + g) * ATTN_W
        y = jnp.dot(self.h, self.w_ref[:, col:col + ATTN_W], preferred_element_type=F32)
        if s == 2:
            return y
        yy = (y * y).astype(BF16)
        ss = jnp.concatenate([jnp.dot(yy[:, :256], self.ones, preferred_element_type=F32),
                              jnp.dot(yy[:, 256:], self.ones, preferred_element_type=F32)], axis=1)
        yn = y * lax.rsqrt(ss * (1.0 / HEAD_DIM) + RMS_EPS) * self.gains[s][...]
        swapped = jnp.where(self.first_half,
                            pltpu.roll(yn, ATTN_W - HEAD_DIM // 2, 1),
                            pltpu.roll(yn, HEAD_DIM // 2, 1))
        return yn * self.cos + swapped * self.sin


def _split3_bf16(x):
    hi = x.astype(BF16)
    r1 = x - hi.astype(F32)
    mid = r1.astype(BF16)
    lo = (r1 - mid.astype(F32)).astype(BF16)
    return hi, mid, lo


def _qkv_prompt_kernel(x_ref, g_ref, w_ref, perm_ref, permt_ref, ones_ref, qg_ref, kg_ref, cos_ref, sin_ref,
                       q0_ref, k0_ref, v0_ref, q_ref, k_ref, v_ref, t0_ref, t1_ref, t2_ref, *, tail_first):
    slabs = _QkvSlabs(x_ref, g_ref, w_ref, perm_ref, ones_ref, qg_ref, kg_ref, cos_ref, sin_ref, True)
    rows = x_ref.shape[0]
    tails = (t0_ref, t1_ref, t2_ref)
    t = pl.program_id(1)
    for g in range(3):
        y = [slabs.slab(s, g) for s in range(3)]
        for s, (ref0, ref) in enumerate(((q0_ref, q_ref), (k0_ref, k_ref), (v0_ref, v_ref))):
            for i in range(rows // ROW_TILE):
                blk = y[s][i * ROW_TILE:(i + 1) * ROW_TILE].reshape(P16, P16, ATTN_W)
                if g == 0:
                    ref0[:, i * P16:(i + 1) * P16, :] = blk
                else:
                    ref[:, i * P16:(i + 1) * P16, (g - 1) * ATTN_W:g * ATTN_W] = blk.astype(BF16)

        @pl.when(t >= tail_first[g])
        def _(g=g, y=y):
            keep = tails[g].shape[-1]
            for i in range((rows - keep) // ROW_TILE, rows // ROW_TILE):
                kv = jnp.concatenate([y[1][i * ROW_TILE:(i + 1) * ROW_TILE],
                                      y[2][i * ROW_TILE:(i + 1) * ROW_TILE]], axis=1)
                nat = sum(jnp.dot(permt_ref[...], part, preferred_element_type=F32)
                          for part in _split3_bf16(kv))
                lo = max(i * ROW_TILE, rows - keep)
                width = (i + 1) * ROW_TILE - lo
                tails[g][:, :, :, lo - (rows - keep):lo - (rows - keep) + width] = (
                    nat[ROW_TILE - width:].T.reshape(2, N_HEADS, HEAD_DIM, width))


def _qkv_sample_kernel(x_ref, g_ref, w_ref, perm_ref, ones_ref, qg_ref, kg_ref, cos_ref, sin_ref,
                       q_ref, k_ref, v_ref):
    slabs = _QkvSlabs(x_ref, g_ref, w_ref, perm_ref, ones_ref, qg_ref, kg_ref, cos_ref, sin_ref, False)
    for g in range(3):
        lo, hi = g * ATTN_W, (g + 1) * ATTN_W
        for s, ref in enumerate((q_ref, k_ref, v_ref)):
            ref[:, lo:hi] = slabs.slab(s, g)


def _qkv_common_inputs(norm_g, w_qkv_bf, q_gain, k_gain):
    ones = np.kron(np.eye(4, dtype=np.float32), np.ones((HEAD_DIM, HEAD_DIM), np.float32))
    return (norm_g.reshape(1, D_MODEL), w_qkv_bf, _perm_matrices()[0], jnp.asarray(ones, BF16),
            jnp.tile(q_gain, N_HEADS).reshape(1, ATTN_W), jnp.tile(k_gain, N_HEADS).reshape(1, ATTN_W))


def _rope_tables(pos):
    half = HEAD_DIM // 2
    inv = ROPE_THETA ** (-jnp.arange(half, dtype=F32) / half)
    ang = pos.astype(F32)[:, None] * inv[None, :]
    cos, sin = jnp.cos(ang), jnp.sin(ang)
    cos = jnp.concatenate([cos, cos, cos, cos], axis=1)
    sin = jnp.concatenate([-sin, sin, -sin, sin], axis=1)
    return cos, sin


def _p16_positions(seq):
    t = np.arange(seq).reshape(seq // ROW_TILE, P16, P16)
    return jnp.asarray(t.transpose(0, 2, 1).reshape(seq))


def _qkv_prompt(x2d, batch, seq, common):
    n_tb = seq // QKV_TILE
    n_a = seq // P16
    a_blk = QKV_TILE // P16
    cos, sin = _rope_tables(_p16_positions(seq))
    out_shape = ((jax.ShapeDtypeStruct((batch, P16, n_a, ATTN_W), F32),) * 3
                 + (jax.ShapeDtypeStruct((batch, P16, n_a, 2 * ATTN_W), BF16),) * 3
                 + tuple(jax.ShapeDtypeStruct((batch, 2, N_HEADS, HEAD_DIM, w), F32) for w in WINDOWS))
    g0_spec = pl.BlockSpec((None, P16, a_blk, ATTN_W), lambda b, t: (b, 0, t, 0))
    g12_spec = pl.BlockSpec((None, P16, a_blk, 2 * ATTN_W), lambda b, t: (b, 0, t, 0))
    tail_blk = tuple(min(w, QKV_TILE) for w in WINDOWS)
    tail_first = tuple(n_tb - w // blk for w, blk in zip(WINDOWS, tail_blk))

    def tail_spec(blk, first):
        return pl.BlockSpec((None, 2, N_HEADS, HEAD_DIM, blk),
                            lambda b, t: (b, 0, 0, 0, jnp.maximum(t - first, 0)))

    norm_g, w_bf, perm, ones, qg, kg = common
    return pl.pallas_call(
        functools.partial(_qkv_prompt_kernel, tail_first=tail_first),
        grid=(batch, n_tb),
        in_specs=[pl.BlockSpec((QKV_TILE, D_MODEL), lambda b, t: (b * n_tb + t, 0)),
                  _const_spec((1, D_MODEL)),
                  pl.BlockSpec((D_MODEL, 9 * ATTN_W), lambda b, t: (0, 0), pipeline_mode=pl.Buffered(1)),
                  _const_spec((ROW_TILE, ROW_TILE)), _const_spec((ROW_TILE, ROW_TILE)), _const_spec((256, 256)),
                  _const_spec((1, ATTN_W)), _const_spec((1, ATTN_W)),
                  pl.BlockSpec((QKV_TILE, 128), lambda b, t: (t, 0)),
                  pl.BlockSpec((QKV_TILE, 128), lambda b, t: (t, 0))],
        out_specs=((g0_spec,) * 3 + (g12_spec,) * 3
                   + tuple(tail_spec(blk, first) for blk, first in zip(tail_blk, tail_first))),
        out_shape=out_shape,
        compiler_params=_cparams("arbitrary", "arbitrary"),
        name="qkv_prompt",
    )(x2d, norm_g, w_bf, perm, _perm_matrices()[1], ones, qg, kg, cos, sin)


def _qkv_sample(x2d, pos, common):
    rows = x2d.shape[0]
    cos, sin = _rope_tables(pos)
    spec = pl.BlockSpec((ROW_TILE, 3 * ATTN_W), lambda i: (i, 0))
    return pl.pallas_call(
        _qkv_sample_kernel,
        grid=(rows // ROW_TILE,),
        in_specs=[pl.BlockSpec((ROW_TILE, D_MODEL), lambda i: (i, 0)),
                  _const_spec((1, D_MODEL)), _const_spec((D_MODEL, 9 * ATTN_W)),
                  _const_spec((ROW_TILE, ROW_TILE)), _const_spec((256, 256)),
                  _const_spec((1, ATTN_W)), _const_spec((1, ATTN_W)),
                  pl.BlockSpec((ROW_TILE, 128), lambda i: (i, 0)),
                  pl.BlockSpec((ROW_TILE, 128), lambda i: (i, 0))],
        out_specs=(spec,) * 3,
        out_shape=(jax.ShapeDtypeStruct((rows, 3 * ATTN_W), F32),) * 3,
        compiler_params=_cparams("arbitrary"),
        name="qkv_sample",
    )(x2d, *common, cos, sin)


ATTN_Q = 128


def _attn_geometry(dil):
    nres = P16 // dil
    qa = ATTN_Q // nres
    return nres, qa


def _attn_bias_tables(dil):
    nres, qa = _attn_geometry(dil)
    m = np.arange(ATTN_Q)
    n = np.arange(2 * ATTN_Q)
    qoff = nres * (m % qa) + m // qa
    koff = nres * (n % (2 * qa)) + n // (2 * qa)
    tables = []
    for delta in (0, qa):
        dist = qoff[:, None] - koff[None, :] + nres * delta
        bias = np.where((dist >= 0) & (dist <= KEYS_BACK), 0.0, NEG_BIG).astype(np.float32)
        tables.append(np.concatenate([bias, bias], axis=0))
    return jnp.asarray(np.stack(tables))


def _attn_group(bias_ref, q_ref, k_ref, v_ref, o_ref, l_ref, m_ref, dil, n_a):
    nres, qa = _attn_geometry(dil)
    n_atiles = n_a // qa
    head1 = lax.broadcasted_iota(jnp.int32, (ATTN_Q, 128), 1) >= HEAD_DIM
    zeros = jnp.zeros((ATTN_Q, 128), BF16)
    ones = jnp.ones((2 * ATTN_Q, 128), BF16)

    def tile(t, carry):
        c = t // n_atiles
        at = t % n_atiles
        a0 = pl.multiple_of(at * qa, qa)
        k0 = pl.multiple_of(jnp.maximum(at * qa - qa, 0), qa)
        rows = [c + dil * b for b in range(nres)]
        q = jnp.concatenate([q_ref[r, pl.ds(a0, qa), :] for r in rows], axis=0).astype(BF16)
        k = jnp.concatenate([k_ref[r, pl.ds(k0, 2 * qa), :] for r in rows], axis=0).astype(BF16)
        v = jnp.concatenate([v_ref[r, pl.ds(k0, 2 * qa), :] for r in rows], axis=0).astype(BF16)
        q = q * jnp.asarray(HEAD_DIM ** -0.5, BF16)
        q2 = jnp.concatenate([jnp.where(head1, zeros, q), jnp.where(head1, q, zeros)], axis=0)
        s = lax.dot_general(q2, k, (((1,), (1,)), ((), ())), preferred_element_type=F32)
        s = s + bias_ref[jnp.minimum(at, 1)]
        mx = jnp.max(s, axis=-1, keepdims=True)
        p = jnp.exp(s - mx).astype(BF16)
        ov = jnp.dot(p, jnp.concatenate([v, ones], axis=1), preferred_element_type=F32)
        o = jnp.where(head1, ov[ATTN_Q:, :128], ov[:ATTN_Q, :128])
        den = jnp.where(head1, ov[ATTN_Q:, 128:], ov[:ATTN_Q, 128:])
        m = jnp.where(head1, mx[ATTN_Q:], mx[:ATTN_Q])
        for b, r in enumerate(rows):
            o_ref[r, pl.ds(a0, qa), :] = o[b * qa:(b + 1) * qa]
            l_ref[r, pl.ds(a0, qa), :] = den[b * qa:(b + 1) * qa]
            m_ref[r, pl.ds(a0, qa), :] = m[b * qa:(b + 1) * qa]
        return carry

    lax.fori_loop(0, dil * n_atiles, tile, 0, unroll=32)


def _attn_prompt_kernel(b0, b1, b2, q0, k0, v0, q1, k1, v1, q2, k2, v2, o_ref, og_ref, lg_ref, mg_ref, *, n_a):
    groups = ((b0, q0, k0, v0), (b1, q1, k1, v1), (b2, q2, k2, v2))
    for g, (bias_ref, q_ref, k_ref, v_ref) in enumerate(groups):
        _attn_group(bias_ref, q_ref, k_ref, v_ref, og_ref.at[g], lg_ref.at[g], mg_ref.at[g], DILATIONS[g], n_a)

    def merge(r, carry):
        ms = [mg_ref[g, r] for g in range(3)]
        mx = jnp.maximum(jnp.maximum(ms[0], ms[1]), ms[2])
        es = [jnp.exp(m - mx) for m in ms]
        num = es[0] * og_ref[0, r] + es[1] * og_ref[1, r] + es[2] * og_ref[2, r]
        den = es[0] * lg_ref[0, r] + es[1] * lg_ref[1, r] + es[2] * lg_ref[2, r]
        o_ref[r] = (num / den).astype(BF16)
        return carry

    lax.fori_loop(0, P16, merge, 0)


def _attn_prompt(qkv0, qkv12):
    batch, _, n_a, _ = qkv0[0].shape
    biases = [_attn_bias_tables(dil) for dil in DILATIONS]
    in_specs = [_const_spec(b.shape) for b in biases]
    operands = list(biases)
    for g in range(3):
        in_specs += [pl.BlockSpec((None, P16, n_a, 128), lambda b, h, g=g: (b, 0, 0, max(g - 1, 0) * 4 + h))] * 3
        operands += list(qkv0 if g == 0 else qkv12)
    return pl.pallas_call(
        functools.partial(_attn_prompt_kernel, n_a=n_a),
        grid=(batch, 4),
        in_specs=in_specs,
        out_specs=pl.BlockSpec((None, P16, n_a, 128), lambda b, h: (b, 0, 0, h)),
        out_shape=jax.ShapeDtypeStruct((batch, P16, n_a, ATTN_W), BF16),
        scratch_shapes=[pltpu.VMEM((3, P16, n_a, 128), F32)] * 3,
        compiler_params=_cparams("arbitrary", "arbitrary"),
        name="attn_prompt",
    )(*operands)


def _attn_sample_item(q_ref, n_ref, c0_ref, c1_ref, c2_ref, o_ref, nc0_ref, nc1_ref, nc2_ref):
    n_new = q_ref.shape[2]
    n_h = q_ref.shape[1]
    assert n_h % 2 == 0 and n_new & (n_new - 1) == 0
    rows, width = n_h * n_new, n_h * HEAD_DIM
    nt = (((1,), (1,)), ((), ()))
    caches = ((c0_ref, nc0_ref), (c1_ref, nc1_ref), (c2_ref, nc2_ref))
    lane = lax.broadcasted_iota(jnp.int32, (HEAD_DIM, 128), 1)
    pad_rows = jnp.zeros((128 - n_new, 128), F32)
    row_r = lax.broadcasted_iota(jnp.int32, (rows, width), 0)
    lane_r = lax.broadcasted_iota(jnp.int32, (rows, width), 1)
    own = (lane_r // HEAD_DIM) == (row_r // n_new)
    row_n = lax.broadcasted_iota(jnp.int32, (rows, rows), 0)
    col_n = lax.broadcasted_iota(jnp.int32, (rows, rows), 1)
    dist_n = (row_n & (n_new - 1)) - (col_n & (n_new - 1))
    same_head = (row_n // n_new) == (col_n // n_new)
    zero_tile = jnp.zeros((n_new, 128), F32)

    def stacked(per_head):
        blocks = []
        for h, a in enumerate(per_head):
            tile = a if h % 2 == 0 else pltpu.roll(a, HEAD_DIM, 1)
            blocks.append(jnp.concatenate([tile if t == h // 2 else zero_tile for t in range(n_h // 2)], axis=1))
        return jnp.concatenate(blocks, axis=0)

    outs, lses = [], []
    for g, ((c_ref, nc_ref), dil) in enumerate(zip(caches, DILATIONS)):
        win = c_ref.shape[-1]
        dist_c = (win + (lax.broadcasted_iota(jnp.int32, (rows, win), 0) & (n_new - 1))
                  - lax.broadcasted_iota(jnp.int32, (rows, win), 1))
        ok_c = (dist_c <= win) & ((dist_c & (dil - 1)) == 0)
        ok_n = same_head & (dist_n >= 0) & ((dist_n & (dil - 1)) == 0)
        q2 = stacked([q_ref[g, h] for h in range(n_h)]) * (HEAD_DIM ** -0.5)
        kn2 = stacked([n_ref[g, 0, h] for h in range(n_h)])
        vn2 = stacked([n_ref[g, 1, h] for h in range(n_h)])
        kt2 = c_ref[0].reshape(width, win)
        vt2 = c_ref[1].reshape(width, win)
        s_c = jnp.dot(q2.astype(BF16), kt2.astype(BF16), preferred_element_type=F32)
        s_n = lax.dot_general(q2, kn2, nt, preferred_element_type=F32)
        s_c = jnp.where(ok_c, s_c, NEG_BIG)
        s_n = jnp.where(ok_n, s_n, NEG_BIG)
        mx = jnp.maximum(jnp.max(s_c, axis=-1, keepdims=True), jnp.max(s_n, axis=-1, keepdims=True))
        p_c = jnp.exp(s_c - mx)
        p_n = jnp.exp(s_n - mx)
        den = jnp.sum(p_c, axis=-1, keepdims=True) + jnp.sum(p_n, axis=-1, keepdims=True)
        o = (lax.dot_general(p_c.astype(BF16), vt2.astype(BF16), nt, preferred_element_type=F32)
             + jnp.dot(p_n, vn2, preferred_element_type=F32))
        outs.append(jnp.where(own, o, 0.0) / den)
        lses.append(mx + jnp.log(den))
        for kv in range(2):
            for h in range(n_h):
                rolled = pltpu.roll(c_ref[kv, h], win - n_new, 1)
                new_t = jnp.concatenate([pad_rows, n_ref[g, kv, h]], axis=0).T[:HEAD_DIM]
                if win > 128:
                    nc_ref[kv, h, :, :win - 128] = rolled[:, :win - 128]
                nc_ref[kv, h, :, win - 128:] = jnp.where(lane < 128 - n_new, rolled[:, win - 128:], new_t)
    mx = jnp.maximum(jnp.maximum(lses[0], lses[1]), lses[2])
    es = [jnp.exp(l - mx) for l in lses]
    o = (es[0] * outs[0] + es[1] * outs[1] + es[2] * outs[2]) / (es[0] + es[1] + es[2])
    for h in range(n_h):
        tile = o[h * n_new:(h + 1) * n_new, (h // 2) * 128:(h // 2 + 1) * 128]
        o_ref[h] = (tile if h % 2 == 0 else pltpu.roll(tile, HEAD_DIM, 1))[:, :HEAD_DIM]


def _ffn_block(x, g_ref, wg_ref, wu_ref, wd_ref, chunks):
    h = _rms(x, g_ref[...]).astype(BF16)
    acc = x
    for lo, hi in chunks:
        gate = jnp.dot(h, wg_ref[:, lo:hi], preferred_element_type=F32)
        up = jnp.dot(h, wu_ref[:, lo:hi], preferred_element_type=F32)
        act = (jax.nn.silu(gate) * up).astype(BF16)
        acc = acc + jnp.dot(act, wd_ref[lo:hi, :], preferred_element_type=F32)
    return acc


def _mixer_residual(mode, refs):
    if mode == "wo":
        o_ref, x_ref, wo_ref, permt_ref = refs
        blocks = []
        for i in range(o_ref.shape[1] // P16):
            o = o_ref[:, i * P16:(i + 1) * P16, :].astype(F32).reshape(ROW_TILE, ATTN_W).astype(BF16)
            blocks.append(jnp.dot(permt_ref[...], o, preferred_element_type=F32).astype(BF16))
        return x_ref[...] + jnp.dot(jnp.concatenate(blocks, axis=0), wo_ref[...], preferred_element_type=F32)
    if mode == "wo_rows":
        o_ref, x_ref, wo_ref = refs
        return x_ref[...] + jnp.dot(o_ref[...].astype(BF16), wo_ref[...], preferred_element_type=F32)
    a_ref, x_ref, wglu_ref = refs
    z = jnp.dot(a_ref[...], wglu_ref[...], preferred_element_type=F32)
    return x_ref[...] + z[:, :D_MODEL] * jax.nn.sigmoid(z[:, D_MODEL:])


_MIXER_INPUTS = {"wo": 4, "wo_rows": 3, "glu": 3}
def _mixer_ffn_kernel(*refs, mode, chunks, rider_inputs):
    n_pre = _MIXER_INPUTS[mode]
    g_ref, wg_ref, wu_ref, wd_ref = refs[n_pre:n_pre + 4]
    rest = refs[n_pre + 4:]
    y_ref = rest[rider_inputs]
    if rider_inputs:
        _attn_sample_item(*rest[:rider_inputs], *rest[rider_inputs + 1:])
    y_ref[...] = _ffn_block(_mixer_residual(mode, refs[:n_pre]), g_ref, wg_ref, wu_ref, wd_ref, chunks)


def _ffn(x2d, norm_g, wg_bf, wu_bf, wd_bf, layer, tm, pre, seq=None, rider=None):
    rows = x2d.shape[0]
    d_ff = wg_bf.shape[2]
    step = 1024
    chunks = tuple((lo, min(lo + step, d_ff)) for lo in range(0, d_ff, step))
    x_spec = pl.BlockSpec((tm, D_MODEL), lambda i: (i, 0))

    def resident(shape):
        return pl.BlockSpec(shape, lambda i: (0,) * len(shape), pipeline_mode=pl.Buffered(1))

    def of_layer(shape):
        return pl.BlockSpec((None,) + shape, lambda i: (layer, 0, 0), pipeline_mode=pl.Buffered(1))

    ffn_specs = [_const_spec((1, D_MODEL)), of_layer((D_MODEL, d_ff)), of_layer((D_MODEL, d_ff)),
                 of_layer((d_ff, D_MODEL))]
    ffn_args = (norm_g.reshape(1, D_MODEL), wg_bf, wu_bf, wd_bf)
    mode = pre[0]
    if mode == "wo":
        n_tb = seq // tm
        o_spec = pl.BlockSpec((None, P16, tm // P16, ATTN_W), lambda i: (i // n_tb, 0, i % n_tb, 0))
        specs = [o_spec, x_spec, resident((ATTN_W, D_MODEL)), _const_spec((ROW_TILE, ROW_TILE))]
        args = (pre[1], x2d, pre[2], _perm_matrices()[1])
    elif mode == "wo_rows":
        specs = [pl.BlockSpec((tm, ATTN_W), lambda i: (i, 0)), x_spec, resident((ATTN_W, D_MODEL))]
        args = (pre[1], x2d, pre[2])
    else:
        specs = [x_spec, x_spec, resident((D_MODEL, 2 * D_MODEL))]
        args = (pre[1], x2d, pre[2])
    out_specs = [x_spec]
    out_shape = [jax.ShapeDtypeStruct((rows, D_MODEL), F32)]
    rider_specs, rider_args = [], ()
    if rider is not None:
        q_p, new_p, caches_t = rider
        batch, _, _, n_new, _ = q_p.shape
        assert rows // tm == 2 * batch
        hh = N_HEADS // 2
        c_specs = [pl.BlockSpec((None, 2, hh, HEAD_DIM, c.shape[-1]), lambda i: (i // 2, 0, i % 2, 0, 0))
                   for c in caches_t]
        rider_specs = [pl.BlockSpec((None, 3, hh, n_new, 128), lambda i: (i // 2, 0, i % 2, 0, 0)),
                       pl.BlockSpec((None, 3, 2, hh, n_new, 128), lambda i: (i // 2, 0, 0, i % 2, 0, 0)),
                       *c_specs]
        rider_args = (q_p, new_p, *caches_t)
        out_specs += [pl.BlockSpec((None, hh, n_new, HEAD_DIM), lambda i: (i // 2, i % 2, 0, 0)), *c_specs]
        out_shape += [jax.ShapeDtypeStruct((batch, N_HEADS, n_new, HEAD_DIM), F32)]
        out_shape += [jax.ShapeDtypeStruct(c.shape, F32) for c in caches_t]
    outs = pl.pallas_call(
        functools.partial(_mixer_ffn_kernel, mode=mode, chunks=chunks, rider_inputs=len(rider_specs)),
        grid=(rows // tm,),
        in_specs=specs + ffn_specs + rider_specs,
        out_specs=out_specs,
        out_shape=out_shape,
        compiler_params=_cparams("arbitrary"),
        name=mode + "_ffn",
    )(*args, *ffn_args, *rider_args)
    return outs[0] if rider is None else outs


def _norm_matmul_kernel(x_ref, g_ref, w_ref, y_ref):
    h = _rms(x_ref[...], g_ref[...]).astype(BF16)
    y_ref[...] = jnp.dot(h, w_ref[...], preferred_element_type=F32).astype(y_ref.dtype)


def _norm_matmul(x2d, norm_g, w_bf, tm):
    rows = x2d.shape[0]
    n = w_bf.shape[1]
    return pl.pallas_call(
        _norm_matmul_kernel,
        grid=(rows // tm,),
        in_specs=[pl.BlockSpec((tm, D_MODEL), lambda i: (i, 0)), _const_spec((1, D_MODEL)),
                  _const_spec((D_MODEL, n))],
        out_specs=pl.BlockSpec((tm, n), lambda i: (i, 0)),
        out_shape=jax.ShapeDtypeStruct((rows, n), BF16),
        compiler_params=_cparams("arbitrary"),
        name="ssm_in_proj",
    )(x2d, norm_g.reshape(1, D_MODEL), w_bf)


SSM_LANE_TILE = 128
SSM_TOK_TILE = SSM_LANE_TILE * SSM_CHUNK


def _ssm_in_kernel(x_ref, g_ref, w_ref, perm_ref, ut_ref):
    n_blk = SSM_TOK_TILE // ROW_TILE
    blocks = []
    for kb in range(n_blk):
        h = _rms(x_ref[kb * ROW_TILE:(kb + 1) * ROW_TILE, :], g_ref[...]).astype(BF16)
        blocks.append(jnp.dot(perm_ref[...], h, preferred_element_type=F32).astype(BF16))
    hp = jnp.concatenate([blk[j * P16:(j + 1) * P16] for j in range(SSM_CHUNK) for blk in blocks], axis=0)
    u = jnp.dot(hp, w_ref[...], preferred_element_type=F32)
    for j in range(SSM_CHUNK):
        ut_ref[j] = u[j * SSM_LANE_TILE:(j + 1) * SSM_LANE_TILE].T.astype(BF16)


def _ssm_in(x2d, norm_g, w_bf, perm):
    rows = x2d.shape[0]
    n_tiles = rows // SSM_TOK_TILE
    return pl.pallas_call(
        _ssm_in_kernel,
        grid=(n_tiles,),
        in_specs=[pl.BlockSpec((SSM_TOK_TILE, D_MODEL), lambda i: (i, 0)), _const_spec((1, D_MODEL)),
                  pl.BlockSpec((D_MODEL, D_MODEL), lambda i: (0, 0), pipeline_mode=pl.Buffered(1)),
                  _const_spec((ROW_TILE, ROW_TILE))],
        out_specs=pl.BlockSpec((SSM_CHUNK, D_MODEL, SSM_LANE_TILE), lambda i: (0, 0, i)),
        out_shape=jax.ShapeDtypeStruct((SSM_CHUNK, D_MODEL, n_tiles * SSM_LANE_TILE), BF16),
        compiler_params=_cparams("arbitrary"),
        name="ssm_in_t",
    )(x2d, norm_g.reshape(1, D_MODEL), w_bf, perm)


def _ssm_gate_kernel(yt_ref, permt_ref, g_ref):
    n_blk = SSM_TOK_TILE // ROW_TILE
    pieces = []
    for i in range(SSM_CHUNK):
        pieces.append(jax.nn.gelu(yt_ref[i]).T.astype(BF16))
    for kb in range(n_blk):
        blk = jnp.concatenate([p[kb * P16:(kb + 1) * P16] for p in pieces], axis=0)
        g_ref[kb * ROW_TILE:(kb + 1) * ROW_TILE, :] = jnp.dot(
            permt_ref[...], blk, preferred_element_type=F32).astype(BF16)


def _ssm_gate(yt, permt):
    n_tiles = yt.shape[2] // SSM_LANE_TILE
    return pl.pallas_call(
        _ssm_gate_kernel,
        grid=(n_tiles,),
        in_specs=[pl.BlockSpec((SSM_CHUNK, D_MODEL, SSM_LANE_TILE), lambda i: (0, 0, i)),
                  _const_spec((ROW_TILE, ROW_TILE))],
        out_specs=pl.BlockSpec((SSM_TOK_TILE, D_MODEL), lambda i: (i, 0)),
        out_shape=jax.ShapeDtypeStruct((n_tiles * SSM_TOK_TILE, D_MODEL), BF16),
        compiler_params=_cparams("arbitrary"),
        name="ssm_gate_t",
    )(yt, permt)


def _ssm_core_kernel(u_ref, mt_ref, ft_ref, et_ref, a_ref, x0_ref, y_ref, xf_ref, *, n_chunks, has_init):
    chunk, _, n_lanes = u_ref.shape
    if n_chunks == 1:
        for gi in range(mt_ref.shape[0]):
            rows = pl.ds(gi * SSM_GROUP, SSM_GROUP)
            u = u_ref[:, rows, :].reshape(chunk * SSM_GROUP, n_lanes)
            x0 = x0_ref[gi]
            xre, xim = x0[:SSM_STATE], x0[SSM_STATE:]
            s = jnp.dot(ft_ref[gi], u, preferred_element_type=F32)
            are, aim = a_ref[gi, 0], a_ref[gi, 1]
            xf_ref[gi, :SSM_STATE, :] = are * xre - aim * xim + s[:SSM_STATE]
            xf_ref[gi, SSM_STATE:, :] = are * xim + aim * xre + s[SSM_STATE:]
            y = (jnp.dot(mt_ref[gi], u, preferred_element_type=F32)
                 + jnp.dot(et_ref[gi], x0.astype(BF16), preferred_element_type=F32))
            y_ref[:, rows, :] = y.reshape(chunk, SSM_GROUP, n_lanes)
    else:
        assert not has_init and n_chunks % 128 == 0 and mt_ref.shape[0] == 1
        u = u_ref[...].reshape(chunk * SSM_GROUP, n_lanes)
        y = jnp.dot(mt_ref[0], u, preferred_element_type=F32)
        s = jnp.dot(ft_ref[0], u, preferred_element_type=F32)
        sre, sim = s[:SSM_STATE], s[SSM_STATE:]
        are, aim = a_ref[0, 0], a_ref[0, 1]
        reps = n_lanes // 128
        pos = lax.broadcasted_iota(jnp.int32, (SSM_STATE, n_lanes), 1) & (n_chunks - 1)
        shift = 1
        while shift < n_chunks:
            keep = pos >= shift
            tre = jnp.where(keep, pltpu.roll(sre, shift, 1), 0.0)
            tim = jnp.where(keep, pltpu.roll(sim, shift, 1), 0.0)
            bre = jnp.concatenate([are] * reps, axis=1)
            bim = jnp.concatenate([aim] * reps, axis=1)
            sre, sim = sre + bre * tre - bim * tim, sim + bre * tim + bim * tre
            are, aim = are * are - aim * aim, 2.0 * are * aim
            shift *= 2
        lane128 = lax.broadcasted_iota(jnp.int32, (SSM_STATE, 128), 1)
        fre = jnp.zeros((SSM_STATE, 128), F32)
        fim = jnp.zeros((SSM_STATE, 128), F32)
        for b in range(n_lanes // n_chunks):
            lo, hi = (b + 1) * n_chunks - 128, (b + 1) * n_chunks
            last = lane128 == 127
            fre = jnp.where(lane128 == b, jnp.sum(jnp.where(last, sre[:, lo:hi], 0.0), axis=1, keepdims=True), fre)
            fim = jnp.where(lane128 == b, jnp.sum(jnp.where(last, sim[:, lo:hi], 0.0), axis=1, keepdims=True), fim)
        xf_ref[0, :SSM_STATE, :] = fre
        xf_ref[0, SSM_STATE:, :] = fim
        keep = pos >= 1
        xin = jnp.concatenate([jnp.where(keep, pltpu.roll(sre, 1, 1), 0.0),
                               jnp.where(keep, pltpu.roll(sim, 1, 1), 0.0)], axis=0)
        y = y + jnp.dot(et_ref[0], xin.astype(BF16), preferred_element_type=F32)
        y_ref[...] = y.reshape(chunk, SSM_GROUP, n_lanes)


def _ssm_core(ut, mats, x0t, n_chunks):
    mt, ft, et, a_pow = mats
    chunk, _, n_lanes = ut.shape
    groups = SSM_GROUPS
    has_init = x0t is not None
    a_lanes = n_lanes if n_chunks == 1 else 128
    a_b = jnp.broadcast_to(a_pow[:, :, :, None], a_pow.shape + (a_lanes,))
    if x0t is None:
        x0t = jnp.zeros((groups, 2 * SSM_STATE, 128), F32)
    xf_lanes = n_lanes if n_chunks == 1 else 128
    assert n_chunks == 1 or n_lanes // n_chunks <= 128
    gb = 8 if n_chunks == 1 else 1

    def gspec(shape):
        return pl.BlockSpec((gb,) + tuple(shape[1:]), lambda g: (g,) + (0,) * (len(shape) - 1))

    t_spec = pl.BlockSpec((chunk, gb * SSM_GROUP, n_lanes), lambda g: (0, g, 0))
    return pl.pallas_call(
        functools.partial(_ssm_core_kernel, n_chunks=n_chunks, has_init=has_init),
        grid=(groups // gb,),
        in_specs=[t_spec, gspec(mt.shape), gspec(ft.shape), gspec(et.shape), gspec(a_b.shape),
                  gspec(x0t.shape)],
        out_specs=(t_spec, gspec((groups, 2 * SSM_STATE, xf_lanes))),
        out_shape=(jax.ShapeDtypeStruct((chunk, D_MODEL, n_lanes), F32),
                   jax.ShapeDtypeStruct((groups, 2 * SSM_STATE, xf_lanes), F32)),
        compiler_params=_cparams("arbitrary"),
        name=f"ssm_core_{n_chunks}",
    )(ut, mt, ft, et, a_b, x0t)


def _cexp(mag_arg, ang):
    mag = jnp.exp(mag_arg)
    return mag * jnp.cos(ang), mag * jnp.sin(ang)


def _ssm_matrices_kernel(ar_row, ai_row, ar_col, ai_col, ld_ref, bre_ref, bim_ref, cre_ref, cim_ref,
                         d_ref, tile_ref, tile2_ref, mt_ref, ft_ref, et_ref, ap_ref, mth_ref, fth_ref, eth_ref,
                         *, chunk):
    hp = lax.Precision.HIGHEST
    rows = SSM_GROUP * chunk
    dt = jnp.exp(ld_ref[...])

    def powers(re, im, n):
        out = [(jnp.ones_like(re), jnp.zeros_like(im))]
        for _ in range(n):
            pr, pi = out[-1]
            out.append((pr * re - pi * im, pr * im + pi * re))
        return out

    pw = powers(*_cexp(ar_row[...] * dt, ai_row[...] * dt), chunk)
    cre = jnp.dot(cre_ref[...], tile2_ref[...], precision=hp, preferred_element_type=F32)
    cim = jnp.dot(cim_ref[...], tile2_ref[...], precision=hp, preferred_element_type=F32)
    cre, cim = jnp.concatenate([cre] * chunk, axis=0), jnp.concatenate([cim] * chunk, axis=0)

    def by_step(first):
        return tuple(jnp.concatenate([jnp.broadcast_to(pw[first + i][k], (SSM_GROUP, 128))
                                      for i in range(chunk)], axis=0) for k in range(2))

    pr, pi = by_step(0)
    r0_re, r0_im = (cre * pr - cim * pi)[:, :SSM_STATE], (cre * pi + cim * pr)[:, :SSM_STATE]
    pr, pi = by_step(1)
    lane = lax.broadcasted_iota(jnp.int32, (rows, 128), 1)
    et = jnp.where(lane < SSM_STATE, cre * pr - cim * pi, -(cre * pi + cim * pr)).astype(BF16)
    et_ref[...] = et
    eth_ref[...] = et[:rows // 2]
    for k, n in enumerate((chunk, chunk // 2)):
        ap_ref[2 * k:2 * k + 1, :] = pw[n][0]
        ap_ref[2 * k + 1:2 * k + 2, :] = pw[n][1]
    ar, ai = ar_col[...], ai_col[...]
    abr, abi = _cexp(ar * dt, ai * dt)
    inv = 1.0 / (ar * ar + ai * ai)
    zr = ((abr - 1.0) * ar + abi * ai) * inv
    zi = (abi * ar - (abr - 1.0) * ai) * inv
    bre, bim = bre_ref[...], bim_ref[...]
    bbr = jnp.dot(zr * bre - zi * bim, tile_ref[...], precision=hp, preferred_element_type=F32)
    bbi = jnp.dot(zr * bim + zi * bre, tile_ref[...], precision=hp, preferred_element_type=F32)
    pc = powers(abr, abi, chunk - 1)
    col_step = lax.broadcasted_iota(jnp.int32, (SSM_STATE, rows), 1) >> 4
    pr = jnp.zeros((SSM_STATE, rows), F32)
    pi = jnp.zeros((SSM_STATE, rows), F32)
    for j in range(chunk):
        pr = jnp.where(col_step == j, pc[chunk - 1 - j][0], pr)
        pi = jnp.where(col_step == j, pc[chunk - 1 - j][1], pi)
    ft = jnp.concatenate([pr * bbr - pi * bbi, pr * bbi + pi * bbr], axis=0).astype(BF16)
    ft_ref[...] = ft
    fth_ref[...] = ft[:, rows // 2:]
    tw = (jnp.dot(r0_re, bbr, precision=hp, preferred_element_type=F32)
          - jnp.dot(r0_im, bbi, precision=hp, preferred_element_type=F32))
    per_tile = 128 // SSM_GROUP
    row_idx = lax.broadcasted_iota(jnp.int32, (rows, 128), 0)
    col_idx = lax.broadcasted_iota(jnp.int32, (rows, 128), 1)
    row_blk, col_blk = row_idx >> 4, col_idx >> 4
    for lt in range(rows // 128):
        tw_lt = tw[:, lt * 128:(lt + 1) * 128]
        acc = jnp.zeros((rows, 128), F32)
        for jj in range(per_tile):
            j = lt * per_tile + jj
            shifted = tw_lt if j == 0 else pltpu.roll(tw_lt, SSM_GROUP * j, 0)
            acc = jnp.where(col_blk == jj, shifted, acc)
        keep = row_blk >= col_blk + lt * per_tile
        diag = row_idx == col_idx + lt * 128
        m_lt = (jnp.where(keep, acc, 0.0) + jnp.where(diag, d_ref[:, lt * 128:(lt + 1) * 128], 0.0)).astype(BF16)
        mt_ref[:, lt * 128:(lt + 1) * 128] = m_lt
        if lt == 0:
            mth_ref[...] = m_lt[:rows // 2]


def _ssm_matrices(a_re, a_im, log_dt, b_re, b_im, c_re, c_im, d_skip, chunk):
    groups, rows = SSM_GROUPS, SSM_GROUP * chunk
    half_rows = rows // 2

    def gspec(*shape):
        return pl.BlockSpec((None,) + shape, lambda g: (g,) + (0,) * len(shape))

    row2 = lambda a: jnp.tile(a, (1, 2)).reshape(groups, 1, 2 * SSM_STATE)
    col = lambda a: a.reshape(groups, SSM_STATE, 1)
    tile = jnp.asarray(np.tile(np.eye(SSM_GROUP, dtype=np.float32), (1, chunk)))
    tile2 = jnp.asarray(np.tile(np.eye(SSM_STATE, dtype=np.float32), (1, 2)))
    d_t = jnp.tile(d_skip.reshape(groups, 1, SSM_GROUP), (1, 1, chunk))
    mt, ft, et, a_pow, mt_h, ft_h, et_h = pl.pallas_call(
        functools.partial(_ssm_matrices_kernel, chunk=chunk),
        grid=(groups,),
        in_specs=[gspec(1, 128), gspec(1, 128), gspec(SSM_STATE, 1), gspec(SSM_STATE, 1), gspec(1, 1),
                  gspec(SSM_STATE, SSM_GROUP), gspec(SSM_STATE, SSM_GROUP),
                  gspec(SSM_GROUP, SSM_STATE), gspec(SSM_GROUP, SSM_STATE), gspec(1, rows),
                  _const_spec(tile.shape), _const_spec(tile2.shape)],
        out_specs=(gspec(rows, rows), gspec(2 * SSM_STATE, rows), gspec(rows, 2 * SSM_STATE), gspec(4, 128),
                   gspec(half_rows, half_rows), gspec(2 * SSM_STATE, half_rows), gspec(half_rows, 2 * SSM_STATE)),
        out_shape=(jax.ShapeDtypeStruct((groups, rows, rows), BF16),
                   jax.ShapeDtypeStruct((groups, 2 * SSM_STATE, rows), BF16),
                   jax.ShapeDtypeStruct((groups, rows, 2 * SSM_STATE), BF16),
                   jax.ShapeDtypeStruct((groups, 4, 128), F32),
                   jax.ShapeDtypeStruct((groups, half_rows, half_rows), BF16),
                   jax.ShapeDtypeStruct((groups, 2 * SSM_STATE, half_rows), BF16),
                   jax.ShapeDtypeStruct((groups, half_rows, 2 * SSM_STATE), BF16)),
        compiler_params=_cparams("arbitrary"),
        name="ssm_matrices",
    )(row2(a_re), row2(a_im), col(a_re), col(a_im), log_dt.reshape(groups, 1, 1),
      b_re, b_im, c_re, c_im, d_t, tile, tile2)
    a_pow = a_pow[:, :, :SSM_STATE]
    return (mt, ft, et, a_pow[:, 0:2]), (mt_h, ft_h, et_h, a_pow[:, 2:4])


def _gelu_kernel(y_ref, a_ref):
    a_ref[...] = jax.nn.gelu(y_ref[...]).astype(BF16)


def _gelu(y2d, tm):
    spec = pl.BlockSpec((tm, D_MODEL), lambda i: (i, 0))
    return pl.pallas_call(
        _gelu_kernel,
        grid=(y2d.shape[0] // tm,),
        in_specs=[spec],
        out_specs=spec,
        out_shape=jax.ShapeDtypeStruct(y2d.shape, BF16),
        compiler_params=_cparams("arbitrary"),
        name="ssm_gelu",
    )(y2d)


def _ssm_mixer_prompt(x2d, batch, seq, norm_g, w_in_bf, mats, perms):
    assert seq % SSM_TOK_TILE == 0
    n_chunks = seq // SSM_CHUNK
    ut = _ssm_in(x2d, norm_g, w_in_bf, perms[0])
    yt, xf = _ssm_core(ut, mats, None, n_chunks)
    xf = xf[:, :, :batch].reshape(SSM_GROUPS, 2, SSM_STATE, batch)
    return _ssm_gate(yt, perms[1]), xf.transpose(3, 0, 2, 1)


def _ssm_mixer_sample(x2d, batch, seq, norm_g, w_in_bf, mats, state0, tm):
    u = _norm_matmul(x2d, norm_g, w_in_bf, tm)
    ut = u.reshape(batch, seq, D_MODEL).transpose(1, 2, 0)
    x0t = state0.transpose(1, 3, 2, 0).reshape(SSM_GROUPS, 2 * SSM_STATE, batch)
    yt, xf = _ssm_core(ut, mats, x0t, 1)
    y = yt.transpose(2, 0, 1).reshape(batch * seq, D_MODEL)
    state = xf.reshape(SSM_GROUPS, 2, SSM_STATE, batch).transpose(3, 0, 2, 1)
    return _gelu(y, tm), state


def kernel(x_prompt, x_sample, cache_kv_w128, cache_kv_w512, cache_kv_w2048, state_ssm, norm_mix, norm_ffn,
           w_qkv, q_norm, k_norm, w_o, ssm_w_in, ssm_a_re, ssm_a_im, ssm_log_dt, ssm_b_re, ssm_b_im,
           ssm_c_re, ssm_c_im, ssm_d, ssm_w_glu, ffn_w_gate, ffn_w_up, ffn_w_down):
    batch, seq, _ = x_prompt.shape
    dec_batch, dec_seq, _ = x_sample.shape
    caches = (cache_kv_w128, cache_kv_w512, cache_kv_w2048)
    assert seq % ROW_TILE == 0 and seq >= WINDOWS[-1] and (dec_batch * dec_seq) % ROW_TILE == 0
    assert all(c.shape[2] == w for c, w in zip(caches, WINDOWS))
    xp = x_prompt.reshape(batch * seq, D_MODEL)
    xs = x_sample.reshape(dec_batch * dec_seq, D_MODEL)

    common = _qkv_common_inputs(norm_mix[0], w_qkv[0].astype(BF16), q_norm[0], k_norm[0])
    w_o_bf = w_o[0].astype(BF16)
    q0, k0, v0, q12, k12, v12, t0, t1, t2 = _qkv_prompt(xp, batch, seq, common)
    o_prompt = _attn_prompt((q0, k0, v0), (q12, k12, v12))
    kv_prompt = [tail.transpose(0, 4, 1, 2, 3)[None] for tail in (t0, t1, t2)]

    pos_s = jnp.tile(PAST_LEN + jnp.arange(dec_seq), dec_batch)
    qs, ks, vs = _qkv_sample(xs, pos_s, common)

    def heads_padded(a):
        a = a.reshape(dec_batch, dec_seq, 3, N_HEADS, HEAD_DIM).transpose(0, 2, 3, 1, 4)
        return jnp.pad(a, ((0, 0),) * 4 + ((0, 128 - HEAD_DIM),))

    q_p = heads_padded(qs)
    new_p = jnp.stack([heads_padded(ks), heads_padded(vs)], axis=2)
    caches_t = [c[0].transpose(0, 2, 3, 4, 1) for c in caches]

    ffn_w = (ffn_w_gate.astype(BF16), ffn_w_up.astype(BF16), ffn_w_down.astype(BF16))
    xp, o_s, *new_caches = _ffn(xp, norm_ffn[0], *ffn_w, 0, QKV_TILE, pre=("wo", o_prompt, w_o_bf), seq=seq,
                                rider=(q_p, new_p, caches_t))
    kv_sample = [nc.transpose(0, 4, 1, 2, 3)[None] for nc in new_caches]

    ssm_p = (ssm_a_re[0], ssm_a_im[0], ssm_log_dt[0], ssm_b_re[0], ssm_b_im[0], ssm_c_re[0], ssm_c_im[0])
    w_in_bf = ssm_w_in[0].astype(BF16)
    w_glu_bf = ssm_w_glu[0].astype(BF16)
    assert 2 * dec_seq == SSM_CHUNK
    mats_full, mats_half = _ssm_matrices(*ssm_p, ssm_d[0], SSM_CHUNK)
    act_p, st_p = _ssm_mixer_prompt(xp, batch, seq, norm_mix[1], w_in_bf, mats_full, _perm_matrices())
    xp = _ffn(xp, norm_ffn[1], *ffn_w, 1, QKV_TILE, pre=("glu", act_p, w_glu_bf))

    o_sample = o_s.transpose(0, 2, 1, 3).reshape(dec_batch * dec_seq, ATTN_W)
    xs = _ffn(xs, norm_ffn[0], *ffn_w, 0, ROW_TILE, pre=("wo_rows", o_sample, w_o_bf))
    act_s, st_s = _ssm_mixer_sample(xs, dec_batch, dec_seq, norm_mix[1], w_in_bf, mats_half, state_ssm[0],
                                    ROW_TILE)
    xs = _ffn(xs, norm_ffn[1], *ffn_w, 1, ROW_TILE, pre=("glu", act_s, w_glu_bf))

    return (xp.reshape(batch, seq, D_MODEL), xs.reshape(dec_batch, dec_seq, D_MODEL),
            kv_prompt[0], kv_prompt[1], kv_prompt[2], st_p[None],
            kv_sample[0], kv_sample[1], kv_sample[2], st_s[None])
```

```python
import functools
import math

import numpy as np
import jax
import jax.numpy as jnp
from jax import lax
from jax.experimental import pallas as pl
from jax.experimental.pallas import tpu as pltpu

F32 = jnp.float32
BF16 = jnp.bfloat16

D_MODEL = 1024
HEAD_DIM = 64
N_HEADS = 8
ATTN_W = N_HEADS * HEAD_DIM
DILATIONS = (1, 4, 16)
WINDOWS = (128, 512, 2048)
KEYS_BACK = 128
PAST_LEN = 16384
ROPE_THETA = 10000.0
RMS_EPS = 1e-6
SSM_GROUP = 16
SSM_GROUPS = D_MODEL // SSM_GROUP
SSM_STATE = 64
SSM_CHUNK = 16
ROW_TILE = 256
QKV_TILE = 512
P16 = 16
NEG_BIG = -1e30
VMEM_LIMIT = 56 * 1024 * 1024
STREAM_BUFFERS = 3


def _cparams(*sem):
    return pltpu.CompilerParams(dimension_semantics=tuple(sem), vmem_limit_bytes=VMEM_LIMIT)


def _const_spec(shape):
    nd = len(shape)
    return pl.BlockSpec(shape, lambda *_: (0,) * nd)


def _perm_matrices():
    m = np.arange(ROW_TILE)
    to_p16 = np.zeros((ROW_TILE, ROW_TILE), np.float32)
    to_p16[m, P16 * (m % P16) + m // P16] = 1.0
    return jnp.asarray(to_p16, BF16), jnp.asarray(to_p16.T, BF16)


def _rms(x, gain):
    ms = jnp.mean(x * x, axis=-1, keepdims=True)
    return x * lax.rsqrt(ms + RMS_EPS) * gain


class _QkvSlabs:
    def __init__(self, x_ref, g_ref, w_ref, perm_ref, ones_ref, qg_ref, kg_ref, cos_ref, sin_ref, permute):
        rows = x_ref.shape[0]
        h = _rms(x_ref[...], g_ref[...]).astype(BF16)
        if permute:
            h = jnp.concatenate(
                [jnp.dot(perm_ref[...], h[i:i + ROW_TILE], preferred_element_type=F32).astype(BF16)
                 for i in range(0, rows, ROW_TILE)], axis=0)
        self.h, self.w_ref, self.ones = h, w_ref, ones_ref[...]
        self.gains = (qg_ref, kg_ref)
        self.cos = jnp.concatenate([cos_ref[...]] * 4, axis=1)
        self.sin = jnp.concatenate([sin_ref[...]] * 4, axis=1)
        lane = lax.broadcasted_iota(jnp.int32, (rows, ATTN_W), 1)
        self.first_half = (lane & (HEAD_DIM - 1)) < (HEAD_DIM // 2)

    def slab(self, s, g):
        col = (s * 3 + g) * ATTN_W
        y = jnp.dot(self.h, self.w_ref[:, col:col + ATTN_W], preferred_element_type=F32)
        if s == 2:
            return y
        yy = (y * y).astype(BF16)
        ss = jnp.concatenate([jnp.dot(yy[:, :256], self.ones, preferred_element_type=F32),
                              jnp.dot(yy[:, 256:], self.ones, preferred_element_type=F32)], axis=1)
        yn = y * lax.rsqrt(ss * (1.0 / HEAD_DIM) + RMS_EPS) * self.gains[s][...]
        swapped = jnp.where(self.first_half,
                            pltpu.roll(yn, ATTN_W - HEAD_DIM // 2, 1),
                            pltpu.roll(yn, HEAD_DIM // 2, 1))
        return yn * self.cos + swapped * self.sin


def _split3_bf16(x):
    hi = x.astype(BF16)
    r1 = x - hi.astype(F32)
    mid = r1.astype(BF16)
    lo = (r1 - mid.astype(F32)).astype(BF16)
    return hi, mid, lo


def _qkv_prompt_kernel(x_ref, g_ref, w_ref, perm_ref, permt_ref, ones_ref, qg_ref, kg_ref, cos_ref, sin_ref,
                       q0_ref, k0_ref, v0_ref, q_ref, k_ref, v_ref, t0_ref, t1_ref, t2_ref, *, tail_first):
    slabs = _QkvSlabs(x_ref, g_ref, w_ref, perm_ref, ones_ref, qg_ref, kg_ref, cos_ref, sin_ref, True)
    rows = x_ref.shape[0]
    tails = (t0_ref, t1_ref, t2_ref)
    t = pl.program_id(1)
    for g in range(3):
        y = [slabs.slab(s, g) for s in range(3)]
        for s, (ref0, ref) in enumerate(((q0_ref, q_ref), (k0_ref, k_ref), (v0_ref, v_ref))):
            for i in range(rows // ROW_TILE):
                blk = y[s][i * ROW_TILE:(i + 1) * ROW_TILE].reshape(P16, P16, ATTN_W)
                if g == 0:
                    ref0[:, i * P16:(i + 1) * P16, :] = blk
                else:
                    ref[:, i * P16:(i + 1) * P16, (g - 1) * ATTN_W:g * ATTN_W] = blk.astype(BF16)

        @pl.when(t >= tail_first[g])
        def _(g=g, y=y):
            keep = tails[g].shape[-1]
            for i in range((rows - keep) // ROW_TILE, rows // ROW_TILE):
                kv = jnp.concatenate([y[1][i * ROW_TILE:(i + 1) * ROW_TILE],
                                      y[2][i * ROW_TILE:(i + 1) * ROW_TILE]], axis=1)
                nat = sum(jnp.dot(permt_ref[...], part, preferred_element_type=F32)
                          for part in _split3_bf16(kv))
                lo = max(i * ROW_TILE, rows - keep)
                width = (i + 1) * ROW_TILE - lo
                tails[g][:, :, :, lo - (rows - keep):lo - (rows - keep) + width] = (
                    nat[ROW_TILE - width:].T.reshape(2, N_HEADS, HEAD_DIM, width))


def _qkv_sample_kernel(x_ref, g_ref, w_ref, perm_ref, ones_ref, qg_ref, kg_ref, cos_ref, sin_ref,
                       q_ref, k_ref, v_ref):
    slabs = _QkvSlabs(x_ref, g_ref, w_ref, perm_ref, ones_ref, qg_ref, kg_ref, cos_ref, sin_ref, False)
    for g in range(3):
        lo, hi = g * ATTN_W, (g + 1) * ATTN_W
        for s, ref in enumerate((q_ref, k_ref, v_ref)):
            ref[:, lo:hi] = slabs.slab(s, g)


def _qkv_common_inputs(norm_g, w_qkv_bf, q_gain, k_gain):
    ones = np.kron(np.eye(4, dtype=np.float32), np.ones((HEAD_DIM, HEAD_DIM), np.float32))
    return (norm_g.reshape(1, D_MODEL), w_qkv_bf, _perm_matrices()[0], jnp.asarray(ones, BF16),
            jnp.tile(q_gain, N_HEADS).reshape(1, ATTN_W), jnp.tile(k_gain, N_HEADS).reshape(1, ATTN_W))


def _rope_tables(pos):
    half = HEAD_DIM // 2
    inv = ROPE_THETA ** (-jnp.arange(half, dtype=F32) / half)
    ang = pos.astype(F32)[:, None] * inv[None, :]
    cos, sin = jnp.cos(ang), jnp.sin(ang)
    cos = jnp.concatenate([cos, cos, cos, cos], axis=1)
    sin = jnp.concatenate([-sin, sin, -sin, sin], axis=1)
    return cos, sin


def _p16_positions(seq):
    t = np.arange(seq).reshape(seq // ROW_TILE, P16, P16)
    return jnp.asarray(t.transpose(0, 2, 1).reshape(seq))


def _qkv_prompt(x2d, batch, seq, common):
    n_tb = seq // QKV_TILE
    n_a = seq // P16
    a_blk = QKV_TILE // P16
    cos, sin = _rope_tables(_p16_positions(seq))
    out_shape = ((jax.ShapeDtypeStruct((batch, P16, n_a, ATTN_W), F32),) * 3
                 + (jax.ShapeDtypeStruct((batch, P16, n_a, 2 * ATTN_W), BF16),) * 3
                 + tuple(jax.ShapeDtypeStruct((batch, 2, N_HEADS, HEAD_DIM, w), F32) for w in WINDOWS))
    g0_spec = pl.BlockSpec((None, P16, a_blk, ATTN_W), lambda b, t: (b, 0, t, 0))
    g12_spec = pl.BlockSpec((None, P16, a_blk, 2 * ATTN_W), lambda b, t: (b, 0, t, 0))
    tail_blk = tuple(min(w, QKV_TILE) for w in WINDOWS)
    tail_first = tuple(n_tb - w // blk for w, blk in zip(WINDOWS, tail_blk))

    def tail_spec(blk, first):
        return pl.BlockSpec((None, 2, N_HEADS, HEAD_DIM, blk),
                            lambda b, t: (b, 0, 0, 0, jnp.maximum(t - first, 0)))

    norm_g, w_bf, perm, ones, qg, kg = common
    return pl.pallas_call(
        functools.partial(_qkv_prompt_kernel, tail_first=tail_first),
        grid=(batch, n_tb),
        in_specs=[pl.BlockSpec((QKV_TILE, D_MODEL), lambda b, t: (b * n_tb + t, 0)),
                  _const_spec((1, D_MODEL)),
                  pl.BlockSpec((D_MODEL, 9 * ATTN_W), lambda b, t: (0, 0), pipeline_mode=pl.Buffered(1)),
                  _const_spec((ROW_TILE, ROW_TILE)), _const_spec((ROW_TILE, ROW_TILE)), _const_spec((256, 256)),
                  _const_spec((1, ATTN_W)), _const_spec((1, ATTN_W)),
                  pl.BlockSpec((QKV_TILE, 128), lambda b, t: (t, 0)),
                  pl.BlockSpec((QKV_TILE, 128), lambda b, t: (t, 0))],
        out_specs=((g0_spec,) * 3 + (g12_spec,) * 3
                   + tuple(tail_spec(blk, first) for blk, first in zip(tail_blk, tail_first))),
        out_shape=out_shape,
        compiler_params=_cparams("arbitrary", "arbitrary"),
        name="qkv_prompt",
    )(x2d, norm_g, w_bf, perm, _perm_matrices()[1], ones, qg, kg, cos, sin)


def _qkv_sample(x2d, pos, common):
    rows = x2d.shape[0]
    cos, sin = _rope_tables(pos)
    spec = pl.BlockSpec((ROW_TILE, 3 * ATTN_W), lambda i: (i, 0))
    return pl.pallas_call(
        _qkv_sample_kernel,
        grid=(rows // ROW_TILE,),
        in_specs=[pl.BlockSpec((ROW_TILE, D_MODEL), lambda i: (i, 0)),
                  _const_spec((1, D_MODEL)), _const_spec((D_MODEL, 9 * ATTN_W)),
                  _const_spec((ROW_TILE, ROW_TILE)), _const_spec((256, 256)),
                  _const_spec((1, ATTN_W)), _const_spec((1, ATTN_W)),
                  pl.BlockSpec((ROW_TILE, 128), lambda i: (i, 0)),
                  pl.BlockSpec((ROW_TILE, 128), lambda i: (i, 0))],
        out_specs=(spec,) * 3,
        out_shape=(jax.ShapeDtypeStruct((rows, 3 * ATTN_W), F32),) * 3,
        compiler_params=_cparams("arbitrary"),
        name="qkv_sample",
    )(x2d, *common, cos, sin)


ATTN_Q = 128


def _attn_geometry(dil):
    nres = P16 // dil
    qa = ATTN_Q // nres
    return nres, qa


def _attn_bias_tables(dil):
    nres, qa = _attn_geometry(dil)
    m = np.arange(ATTN_Q)
    n = np.arange(2 * ATTN_Q)
    qoff = nres * (m % qa) + m // qa
    koff = nres * (n % (2 * qa)) + n // (2 * qa)
    tables = []
    for delta in (0, qa):
        dist = qoff[:, None] - koff[None, :] + nres * delta
        bias = np.where((dist >= 0) & (dist <= KEYS_BACK), 0.0, NEG_BIG).astype(np.float32)
        tables.append(np.concatenate([bias, bias], axis=0))
    return jnp.asarray(np.stack(tables))


def _attn_group(bias_ref, q_ref, k_ref, v_ref, o_ref, l_ref, m_ref, dil, n_a):
    nres, qa = _attn_geometry(dil)
    n_atiles = n_a // qa
    head1 = lax.broadcasted_iota(jnp.int32, (ATTN_Q, 128), 1) >= HEAD_DIM
    zeros = jnp.zeros((ATTN_Q, 128), BF16)
    ones = jnp.ones((2 * ATTN_Q, 128), BF16)

    def tile(t, carry):
        c = t // n_atiles
        at = t % n_atiles
        a0 = pl.multiple_of(at * qa, qa)
        k0 = pl.multiple_of(jnp.maximum(at * qa - qa, 0), qa)
        rows = [c + dil * b for b in range(nres)]
        q = jnp.concatenate([q_ref[r, pl.ds(a0, qa), :] for r in rows], axis=0).astype(BF16)
        k = jnp.concatenate([k_ref[r, pl.ds(k0, 2 * qa), :] for r in rows], axis=0).astype(BF16)
        v = jnp.concatenate([v_ref[r, pl.ds(k0, 2 * qa), :] for r in rows], axis=0).astype(BF16)
        q = q * jnp.asarray(HEAD_DIM ** -0.5, BF16)
        q2 = jnp.concatenate([jnp.where(head1, zeros, q), jnp.where(head1, q, zeros)], axis=0)
        s = lax.dot_general(q2, k, (((1,), (1,)), ((), ())), preferred_element_type=F32)
        s = s + bias_ref[jnp.minimum(at, 1)]
        mx = jnp.max(s, axis=-1, keepdims=True)
        p = jnp.exp(s - mx).astype(BF16)
        ov = jnp.dot(p, jnp.concatenate([v, ones], axis=1), preferred_element_type=F32)
        o = jnp.where(head1, ov[ATTN_Q:, :128], ov[:ATTN_Q, :128])
        den = jnp.where(head1, ov[ATTN_Q:, 128:], ov[:ATTN_Q, 128:])
        m = jnp.where(head1, mx[ATTN_Q:], mx[:ATTN_Q])
        for b, r in enumerate(rows):
            o_ref[r, pl.ds(a0, qa), :] = o[b * qa:(b + 1) * qa]
            l_ref[r, pl.ds(a0, qa), :] = den[b * qa:(b + 1) * qa]
            m_ref[r, pl.ds(a0, qa), :] = m[b * qa:(b + 1) * qa]
        return carry

    lax.fori_loop(0, dil * n_atiles, tile, 0, unroll=32)


def _attn_prompt_kernel(b0, b1, b2, q0, k0, v0, q1, k1, v1, q2, k2, v2, o_ref, og_ref, lg_ref, mg_ref, *, n_a):
    groups = ((b0, q0, k0, v0), (b1, q1, k1, v1), (b2, q2, k2, v2))
    for g, (bias_ref, q_ref, k_ref, v_ref) in enumerate(groups):
        _attn_group(bias_ref, q_ref, k_ref, v_ref, og_ref.at[g], lg_ref.at[g], mg_ref.at[g], DILATIONS[g], n_a)

    def merge(r, carry):
        ms = [mg_ref[g, r] for g in range(3)]
        mx = jnp.maximum(jnp.maximum(ms[0], ms[1]), ms[2])
        es = [jnp.exp(m - mx) for m in ms]
        num = es[0] * og_ref[0, r] + es[1] * og_ref[1, r] + es[2] * og_ref[2, r]
        den = es[0] * lg_ref[0, r] + es[1] * lg_ref[1, r] + es[2] * lg_ref[2, r]
        o_ref[r] = (num / den).astype(BF16)
        return carry

    lax.fori_loop(0, P16, merge, 0)


def _attn_prompt(qkv0, qkv12):
    batch, _, n_a, _ = qkv0[0].shape
    biases = [_attn_bias_tables(dil) for dil in DILATIONS]
    in_specs = [_const_spec(b.shape) for b in biases]
    operands = list(biases)
    for g in range(3):
        in_specs += [pl.BlockSpec((None, P16, n_a, 128), lambda b, h, g=g: (b, 0, 0, max(g - 1, 0) * 4 + h))] * 3
        operands += list(qkv0 if g == 0 else qkv12)
    return pl.pallas_call(
        functools.partial(_attn_prompt_kernel, n_a=n_a),
        grid=(batch, 4),
        in_specs=in_specs,
        out_specs=pl.BlockSpec((None, P16, n_a, 128), lambda b, h: (b, 0, 0, h)),
        out_shape=jax.ShapeDtypeStruct((batch, P16, n_a, ATTN_W), BF16),
        scratch_shapes=[pltpu.VMEM((3, P16, n_a, 128), F32)] * 3,
        compiler_params=_cparams("arbitrary", "arbitrary"),
        name="attn_prompt",
    )(*operands)


def _attn_sample_item(q_ref, n_ref, c0_ref, c1_ref, c2_ref, o_ref, nc0_ref, nc1_ref, nc2_ref):
    n_new = q_ref.shape[2]
    n_h = q_ref.shape[1]
    assert n_h % 2 == 0 and n_new & (n_new - 1) == 0
    rows, width = n_h * n_new, n_h * HEAD_DIM
    nt = (((1,), (1,)), ((), ()))
    caches = ((c0_ref, nc0_ref), (c1_ref, nc1_ref), (c2_ref, nc2_ref))
    lane = lax.broadcasted_iota(jnp.int32, (HEAD_DIM, 128), 1)
    pad_rows = jnp.zeros((128 - n_new, 128), F32)
    row_r = lax.broadcasted_iota(jnp.int32, (rows, width), 0)
    lane_r = lax.broadcasted_iota(jnp.int32, (rows, width), 1)
    own = (lane_r // HEAD_DIM) == (row_r // n_new)
    row_n = lax.broadcasted_iota(jnp.int32, (rows, rows), 0)
    col_n = lax.broadcasted_iota(jnp.int32, (rows, rows), 1)
    dist_n = (row_n & (n_new - 1)) - (col_n & (n_new - 1))
    same_head = (row_n // n_new) == (col_n // n_new)
    zero_tile = jnp.zeros((n_new, 128), F32)

    def stacked(per_head):
        blocks = []
        for h, a in enumerate(per_head):
            tile = a if h % 2 == 0 else pltpu.roll(a, HEAD_DIM, 1)
            blocks.append(jnp.concatenate([tile if t == h // 2 else zero_tile for t in range(n_h // 2)], axis=1))
        return jnp.concatenate(blocks, axis=0)

    outs, lses = [], []
    for g, ((c_ref, nc_ref), dil) in enumerate(zip(caches, DILATIONS)):
        win = c_ref.shape[-1]
        dist_c = (win + (lax.broadcasted_iota(jnp.int32, (rows, win), 0) & (n_new - 1))
                  - lax.broadcasted_iota(jnp.int32, (rows, win), 1))
        ok_c = (dist_c <= win) & ((dist_c & (dil - 1)) == 0)
        ok_n = same_head & (dist_n >= 0) & ((dist_n & (dil - 1)) == 0)
        q2 = stacked([q_ref[g, h] for h in range(n_h)]) * (HEAD_DIM ** -0.5)
        kn2 = stacked([n_ref[g, 0, h] for h in range(n_h)])
        vn2 = stacked([n_ref[g, 1, h] for h in range(n_h)])
        kt2 = c_ref[0].reshape(width, win)
        vt2 = c_ref[1].reshape(width, win)
        s_c = jnp.dot(q2.astype(BF16), kt2.astype(BF16), preferred_element_type=F32)
        s_n = lax.dot_general(q2, kn2, nt, preferred_element_type=F32)
        s_c = jnp.where(ok_c, s_c, NEG_BIG)
        s_n = jnp.where(ok_n, s_n, NEG_BIG)
        mx = jnp.maximum(jnp.max(s_c, axis=-1, keepdims=True), jnp.max(s_n, axis=-1, keepdims=True))
        p_c = jnp.exp(s_c - mx)
        p_n = jnp.exp(s_n - mx)
        den = jnp.sum(p_c, axis=-1, keepdims=True) + jnp.sum(p_n, axis=-1, keepdims=True)
        o = (lax.dot_general(p_c.astype(BF16), vt2.astype(BF16), nt, preferred_element_type=F32)
             + jnp.dot(p_n, vn2, preferred_element_type=F32))
        outs.append(jnp.where(own, o, 0.0) / den)
        lses.append(mx + jnp.log(den))
        for kv in range(2):
            for h in range(n_h):
                rolled = pltpu.roll(c_ref[kv, h], win - n_new, 1)
                new_t = jnp.concatenate([pad_rows, n_ref[g, kv, h]], axis=0).T[:HEAD_DIM]
                if win > 128:
                    nc_ref[kv, h, :, :win - 128] = rolled[:, :win - 128]
                nc_ref[kv, h, :, win - 128:] = jnp.where(lane < 128 - n_new, rolled[:, win - 128:], new_t)
    mx = jnp.maximum(jnp.maximum(lses[0], lses[1]), lses[2])
    es = [jnp.exp(l - mx) for l in lses]
    o = (es[0] * outs[0] + es[1] * outs[1] + es[2] * outs[2]) / (es[0] + es[1] + es[2])
    for h in range(n_h):
        tile = o[h * n_new:(h + 1) * n_new, (h // 2) * 128:(h // 2 + 1) * 128]
        o_ref[h] = (tile if h % 2 == 0 else pltpu.roll(tile, HEAD_DIM, 1))[:, :HEAD_DIM]


def _ffn_block(x, g_ref, wg_ref, wu_ref, wd_ref, chunks):
    h = _rms(x, g_ref[...]).astype(BF16)
    acc = x
    for lo, hi in chunks:
        gate = jnp.dot(h, wg_ref[:, lo:hi], preferred_element_type=F32)
        up = jnp.dot(h, wu_ref[:, lo:hi], preferred_element_type=F32)
        act = (jax.nn.silu(gate) * up).astype(BF16)
        acc = acc + jnp.dot(act, wd_ref[lo:hi, :], preferred_element_type=F32)
    return acc


def _mixer_residual(mode, refs):
    if mode == "wo":
        o_ref, x_ref, wo_ref, permt_ref = refs
        blocks = []
        for i in range(o_ref.shape[1] // P16):
            o = o_ref[:, i * P16:(i + 1) * P16, :].astype(F32).reshape(ROW_TILE, ATTN_W).astype(BF16)
            blocks.append(jnp.dot(permt_ref[...], o, preferred_element_type=F32).astype(BF16))
        return x_ref[...] + jnp.dot(jnp.concatenate(blocks, axis=0), wo_ref[...], preferred_element_type=F32)
    if mode == "wo_rows":
        o_ref, x_ref, wo_ref = refs
        return x_ref[...] + jnp.dot(o_ref[...].astype(BF16), wo_ref[...], preferred_element_type=F32)
    a_ref, x_ref, wglu_ref = refs
    z = jnp.dot(a_ref[...], wglu_ref[...], preferred_element_type=F32)
    return x_ref[...] + z[:, :D_MODEL] * jax.nn.sigmoid(z[:, D_MODEL:])


_MIXER_INPUTS = {"wo": 4, "wo_rows": 3, "glu": 3}
def _mixer_ffn_kernel(*refs, mode, chunks, rider_inputs):
    n_pre = _MIXER_INPUTS[mode]
    g_ref, wg_ref, wu_ref, wd_ref = refs[n_pre:n_pre + 4]
    rest = refs[n_pre + 4:]
    y_ref = rest[rider_inputs]
    if rider_inputs:
        _attn_sample_item(*rest[:rider_inputs], *rest[rider_inputs + 1:])
    y_ref[...] = _ffn_block(_mixer_residual(mode, refs[:n_pre]), g_ref, wg_ref, wu_ref, wd_ref, chunks)


def _ffn(x2d, norm_g, wg_bf, wu_bf, wd_bf, layer, tm, pre, seq=None, rider=None):
    rows = x2d.shape[0]
    d_ff = wg_bf.shape[2]
    step = 1024
    chunks = tuple((lo, min(lo + step, d_ff)) for lo in range(0, d_ff, step))
    x_spec = pl.BlockSpec((tm, D_MODEL), lambda i: (i, 0))

    def resident(shape):
        return pl.BlockSpec(shape, lambda i: (0,) * len(shape), pipeline_mode=pl.Buffered(1))

    def of_layer(shape):
        return pl.BlockSpec((None,) + shape, lambda i: (layer, 0, 0), pipeline_mode=pl.Buffered(1))

    ffn_specs = [_const_spec((1, D_MODEL)), of_layer((D_MODEL, d_ff)), of_layer((D_MODEL, d_ff)),
                 of_layer((d_ff, D_MODEL))]
    ffn_args = (norm_g.reshape(1, D_MODEL), wg_bf, wu_bf, wd_bf)
    mode = pre[0]
    if mode == "wo":
        n_tb = seq // tm
        o_spec = pl.BlockSpec((None, P16, tm // P16, ATTN_W), lambda i: (i // n_tb, 0, i % n_tb, 0))
        specs = [o_spec, x_spec, resident((ATTN_W, D_MODEL)), _const_spec((ROW_TILE, ROW_TILE))]
        args = (pre[1], x2d, pre[2], _perm_matrices()[1])
    elif mode == "wo_rows":
        specs = [pl.BlockSpec((tm, ATTN_W), lambda i: (i, 0)), x_spec, resident((ATTN_W, D_MODEL))]
        args = (pre[1], x2d, pre[2])
    else:
        specs = [x_spec, x_spec, resident((D_MODEL, 2 * D_MODEL))]
        args = (pre[1], x2d, pre[2])
    out_specs = [x_spec]
    out_shape = [jax.ShapeDtypeStruct((rows, D_MODEL), F32)]
    rider_specs, rider_args = [], ()
    if rider is not None:
        q_p, new_p, caches_t = rider
        batch, _, _, n_new, _ = q_p.shape
        assert rows // tm == 2 * batch
        hh = N_HEADS // 2
        c_specs = [pl.BlockSpec((None, 2, hh, HEAD_DIM, c.shape[-1]), lambda i: (i // 2, 0, i % 2, 0, 0))
                   for c in caches_t]
        rider_specs = [pl.BlockSpec((None, 3, hh, n_new, 128), lambda i: (i // 2, 0, i % 2, 0, 0)),
                       pl.BlockSpec((None, 3, 2, hh, n_new, 128), lambda i: (i // 2, 0, 0, i % 2, 0, 0)),
                       *c_specs]
        rider_args = (q_p, new_p, *caches_t)
        out_specs += [pl.BlockSpec((None, hh, n_new, HEAD_DIM), lambda i: (i // 2, i % 2, 0, 0)), *c_specs]
        out_shape += [jax.ShapeDtypeStruct((batch, N_HEADS, n_new, HEAD_DIM), F32)]
        out_shape += [jax.ShapeDtypeStruct(c.shape, F32) for c in caches_t]
    outs = pl.pallas_call(
        functools.partial(_mixer_ffn_kernel, mode=mode, chunks=chunks, rider_inputs=len(rider_specs)),
        grid=(rows // tm,),
        in_specs=specs + ffn_specs + rider_specs,
        out_specs=out_specs,
        out_shape=out_shape,
        compiler_params=_cparams("arbitrary"),
        name=mode + "_ffn",
    )(*args, *ffn_args, *rider_args)
    return outs[0] if rider is None else outs


def _norm_matmul_kernel(x_ref, g_ref, w_ref, y_ref):
    h = _rms(x_ref[...], g_ref[...]).astype(BF16)
    y_ref[...] = jnp.dot(h, w_ref[...], preferred_element_type=F32).astype(y_ref.dtype)


def _norm_matmul(x2d, norm_g, w_bf, tm):
    rows = x2d.shape[0]
    n = w_bf.shape[1]
    return pl.pallas_call(
        _norm_matmul_kernel,
        grid=(rows // tm,),
        in_specs=[pl.BlockSpec((tm, D_MODEL), lambda i: (i, 0)), _const_spec((1, D_MODEL)),
                  _const_spec((D_MODEL, n))],
        out_specs=pl.BlockSpec((tm, n), lambda i: (i, 0)),
        out_shape=jax.ShapeDtypeStruct((rows, n), BF16),
        compiler_params=_cparams("arbitrary"),
        name="ssm_in_proj",
    )(x2d, norm_g.reshape(1, D_MODEL), w_bf)


SSM_LANE_TILE = 128
SSM_TOK_TILE = SSM_LANE_TILE * SSM_CHUNK


def _ssm_in_kernel(x_ref, g_ref, w_ref, perm_ref, ut_ref):
    n_blk = SSM_TOK_TILE // ROW_TILE
    blocks = []
    for kb in range(n_blk):
        h = _rms(x_ref[kb * ROW_TILE:(kb + 1) * ROW_TILE, :], g_ref[...]).astype(BF16)
        blocks.append(jnp.dot(perm_ref[...], h, preferred_element_type=F32).astype(BF16))
    hp = jnp.concatenate([blk[j * P16:(j + 1) * P16] for j in range(SSM_CHUNK) for blk in blocks], axis=0)
    u = jnp.dot(hp, w_ref[...], preferred_element_type=F32)
    for j in range(SSM_CHUNK):
        ut_ref[j] = u[j * SSM_LANE_TILE:(j + 1) * SSM_LANE_TILE].T.astype(BF16)


def _ssm_in(x2d, norm_g, w_bf, perm):
    rows = x2d.shape[0]
    n_tiles = rows // SSM_TOK_TILE
    return pl.pallas_call(
        _ssm_in_kernel,
        grid=(n_tiles,),
        in_specs=[pl.BlockSpec((SSM_TOK_TILE, D_MODEL), lambda i: (i, 0)), _const_spec((1, D_MODEL)),
                  pl.BlockSpec((D_MODEL, D_MODEL), lambda i: (0, 0), pipeline_mode=pl.Buffered(1)),
                  _const_spec((ROW_TILE, ROW_TILE))],
        out_specs=pl.BlockSpec((SSM_CHUNK, D_MODEL, SSM_LANE_TILE), lambda i: (0, 0, i)),
        out_shape=jax.ShapeDtypeStruct((SSM_CHUNK, D_MODEL, n_tiles * SSM_LANE_TILE), BF16),
        compiler_params=_cparams("arbitrary"),
        name="ssm_in_t",
    )(x2d, norm_g.reshape(1, D_MODEL), w_bf, perm)


def _ssm_gate_kernel(yt_ref, permt_ref, g_ref):
    n_blk = SSM_TOK_TILE // ROW_TILE
    pieces = []
    for i in range(SSM_CHUNK):
        pieces.append(jax.nn.gelu(yt_ref[i]).T.astype(BF16))
    for kb in range(n_blk):
        blk = jnp.concatenate([p[kb * P16:(kb + 1) * P16] for p in pieces], axis=0)
        g_ref[kb * ROW_TILE:(kb + 1) * ROW_TILE, :] = jnp.dot(
            permt_ref[...], blk, preferred_element_type=F32).astype(BF16)


def _ssm_gate(yt, permt):
    n_tiles = yt.shape[2] // SSM_LANE_TILE
    return pl.pallas_call(
        _ssm_gate_kernel,
        grid=(n_tiles,),
        in_specs=[pl.BlockSpec((SSM_CHUNK, D_MODEL, SSM_LANE_TILE), lambda i: (0, 0, i)),
                  _const_spec((ROW_TILE, ROW_TILE))],
        out_specs=pl.BlockSpec((SSM_TOK_TILE, D_MODEL), lambda i: (i, 0)),
        out_shape=jax.ShapeDtypeStruct((n_tiles * SSM_TOK_TILE, D_MODEL), BF16),
        compiler_params=_cparams("arbitrary"),
        name="ssm_gate_t",
    )(yt, permt)


def _ssm_core_kernel(u_ref, mt_ref, ft_ref, et_ref, a_ref, x0_ref, y_ref, xf_ref, *, n_chunks, has_init):
    chunk, _, n_lanes = u_ref.shape
    if n_chunks == 1:
        for gi in range(mt_ref.shape[0]):
            rows = pl.ds(gi * SSM_GROUP, SSM_GROUP)
            u = u_ref[:, rows, :].reshape(chunk * SSM_GROUP, n_lanes)
            x0 = x0_ref[gi]
            xre, xim = x0[:SSM_STATE], x0[SSM_STATE:]
            s = jnp.dot(ft_ref[gi], u, preferred_element_type=F32)
            are, aim = a_ref[gi, 0], a_ref[gi, 1]
            xf_ref[gi, :SSM_STATE, :] = are * xre - aim * xim + s[:SSM_STATE]
            xf_ref[gi, SSM_STATE:, :] = are * xim + aim * xre + s[SSM_STATE:]
            y = (jnp.dot(mt_ref[gi], u, preferred_element_type=F32)
                 + jnp.dot(et_ref[gi], x0.astype(BF16), preferred_element_type=F32))
            y_ref[:, rows, :] = y.reshape(chunk, SSM_GROUP, n_lanes)
    else:
        assert not has_init and n_chunks % 128 == 0 and mt_ref.shape[0] == 1
        u = u_ref[...].reshape(chunk * SSM_GROUP, n_lanes)
        y = jnp.dot(mt_ref[0], u, preferred_element_type=F32)
        s = jnp.dot(ft_ref[0], u, preferred_element_type=F32)
        sre, sim = s[:SSM_STATE], s[SSM_STATE:]
        are, aim = a_ref[0, 0], a_ref[0, 1]
        reps = n_lanes // 128
        pos = lax.broadcasted_iota(jnp.int32, (SSM_STATE, n_lanes), 1) & (n_chunks - 1)
        shift = 1
        while shift < n_chunks:
            keep = pos >= shift
            tre = jnp.where(keep, pltpu.roll(sre, shift, 1), 0.0)
            tim = jnp.where(keep, pltpu.roll(sim, shift, 1), 0.0)
            bre = jnp.concatenate([are] * reps, axis=1)
            bim = jnp.concatenate([aim] * reps, axis=1)
            sre, sim = sre + bre * tre - bim * tim, sim + bre * tim + bim * tre
            are, aim = are * are - aim * aim, 2.0 * are * aim
            shift *= 2
        lane128 = lax.broadcasted_iota(jnp.int32, (SSM_STATE, 128), 1)
        fre = jnp.zeros((SSM_STATE, 128), F32)
        fim = jnp.zeros((SSM_STATE, 128), F32)
        for b in range(n_lanes // n_chunks):
            lo, hi = (b + 1) * n_chunks - 128, (b + 1) * n_chunks
            last = lane128 == 127
            fre = jnp.where(lane128 == b, jnp.sum(jnp.where(last, sre[:, lo:hi], 0.0), axis=1, keepdims=True), fre)
            fim = jnp.where(lane128 == b, jnp.sum(jnp.where(last, sim[:, lo:hi], 0.0), axis=1, keepdims=True), fim)
        xf_ref[0, :SSM_STATE, :] = fre
        xf_ref[0, SSM_STATE:, :] = fim
        keep = pos >= 1
        xin = jnp.concatenate([jnp.where(keep, pltpu.roll(sre, 1, 1), 0.0),
                               jnp.where(keep, pltpu.roll(sim, 1, 1), 0.0)], axis=0)
        y = y + jnp.dot(et_ref[0], xin.astype(BF16), preferred_element_type=F32)
        y_ref[...] = y.reshape(chunk, SSM_GROUP, n_lanes)


def _ssm_core(ut, mats, x0t, n_chunks):
    mt, ft, et, a_pow = mats
    chunk, _, n_lanes = ut.shape
    groups = SSM_GROUPS
    has_init = x0t is not None
    a_lanes = n_lanes if n_chunks == 1 else 128
    a_b = jnp.broadcast_to(a_pow[:, :, :, None], a_pow.shape + (a_lanes,))
    if x0t is None:
        x0t = jnp.zeros((groups, 2 * SSM_STATE, 128), F32)
    xf_lanes = n_lanes if n_chunks == 1 else 128
    assert n_chunks == 1 or n_lanes // n_chunks <= 128
    gb = 8 if n_chunks == 1 else 1

    def gspec(shape):
        return pl.BlockSpec((gb,) + tuple(shape[1:]), lambda g: (g,) + (0,) * (len(shape) - 1))

    t_spec = pl.BlockSpec((chunk, gb * SSM_GROUP, n_lanes), lambda g: (0, g, 0))
    body = functools.partial(_ssm_core_kernel, n_chunks=n_chunks, has_init=has_init)
    in_specs = [t_spec, gspec(mt.shape), gspec(ft.shape), gspec(et.shape), gspec(a_b.shape), gspec(x0t.shape)]
    out_specs = (t_spec, gspec((groups, 2 * SSM_STATE, xf_lanes)))
    out_shape = (jax.ShapeDtypeStruct((chunk, D_MODEL, n_lanes), F32),
                 jax.ShapeDtypeStruct((groups, 2 * SSM_STATE, xf_lanes), F32))
    operands = (ut, mt, ft, et, a_b, x0t)
    if n_chunks == 1:
        return pl.pallas_call(body, grid=(groups // gb,), in_specs=in_specs, out_specs=out_specs,
                              out_shape=out_shape, compiler_params=_cparams("arbitrary"),
                              name=f"ssm_core_{n_chunks}")(*operands)

    def deep(spec):
        return pl.BlockSpec(spec.block_shape, spec.index_map, pipeline_mode=pl.Buffered(STREAM_BUFFERS))

    def piped(*hbm_refs):
        pltpu.emit_pipeline(body, grid=(groups // gb,), in_specs=[deep(s) for s in in_specs],
                            out_specs=list(out_specs))(*hbm_refs)

    any_spec = pl.BlockSpec(memory_space=pl.ANY)
    return pl.pallas_call(piped, in_specs=[any_spec] * len(operands), out_specs=(any_spec, any_spec),
                          out_shape=out_shape, compiler_params=pltpu.CompilerParams(vmem_limit_bytes=VMEM_LIMIT),
                          name=f"ssm_core_{n_chunks}")(*operands)


def _cexp(mag_arg, ang):
    mag = jnp.exp(mag_arg)
    return mag * jnp.cos(ang), mag * jnp.sin(ang)


def _ssm_matrices_kernel(ar_row, ai_row, ar_col, ai_col, ld_ref, bre_ref, bim_ref, cre_ref, cim_ref,
                         d_ref, tile_ref, tile2_ref, mt_ref, ft_ref, et_ref, ap_ref, mth_ref, fth_ref, eth_ref,
                         *, chunk):
    hp = lax.Precision.HIGHEST
    rows = SSM_GROUP * chunk
    dt = jnp.exp(ld_ref[...])

    def powers(re, im, n):
        out = [(jnp.ones_like(re), jnp.zeros_like(im))]
        for _ in range(n):
            pr, pi = out[-1]
            out.append((pr * re - pi * im, pr * im + pi * re))
        return out

    pw = powers(*_cexp(ar_row[...] * dt, ai_row[...] * dt), chunk)
    cre = jnp.dot(cre_ref[...], tile2_ref[...], precision=hp, preferred_element_type=F32)
    cim = jnp.dot(cim_ref[...], tile2_ref[...], precision=hp, preferred_element_type=F32)
    cre, cim = jnp.concatenate([cre] * chunk, axis=0), jnp.concatenate([cim] * chunk, axis=0)

    def by_step(first):
        return tuple(jnp.concatenate([jnp.broadcast_to(pw[first + i][k], (SSM_GROUP, 128))
                                      for i in range(chunk)], axis=0) for k in range(2))

    pr, pi = by_step(0)
    r0_re, r0_im = (cre * pr - cim * pi)[:, :SSM_STATE], (cre * pi + cim * pr)[:, :SSM_STATE]
    pr, pi = by_step(1)
    lane = lax.broadcasted_iota(jnp.int32, (rows, 128), 1)
    et = jnp.where(lane < SSM_STATE, cre * pr - cim * pi, -(cre * pi + cim * pr)).astype(BF16)
    et_ref[...] = et
    eth_ref[...] = et[:rows // 2]
    for k, n in enumerate((chunk, chunk // 2)):
        ap_ref[2 * k:2 * k + 1, :] = pw[n][0]
        ap_ref[2 * k + 1:2 * k + 2, :] = pw[n][1]
    ar, ai = ar_col[...], ai_col[...]
    abr, abi = _cexp(ar * dt, ai * dt)
    inv = 1.0 / (ar * ar + ai * ai)
    zr = ((abr - 1.0) * ar + abi * ai) * inv
    zi = (abi * ar - (abr - 1.0) * ai) * inv
    bre, bim = bre_ref[...], bim_ref[...]
    bbr = jnp.dot(zr * bre - zi * bim, tile_ref[...], precision=hp, preferred_element_type=F32)
    bbi = jnp.dot(zr * bim + zi * bre, tile_ref[...], precision=hp, preferred_element_type=F32)
    pc = powers(abr, abi, chunk - 1)
    col_step = lax.broadcasted_iota(jnp.int32, (SSM_STATE, rows), 1) >> 4
    pr = jnp.zeros((SSM_STATE, rows), F32)
    pi = jnp.zeros((SSM_STATE, rows), F32)
    for j in range(chunk):
        pr = jnp.where(col_step == j, pc[chunk - 1 - j][0], pr)
        pi = jnp.where(col_step == j, pc[chunk - 1 - j][1], pi)
    ft = jnp.concatenate([pr * bbr - pi * bbi, pr * bbi + pi * bbr], axis=0).astype(BF16)
    ft_ref[...] = ft
    fth_ref[...] = ft[:, rows // 2:]
    tw = (jnp.dot(r0_re, bbr, precision=hp, preferred_element_type=F32)
          - jnp.dot(r0_im, bbi, precision=hp, preferred_element_type=F32))
    per_tile = 128 // SSM_GROUP
    row_idx = lax.broadcasted_iota(jnp.int32, (rows, 128), 0)
    col_idx = lax.broadcasted_iota(jnp.int32, (rows, 128), 1)
    row_blk, col_blk = row_idx >> 4, col_idx >> 4
    for lt in range(rows // 128):
        tw_lt = tw[:, lt * 128:(lt + 1) * 128]
        acc = jnp.zeros((rows, 128), F32)
        for jj in range(per_tile):
            j = lt * per_tile + jj
            shifted = tw_lt if j == 0 else pltpu.roll(tw_lt, SSM_GROUP * j, 0)
            acc = jnp.where(col_blk == jj, shifted, acc)
        keep = row_blk >= col_blk + lt * per_tile
        diag = row_idx == col_idx + lt * 128
        m_lt = (jnp.where(keep, acc, 0.0) + jnp.where(diag, d_ref[:, lt * 128:(lt + 1) * 128], 0.0)).astype(BF16)
        mt_ref[:, lt * 128:(lt + 1) * 128] = m_lt
        if lt == 0:
            mth_ref[...] = m_lt[:rows // 2]


def _ssm_matrices(a_re, a_im, log_dt, b_re, b_im, c_re, c_im, d_skip, chunk):
    groups, rows = SSM_GROUPS, SSM_GROUP * chunk
    half_rows = rows // 2

    def gspec(*shape):
        return pl.BlockSpec((None,) + shape, lambda g: (g,) + (0,) * len(shape))

    row2 = lambda a: jnp.tile(a, (1, 2)).reshape(groups, 1, 2 * SSM_STATE)
    col = lambda a: a.reshape(groups, SSM_STATE, 1)
    tile = jnp.asarray(np.tile(np.eye(SSM_GROUP, dtype=np.float32), (1, chunk)))
    tile2 = jnp.asarray(np.tile(np.eye(SSM_STATE, dtype=np.float32), (1, 2)))
    d_t = jnp.tile(d_skip.reshape(groups, 1, SSM_GROUP), (1, 1, chunk))
    mt, ft, et, a_pow, mt_h, ft_h, et_h = pl.pallas_call(
        functools.partial(_ssm_matrices_kernel, chunk=chunk),
        grid=(groups,),
        in_specs=[gspec(1, 128), gspec(1, 128), gspec(SSM_STATE, 1), gspec(SSM_STATE, 1), gspec(1, 1),
                  gspec(SSM_STATE, SSM_GROUP), gspec(SSM_STATE, SSM_GROUP),
                  gspec(SSM_GROUP, SSM_STATE), gspec(SSM_GROUP, SSM_STATE), gspec(1, rows),
                  _const_spec(tile.shape), _const_spec(tile2.shape)],
        out_specs=(gspec(rows, rows), gspec(2 * SSM_STATE, rows), gspec(rows, 2 * SSM_STATE), gspec(4, 128),
                   gspec(half_rows, half_rows), gspec(2 * SSM_STATE, half_rows), gspec(half_rows, 2 * SSM_STATE)),
        out_shape=(jax.ShapeDtypeStruct((groups, rows, rows), BF16),
                   jax.ShapeDtypeStruct((groups, 2 * SSM_STATE, rows), BF16),
                   jax.ShapeDtypeStruct((groups, rows, 2 * SSM_STATE), BF16),
                   jax.ShapeDtypeStruct((groups, 4, 128), F32),
                   jax.ShapeDtypeStruct((groups, half_rows, half_rows), BF16),
                   jax.ShapeDtypeStruct((groups, 2 * SSM_STATE, half_rows), BF16),
                   jax.ShapeDtypeStruct((groups, half_rows, 2 * SSM_STATE), BF16)),
        compiler_params=_cparams("arbitrary"),
        name="ssm_matrices",
    )(row2(a_re), row2(a_im), col(a_re), col(a_im), log_dt.reshape(groups, 1, 1),
      b_re, b_im, c_re, c_im, d_t, tile, tile2)
    a_pow = a_pow[:, :, :SSM_STATE]
    return (mt, ft, et, a_pow[:, 0:2]), (mt_h, ft_h, et_h, a_pow[:, 2:4])


def _gelu_kernel(y_ref, a_ref):
    a_ref[...] = jax.nn.gelu(y_ref[...]).astype(BF16)


def _gelu(y2d, tm):
    spec = pl.BlockSpec((tm, D_MODEL), lambda i: (i, 0))
    return pl.pallas_call(
        _gelu_kernel,
        grid=(y2d.shape[0] // tm,),
        in_specs=[spec],
        out_specs=spec,
        out_shape=jax.ShapeDtypeStruct(y2d.shape, BF16),
        compiler_params=_cparams("arbitrary"),
        name="ssm_gelu",
    )(y2d)


def _ssm_mixer_prompt(x2d, batch, seq, norm_g, w_in_bf, mats, perms):
    assert seq % SSM_TOK_TILE == 0
    n_chunks = seq // SSM_CHUNK
    ut = _ssm_in(x2d, norm_g, w_in_bf, perms[0])
    yt, xf = _ssm_core(ut, mats, None, n_chunks)
    xf = xf[:, :, :batch].reshape(SSM_GROUPS, 2, SSM_STATE, batch)
    return _ssm_gate(yt, perms[1]), xf.transpose(3, 0, 2, 1)


def _ssm_mixer_sample(x2d, batch, seq, norm_g, w_in_bf, mats, state0, tm):
    u = _norm_matmul(x2d, norm_g, w_in_bf, tm)
    ut = u.reshape(batch, seq, D_MODEL).transpose(1, 2, 0)
    x0t = state0.transpose(1, 3, 2, 0).reshape(SSM_GROUPS, 2 * SSM_STATE, batch)
    yt, xf = _ssm_core(ut, mats, x0t, 1)
    y = yt.transpose(2, 0, 1).reshape(batch * seq, D_MODEL)
    state = xf.reshape(SSM_GROUPS, 2, SSM_STATE, batch).transpose(3, 0, 2, 1)
    return _gelu(y, tm), state


def kernel(x_prompt, x_sample, cache_kv_w128, cache_kv_w512, cache_kv_w2048, state_ssm, norm_mix, norm_ffn,
           w_qkv, q_norm, k_norm, w_o, ssm_w_in, ssm_a_re, ssm_a_im, ssm_log_dt, ssm_b_re, ssm_b_im,
           ssm_c_re, ssm_c_im, ssm_d, ssm_w_glu, ffn_w_gate, ffn_w_up, ffn_w_down):
    batch, seq, _ = x_prompt.shape
    dec_batch, dec_seq, _ = x_sample.shape
    caches = (cache_kv_w128, cache_kv_w512, cache_kv_w2048)
    assert seq % ROW_TILE == 0 and seq >= WINDOWS[-1] and (dec_batch * dec_seq) % ROW_TILE == 0
    assert all(c.shape[2] == w for c, w in zip(caches, WINDOWS))
    xp = x_prompt.reshape(batch * seq, D_MODEL)
    xs = x_sample.reshape(dec_batch * dec_seq, D_MODEL)

    common = _qkv_common_inputs(norm_mix[0], w_qkv[0].astype(BF16), q_norm[0], k_norm[0])
    w_o_bf = w_o[0].astype(BF16)
    q0, k0, v0, q12, k12, v12, t0, t1, t2 = _qkv_prompt(xp, batch, seq, common)
    o_prompt = _attn_prompt((q0, k0, v0), (q12, k12, v12))
    kv_prompt = [tail.transpose(0, 4, 1, 2, 3)[None] for tail in (t0, t1, t2)]

    pos_s = jnp.tile(PAST_LEN + jnp.arange(dec_seq), dec_batch)
    qs, ks, vs = _qkv_sample(xs, pos_s, common)

    def heads_padded(a):
        a = a.reshape(dec_batch, dec_seq, 3, N_HEADS, HEAD_DIM).transpose(0, 2, 3, 1, 4)
        return jnp.pad(a, ((0, 0),) * 4 + ((0, 128 - HEAD_DIM),))

    q_p = heads_padded(qs)
    new_p = jnp.stack([heads_padded(ks), heads_padded(vs)], axis=2)
    caches_t = [c[0].transpose(0, 2, 3, 4, 1) for c in caches]

    ffn_w = (ffn_w_gate.astype(BF16), ffn_w_up.astype(BF16), ffn_w_down.astype(BF16))
    xp, o_s, *new_caches = _ffn(xp, norm_ffn[0], *ffn_w, 0, QKV_TILE, pre=("wo", o_prompt, w_o_bf), seq=seq,
                                rider=(q_p, new_p, caches_t))
    kv_sample = [nc.transpose(0, 4, 1, 2, 3)[None] for nc in new_caches]

    ssm_p = (ssm_a_re[0], ssm_a_im[0], ssm_log_dt[0], ssm_b_re[0], ssm_b_im[0], ssm_c_re[0], ssm_c_im[0])
    w_in_bf = ssm_w_in[0].astype(BF16)
    w_glu_bf = ssm_w_glu[0].astype(BF16)
    assert 2 * dec_seq == SSM_CHUNK
    mats_full, mats_half = _ssm_matrices(*ssm_p, ssm_d[0], SSM_CHUNK)
    act_p, st_p = _ssm_mixer_prompt(xp, batch, seq, norm_mix[1], w_in_bf, mats_full, _perm_matrices())
    xp = _ffn(xp, norm_ffn[1], *ffn_w, 1, QKV_TILE, pre=("glu", act_p, w_glu_bf))

    o_sample = o_s.transpose(0, 2, 1, 3).reshape(dec_batch * dec_seq, ATTN_W)
    xs = _ffn(xs, norm_ffn[0], *ffn_w, 0, ROW_TILE, pre=("wo_rows", o_sample, w_o_bf))
    act_s, st_s = _ssm_mixer_sample(xs, dec_batch, dec_seq, norm_mix[1], w_in_bf, mats_half, state_ssm[0],
                                    ROW_TILE)
    xs = _ffn(xs, norm_ffn[1], *ffn_w, 1, ROW_TILE, pre=("glu", act_s, w_glu_bf))

    return (xp.reshape(batch, seq, D_MODEL), xs.reshape(dec_batch, dec_seq, D_MODEL),
            kv_prompt[0], kv_prompt[1], kv_prompt[2], st_p[None],
            kv_sample[0], kv_sample[1], kv_sample[2], st_s[None])
```
